```python
import math
import jax, jax.numpy as jnp
from jax import lax
import numpy as np

D_MODEL = 2048
BATCH = 4
SEQ = 2048
DEPTH = 1
DEC_BATCH = 8
DEC_SEQ = 32
PAST_LEN = 1024

CHUNK = 64
QBLK = 128
N_MEM = 256
EPS = 1e-6

A_HEADS = 8
A_KV_HEADS = 2
A_HEAD_DIM = 128
A_WIDTH = A_HEADS * A_HEAD_DIM
A_KV_WIDTH = A_KV_HEADS * A_HEAD_DIM
IDX_HEADS = 16
IDX_DIM = 64
TOPK_MAX = 256
IDX_SCALE = (IDX_DIM * IDX_HEADS) ** -0.5

B_WIDTH = 2048
B_HEAD_DIM = 64
B_HEADS = B_WIDTH // B_HEAD_DIM
B_GROUPS = 4
B_HPG = B_HEADS // B_GROUPS
B_STATE = 128
B_CONV = 4
B_CONV_DIM = B_WIDTH + 2 * B_GROUPS * B_STATE

M_HEADS = 4
M_HEAD_DIM = 256
M_WIDTH = M_HEADS * M_HEAD_DIM

N_BRANCH = 3
IN_SPLITS = (A_WIDTH, A_KV_WIDTH, A_KV_WIDTH, IDX_HEADS * IDX_DIM, IDX_DIM, IDX_HEADS, A_WIDTH,
             B_WIDTH, B_CONV_DIM, B_HEADS, M_WIDTH, M_WIDTH, N_BRANCH * D_MODEL)
IN_COLS = sum(IN_SPLITS)

kernel_name = 'dsa_ssd_gated_parallel_streaming_step'


def rmsnorm(x, g):
    xf = x.astype(jnp.float32)
    y = xf * lax.rsqrt(jnp.mean(xf * xf, axis=-1, keepdims=True) + EPS)
    return (y * g.astype(jnp.float32)).astype(x.dtype)


def split_cols(u):
    idx, acc = [], 0
    for w in IN_SPLITS[:-1]:
        acc += w
        idx.append(acc)
    return jnp.split(u, idx, axis=-1)


def dsa_block(q, qi, wi, qpos, k, v, ki, kpos, topk):
    f32 = jnp.float32
    bsz, nq = q.shape[:2]
    rel = jax.nn.relu(jnp.einsum('bqhd,bsd->bqhs', qi.astype(f32), ki.astype(f32)))
    score = jnp.einsum('bqhs,bqh->bqs', rel, wi.astype(f32)) * IDX_SCALE
    adm = (kpos[None, :] // CHUNK) <= (qpos[:, None] // CHUNK)
    score = jnp.where(adm[None], score, -jnp.inf)
    vals, idx = lax.top_k(score, topk)
    valid = jnp.isfinite(vals)
    gather = jax.vmap(lambda t, i: t[i])
    kg = gather(k, idx).astype(f32)
    vg = gather(v, idx).astype(f32)
    qg = q.reshape(bsz, nq, A_KV_HEADS, A_HEADS // A_KV_HEADS, A_HEAD_DIM).astype(f32)
    s = jnp.einsum('bqhgd,bqkhd->bqhgk', qg, kg) * (A_HEAD_DIM ** -0.5)
    s = jnp.where(valid[:, :, None, None, :], s, -jnp.inf)
    p = jax.nn.softmax(s, axis=-1)
    o = jnp.einsum('bqhgk,bqkhd->bqhgd', p, vg)
    return o.reshape(bsz, nq, A_WIDTH).astype(q.dtype)


def dsa_attention(q, qi, wi, k, v, ki, pos0):
    bsz, T = q.shape[:2]
    L = k.shape[1]
    topk = min(TOPK_MAX, L // 4)
    blk = min(QBLK, T)
    nb = T // blk
    kpos = jnp.arange(L)
    qpos = (pos0 + jnp.arange(T)).reshape(nb, blk)

    def blocks(t):
        return jnp.moveaxis(t.reshape((bsz, nb, blk) + t.shape[2:]), 1, 0)

    def one_block(args):
        qb, qib, wib, qpb = args
        return dsa_block(qb, qib, wib, qpb, k, v, ki, kpos, topk)

    out = lax.map(one_block, (blocks(q), blocks(qi), blocks(wi), qpos))
    return jnp.moveaxis(out, 0, 1).reshape(bsz, T, A_WIDTH)


def causal_dwconv(u, prev, w, b):
    up = jnp.concatenate([prev, u], axis=1)
    y = lax.conv_general_dilated(up, w[:, None, :], window_strides=(1,), padding='VALID',
                                 dimension_numbers=('NWC', 'WIO', 'NWC'),
                                 feature_group_count=u.shape[-1])
    return y + b, up[:, up.shape[1] - (B_CONV - 1):]


def ssd_scan(xh, dt, a, bm, cm, h0):
    f32 = jnp.float32
    bsz, T = xh.shape[:2]
    l = min(CHUNK, T)
    c = T // l
    x = xh.reshape(bsz, c, l, B_GROUPS, B_HPG, B_HEAD_DIM).astype(f32)
    dtc = dt.reshape(bsz, c, l, B_GROUPS, B_HPG)
    bc = bm.reshape(bsz, c, l, B_GROUPS, B_STATE).astype(f32)
    cc = cm.reshape(bsz, c, l, B_GROUPS, B_STATE).astype(f32)
    cum = jnp.cumsum(dtc * a.reshape(B_GROUPS, B_HPG), axis=2)
    causal = jnp.tril(jnp.ones((l, l), dtype=bool))[:, :, None, None]
    seg = jnp.where(causal, cum[:, :, :, None] - cum[:, :, None, :], -jnp.inf)
    cb = jnp.einsum('bclgn,bcsgn->bclsg', cc, bc)
    wgt = cb[..., None] * jnp.exp(seg) * dtc[:, :, None]
    y_diag = jnp.einsum('bclsgr,bcsgrp->bclgrp', wgt, x)
    decay_to_end = jnp.exp(cum[:, :, -1:] - cum) * dtc
    states = jnp.einsum('bclgn,bclgr,bclgrp->bcgrpn', bc, decay_to_end, x)
    chunk_decay = jnp.exp(cum[:, :, -1])

    def step(h, inp):
        s_c, d_c = inp
        return h * d_c[..., None, None] + s_c, h

    h_init = h0.reshape(bsz, B_GROUPS, B_HPG, B_HEAD_DIM, B_STATE).astype(f32)
    h_last, h_in = lax.scan(step, h_init, (jnp.moveaxis(states, 1, 0), jnp.moveaxis(chunk_decay, 1, 0)))
    h_in = jnp.moveaxis(h_in, 0, 1)
    y_off = jnp.einsum('bclgn,bcgrpn,bclgr->bclgrp', cc, h_in, jnp.exp(cum))
    y = (y_diag + y_off).reshape(bsz, T, B_HEADS, B_HEAD_DIM)
    return y, h_last.reshape(bsz, B_HEADS, B_HEAD_DIM, B_STATE)


def mamba_branch(z, xbc, dt_raw, conv_prev, h0, conv_w, conv_b, dt_bias, a_log, d_skip, ssm_norm):
    f32 = jnp.float32
    bsz, T = z.shape[:2]
    xbc, conv_state = causal_dwconv(xbc, conv_prev, conv_w, conv_b)
    xbc = jax.nn.silu(xbc)
    gn = B_GROUPS * B_STATE
    xs = xbc[..., :B_WIDTH]
    bm = xbc[..., B_WIDTH:B_WIDTH + gn].reshape(bsz, T, B_GROUPS, B_STATE)
    cm = xbc[..., B_WIDTH + gn:].reshape(bsz, T, B_GROUPS, B_STATE)
    dt = jax.nn.softplus(dt_raw.astype(f32) + dt_bias.astype(f32))
    a = -jnp.exp(a_log.astype(f32))
    xh = xs.reshape(bsz, T, B_HEADS, B_HEAD_DIM)
    y, h_last = ssd_scan(xh, dt, a, bm, cm, h0)
    y = y + d_skip.astype(f32)[:, None] * xh.astype(f32)
    gsz = B_WIDTH // B_GROUPS
    y = y.reshape(bsz, T, B_GROUPS, gsz) * jax.nn.silu(z.astype(f32)).reshape(bsz, T, B_GROUPS, gsz)
    y = y * lax.rsqrt(jnp.mean(y * y, axis=-1, keepdims=True) + EPS)
    y = y.reshape(bsz, T, B_WIDTH) * ssm_norm.astype(f32)
    return y.astype(z.dtype), conv_state, h_last.astype(h0.dtype)


def mem_attend(q, mk, mv):
    f32 = jnp.float32
    bsz, T = q.shape[:2]
    qh = q.reshape(bsz, T, M_HEADS, M_HEAD_DIM).astype(f32)
    s = jnp.einsum('bthd,bmhd->bhtm', qh, mk.astype(f32)) * (M_HEAD_DIM ** -0.5)
    p = jax.nn.softmax(s, axis=-1)
    o = jnp.einsum('bhtm,bmhd->bthd', p, mv.astype(f32))
    return o.reshape(bsz, T, M_WIDTH).astype(q.dtype)


def trunk_layer(x, pos0, past_k, past_v, past_ki, conv_prev, h0, mem_k, mem_v,
                norm_in, w_in, conv_w, conv_b, dt_bias, a_log, d_skip, ssm_norm, w_pa, w_pb, w_pm, w_o):
    bsz, T, _ = x.shape
    h = rmsnorm(x, norm_in)
    (a_q, a_k, a_v, i_q, i_k, i_w, a_z, b_z, b_xbc, b_dt, m_q, m_z, gates) = split_cols(h @ w_in)
    k_new = a_k.reshape(bsz, T, A_KV_HEADS, A_HEAD_DIM)
    v_new = a_v.reshape(bsz, T, A_KV_HEADS, A_HEAD_DIM)
    k_all = jnp.concatenate([past_k, k_new], axis=1)
    v_all = jnp.concatenate([past_v, v_new], axis=1)
    ki_all = jnp.concatenate([past_ki, i_k], axis=1)
    ya = dsa_attention(a_q.reshape(bsz, T, A_HEADS, A_HEAD_DIM), i_q.reshape(bsz, T, IDX_HEADS, IDX_DIM),
                       i_w, k_all, v_all, ki_all, pos0)
    ya = ya * jax.nn.silu(a_z)
    yb, conv_state, h_last = mamba_branch(b_z, b_xbc, b_dt, conv_prev, h0, conv_w, conv_b,
                                          dt_bias, a_log, d_skip, ssm_norm)
    ym = mem_attend(m_q, mem_k, mem_v) * jax.nn.silu(m_z)
    g_a, g_b, g_m = jnp.split(jax.nn.sigmoid(gates), N_BRANCH, axis=-1)
    merged = g_a * (ya @ w_pa) + g_b * (yb @ w_pb) + g_m * (ym @ w_pm)
    return x + merged @ w_o, (k_new, v_new, i_k, conv_state, h_last)


def setup_inputs(seed: int = 0) -> dict:
    key = jax.random.key(seed)
    ks = jax.random.split(key, 32)
    f32 = jnp.float32

    def nrm(k, shape, scale):
        return scale * jax.random.normal(k, shape, f32)

    def gain(k, shape):
        return 1.0 + 0.02 * jax.random.normal(k, shape, f32)

    dt0 = jnp.exp(jax.random.uniform(ks[20], (DEPTH, B_HEADS), f32, math.log(1e-3), math.log(1e-1)))
    return {
        'x_prompt': nrm(ks[0], (BATCH, SEQ, D_MODEL), 1.0),
        'x_sample': nrm(ks[1], (DEC_BATCH, DEC_SEQ, D_MODEL), 1.0),
        'mem_prompt': nrm(ks[2], (BATCH, N_MEM, D_MODEL), 1.0),
        'cache_attn_k': nrm(ks[3], (DEPTH, DEC_BATCH, PAST_LEN, A_KV_HEADS, A_HEAD_DIM), 1.0),
        'cache_attn_v': nrm(ks[4], (DEPTH, DEC_BATCH, PAST_LEN, A_KV_HEADS, A_HEAD_DIM), 1.0),
        'cache_idx_k': nrm(ks[5], (DEPTH, DEC_BATCH, PAST_LEN, IDX_DIM), 1.0),
        'state_conv': nrm(ks[6], (DEPTH, DEC_BATCH, B_CONV - 1, B_CONV_DIM), 1.0),
        'state_ssm': nrm(ks[7], (DEPTH, DEC_BATCH, B_HEADS, B_HEAD_DIM, B_STATE), 0.1),
        'cache_mem_k': nrm(ks[8], (DEPTH, DEC_BATCH, N_MEM, M_HEADS, M_HEAD_DIM), 1.0),
        'cache_mem_v': nrm(ks[9], (DEPTH, DEC_BATCH, N_MEM, M_HEADS, M_HEAD_DIM), 1.0),
        'norm_in': gain(ks[10], (DEPTH, D_MODEL)),
        'w_in': nrm(ks[11], (DEPTH, D_MODEL, IN_COLS), D_MODEL ** -0.5),
        'conv_w': nrm(ks[12], (DEPTH, B_CONV, B_CONV_DIM), B_CONV ** -0.5),
        'conv_b': nrm(ks[13], (DEPTH, B_CONV_DIM), 0.02),
        'dt_bias': dt0 + jnp.log(-jnp.expm1(-dt0)),
        'a_log': jnp.log(jax.random.uniform(ks[14], (DEPTH, B_HEADS), f32, 1.0, 16.0)),
        'd_skip': gain(ks[15], (DEPTH, B_HEADS)),
        'ssm_norm': gain(ks[16], (DEPTH, B_WIDTH)),
        'norm_mem': gain(ks[17], (DEPTH, D_MODEL)),
        'w_mem_kv': nrm(ks[18], (DEPTH, D_MODEL, 2 * M_WIDTH), D_MODEL ** -0.5),
        'w_pa': nrm(ks[19], (DEPTH, A_WIDTH, D_MODEL), A_WIDTH ** -0.5),
        'w_pb': nrm(ks[21], (DEPTH, B_WIDTH, D_MODEL), B_WIDTH ** -0.5),
        'w_pm': nrm(ks[22], (DEPTH, M_WIDTH, D_MODEL), M_WIDTH ** -0.5),
        'w_o': nrm(ks[23], (DEPTH, D_MODEL, D_MODEL), D_MODEL ** -0.5),
        'norm_final': gain(ks[24], (D_MODEL,)),
    }


def reference(x_prompt, x_sample, mem_prompt, cache_attn_k, cache_attn_v, cache_idx_k, state_conv, state_ssm,
              cache_mem_k, cache_mem_v, norm_in, w_in, conv_w, conv_b, dt_bias, a_log, d_skip, ssm_norm,
              norm_mem, w_mem_kv, w_pa, w_pb, w_pm, w_o, norm_final):
    xp, xs = x_prompt, x_sample
    bp = xp.shape[0]
    new_p, new_s = [], []
    for l in range(DEPTH):
        lw = (norm_in[l], w_in[l], conv_w[l], conv_b[l], dt_bias[l], a_log[l], d_skip[l], ssm_norm[l],
              w_pa[l], w_pb[l], w_pm[l], w_o[l])
        mkv = rmsnorm(mem_prompt, norm_mem[l]) @ w_mem_kv[l]
        mk_p = mkv[..., :M_WIDTH].reshape(bp, N_MEM, M_HEADS, M_HEAD_DIM)
        mv_p = mkv[..., M_WIDTH:].reshape(bp, N_MEM, M_HEADS, M_HEAD_DIM)
        empty_kv = jnp.zeros((bp, 0, A_KV_HEADS, A_HEAD_DIM), xp.dtype)
        empty_ki = jnp.zeros((bp, 0, IDX_DIM), xp.dtype)
        conv0 = jnp.zeros((bp, B_CONV - 1, B_CONV_DIM), xp.dtype)
        h0 = jnp.zeros((bp, B_HEADS, B_HEAD_DIM, B_STATE), xp.dtype)
        xp, st_p = trunk_layer(xp, 0, empty_kv, empty_kv, empty_ki, conv0, h0, mk_p, mv_p, *lw)
        xs, st_s = trunk_layer(xs, PAST_LEN, cache_attn_k[l], cache_attn_v[l], cache_idx_k[l], state_conv[l],
                               state_ssm[l], cache_mem_k[l], cache_mem_v[l], *lw)
        new_p.append(st_p + (mk_p, mv_p))
        new_s.append(st_s)
    y_prompt = rmsnorm(xp, norm_final)
    y_sample = rmsnorm(xs, norm_final)
    sp = [jnp.stack(t) for t in zip(*new_p)]
    ss = [jnp.stack(t) for t in zip(*new_s)]
    return (y_prompt, y_sample, sp[0], sp[1], sp[2], sp[3], sp[4], sp[5], sp[6], ss[0], ss[1], ss[2], ss[3], ss[4])
```

```python
import functools

import jax
import jax.numpy as jnp
from jax import lax
from jax.experimental import pallas as pl
from jax.experimental.pallas import tpu as pltpu

F32 = jnp.float32
BF16 = jnp.bfloat16

D_MODEL = 2048
CHUNK = 64
CHUNK_SHIFT = 6
assert 1 << CHUNK_SHIFT == CHUNK
N_MEM = 256
EPS = 1e-6
PAST_LEN = 1024

A_HEADS = 8
A_KV_HEADS = 2
A_HEAD_DIM = 128
A_GROUP = A_HEADS // A_KV_HEADS
A_WIDTH = A_HEADS * A_HEAD_DIM
A_KV_WIDTH = A_KV_HEADS * A_HEAD_DIM
IDX_HEADS = 16
IDX_DIM = 64
TOPK_MAX = 256
IDX_SCALE = (IDX_DIM * IDX_HEADS) ** -0.5

B_WIDTH = 2048
B_HEAD_DIM = 64
B_HEADS = B_WIDTH // B_HEAD_DIM
B_GROUPS = 4
B_HPG = B_HEADS // B_GROUPS
B_STATE = 128
B_CONV = 4
B_CONV_DIM = B_WIDTH + 2 * B_GROUPS * B_STATE
B_GROUP_W = B_WIDTH // B_GROUPS

M_HEADS = 4
M_HEAD_DIM = 256
M_WIDTH = M_HEADS * M_HEAD_DIM

N_BRANCH = 3
IN_SPLITS = (A_WIDTH, A_KV_WIDTH, A_KV_WIDTH, IDX_HEADS * IDX_DIM, IDX_DIM, IDX_HEADS, A_WIDTH,
             B_WIDTH, B_CONV_DIM, B_HEADS, M_WIDTH, M_WIDTH, N_BRANCH * D_MODEL)
IN_COLS = sum(IN_SPLITS)
(SRC_AQ, SRC_AK, SRC_AV, SRC_IQ, SRC_IK, SRC_IW, SRC_AZ, SRC_BZ, SRC_XBC, SRC_DT, SRC_MQ, SRC_MZ,
 SRC_GATES) = (sum(IN_SPLITS[:i]) for i in range(len(IN_SPLITS)))

LANES = 128
SUBLANES = 8
VMEM_LIMIT_BYTES = 56 * 1024 * 1024

COL_GATES = 0
COL_XBC = 6144
COL_AQ = 9216
COL_IQ = 10240
COL_AZ = 11264
COL_MQ = 12288
COL_MZ = 13312
COL_BZ = 14336
COL_AK = 16384
COL_AV = 16640
COL_IKW = 16896
COL_DT = 17024
PACKED_COLS = 17408

PACK_SEGMENTS = ((SRC_GATES, COL_GATES, N_BRANCH * D_MODEL), (SRC_XBC, COL_XBC, B_CONV_DIM),
                 (SRC_AQ, COL_AQ, A_WIDTH), (SRC_IQ, COL_IQ, IDX_HEADS * IDX_DIM), (SRC_AZ, COL_AZ, A_WIDTH),
                 (SRC_MQ, COL_MQ, M_WIDTH), (SRC_MZ, COL_MZ, M_WIDTH), (SRC_BZ, COL_BZ, B_WIDTH),
                 (SRC_AK, COL_AK, A_KV_WIDTH), (SRC_AV, COL_AV, A_KV_WIDTH),
                 (SRC_IK, COL_IKW, IDX_DIM + IDX_HEADS), (SRC_DT, COL_DT, B_HEADS))

KEY_TILE = 256
PROJ_SUB_ROWS = 512
INT_MIN = -2 ** 31
KEY_NEG_INF = INT_MIN + 0x7FFFFF
NEG_BIG = -1e30


def _cparams(n_grid):
    return pltpu.CompilerParams(dimension_semantics=("arbitrary",) * n_grid,
                                vmem_limit_bytes=VMEM_LIMIT_BYTES)


def _silu(z):
    return z * (1.0 / (1.0 + jnp.exp(-z)))


def _sigmoid(z):
    return 1.0 / (1.0 + jnp.exp(-z))


def _dot(a, b):
    return jnp.dot(a, b, preferred_element_type=F32)


def _dot_nt(a, b):
    return lax.dot_general(a, b, (((1,), (1,)), ((), ())), preferred_element_type=F32)


def _pack_kernel(w_ref, o_ref):
    rows = o_ref.shape[0]
    chunk = 1024
    for src, dst, width in PACK_SEGMENTS:
        for off in range(0, width, chunk):
            w = min(chunk, width - off)
            s = src + off
            lo = (s // LANES) * LANES
            hi = min(-(-(s + w) // LANES) * LANES, IN_COLS)
            piece = w_ref[:, lo:hi][:, s - lo:s - lo + w]
            pad = -w % LANES
            if pad:
                piece = jnp.concatenate([piece, jnp.zeros((rows, pad), F32)], axis=1)
            o_ref[:, dst + off:dst + off + w + pad] = piece.astype(o_ref.dtype)
    tail = COL_DT + LANES
    o_ref[:, tail:PACKED_COLS] = jnp.zeros((rows, PACKED_COLS - tail), o_ref.dtype)


def _pack_w_in(w):
    rows = LANES
    return pl.pallas_call(
        _pack_kernel,
        grid=(D_MODEL // rows,),
        in_specs=[pl.BlockSpec((None, rows, IN_COLS), lambda i: (0, i, 0))],
        out_specs=pl.BlockSpec((rows, PACKED_COLS), lambda i: (i, 0)),
        out_shape=jax.ShapeDtypeStruct((D_MODEL, PACKED_COLS), BF16),
        compiler_params=_cparams(1),
        name="pack_w_in",
    )(w)


def _proj_kernel(x_ref, g_ref, w_ref, o_ref, h_ref):
    @pl.when(pl.program_id(1) == 0)
    def _():
        x = x_ref[...]
        ms = jnp.mean(x * x, axis=-1, keepdims=True)
        h_ref[...] = (x * lax.rsqrt(ms + EPS) * g_ref[...]).astype(BF16)

    tm = o_ref.shape[0]
    sub = min(tm, PROJ_SUB_ROWS)
    for r in range(tm // sub):
        o_ref[r * sub:(r + 1) * sub, :] = _dot(h_ref[r * sub:(r + 1) * sub, :], w_ref[...])


def _rms_proj(x, g, w, tm, tn):
    m, d = x.shape
    n = w.shape[1]
    return pl.pallas_call(
        _proj_kernel,
        grid=(m // tm, n // tn),
        in_specs=[pl.BlockSpec((tm, d), lambda i, j: (i, 0)),
                  pl.BlockSpec((1, d), lambda i, j: (0, 0)),
                  pl.BlockSpec((d, tn), lambda i, j: (0, j))],
        out_specs=pl.BlockSpec((tm, tn), lambda i, j: (i, j)),
        out_shape=jax.ShapeDtypeStruct((m, n), F32),
        scratch_shapes=[pltpu.VMEM((tm, d), BF16)],
        compiler_params=_cparams(2),
        name="rms_proj",
    )(x, g.reshape(1, d), w)


def _key_to_f32(key):
    bits = jnp.where(key >= 0, key, key ^ jnp.int32(0x7FFFFFFF))
    return pltpu.bitcast(bits, F32)


def _dsa_kernel(q_ref, iq_ref, ikw_ref, az_ref, k_ref, v_ref, kikw_ref, o_ref,
                kb_ref, vb_ref, kie_ref, kio_ref, iqb_ref, wt_ref, qs_ref, sc_ref, bias_ref,
                cnt_ref, gt_ref, m_ref, ans_ref, *pad_refs,
                tq, tq_in, nq, n_tiles, n_valid_keys, pos0, topk):
    jq = pl.program_id(1)
    kt_sz = KEY_TILE
    topk_f = float(topk)

    @pl.when(jq == 0)
    def _():
        kb_ref[...] = k_ref[...].astype(BF16)
        for kt in range(n_tiles):
            vb_ref[kt] = v_ref[kt].T.astype(BF16)
            ki = kikw_ref[kt]
            lane = lax.broadcasted_iota(jnp.int32, ki.shape, 1)
            kie = jnp.where(lane < IDX_DIM, ki, 0.0)
            kie_ref[kt] = kie.astype(BF16)
            kio_ref[kt] = pltpu.roll(kie, IDX_DIM, axis=1).astype(BF16)

    if tq_in < tq:
        qp_ref, iqp_ref, ikwp_ref, azp_ref = pad_refs
        for dst, src in ((qp_ref, q_ref), (iqp_ref, iq_ref), (ikwp_ref, ikw_ref), (azp_ref, az_ref)):
            dst[...] = jnp.zeros(dst.shape, dst.dtype)
            dst[0:tq_in, :] = src[...]
        q_ref, iq_ref, ikw_ref, az_ref = qp_ref, iqp_ref, ikwp_ref, azp_ref

    iqb_ref[...] = iq_ref[...].astype(BF16)
    wt_ref[...] = ikw_ref[...].T * IDX_SCALE
    q = q_ref[...].astype(BF16)
    for head in range(A_HEADS):
        g, hh = divmod(head, A_GROUP)
        qs_ref[g, hh * tq:(hh + 1) * tq, :] = q[:, A_HEAD_DIM * head:A_HEAD_DIM * (head + 1)]

    q_last = pos0 + (jq + 1) * tq_in - 1
    key_end = jnp.minimum((q_last // CHUNK + 1) * CHUNK, n_valid_keys)
    n_need = (key_end + kt_sz - 1) // kt_sz

    def for_needed_tiles(body):
        for kt in range(n_tiles):
            pl.when(kt < n_need)(functools.partial(body, kt))

    krow = lax.broadcasted_iota(jnp.int32, (kt_sz, tq), 0)
    qcol = lax.broadcasted_iota(jnp.int32, (kt_sz, tq), 1)
    qchunk = jnp.right_shift(pos0 + jq * tq_in + qcol, CHUNK_SHIFT)

    def score_tile(kt):
        acc = None
        for pair in range(IDX_HEADS // 2):
            iq_pair = iqb_ref[:, LANES * pair:LANES * (pair + 1)]
            for half, kref in enumerate((kie_ref, kio_ref)):
                head = 2 * pair + half
                d = _dot_nt(kref[kt], iq_pair)
                contrib = jnp.maximum(d, 0.0) * wt_ref[IDX_DIM + head:IDX_DIM + head + 1, :]
                acc = contrib if acc is None else acc + contrib
        kpos = kt * kt_sz + krow
        if (kt + 1) * kt_sz > n_valid_keys:
            acc = jnp.where(kpos < n_valid_keys, acc, -jnp.inf)
        sc_ref[kt] = jnp.where(jnp.right_shift(kpos, CHUNK_SHIFT) <= qchunk, acc, -jnp.inf)

    for_needed_tiles(score_tile)

    def count_into(ref, kt, hit):
        ref[...] += jnp.sum(hit.reshape(kt_sz // SUBLANES, SUBLANES, tq), axis=0)

    def total(ref):
        return jnp.sum(ref[...], axis=0, keepdims=True)

    def run_bisection(n_static):
        def bisect(i, ans):
            cand = ans + jnp.left_shift(jnp.int32(1), jnp.int32(31) - i)
            cand_f = _key_to_f32(jnp.maximum(cand, jnp.int32(KEY_NEG_INF)))
            part = jnp.zeros((SUBLANES, tq), F32)
            for kt in range(n_static):
                hit = jnp.where(sc_ref[kt] >= cand_f, 1.0, 0.0)
                part = part + jnp.sum(hit.reshape(kt_sz // SUBLANES, SUBLANES, tq), axis=0)
            return jnp.where(jnp.sum(part, axis=0, keepdims=True) >= topk_f, cand, ans)

        ans = lax.fori_loop(0, 32, bisect, jnp.full((1, tq), INT_MIN, jnp.int32))
        ans_ref[...] = jnp.broadcast_to(ans, ans_ref.shape)

    need_of = []
    for j in range(nq):
        end = min(((pos0 + (j + 1) * tq_in - 1) // CHUNK + 1) * CHUNK, n_valid_keys)
        need_of.append(-(-end // kt_sz))
    for n_static in sorted(set(need_of)):
        blocks = [j for j in range(nq) if need_of[j] == n_static]
        if len(blocks) == nq:
            run_bisection(n_static)
        else:
            in_range = jnp.logical_and(jq >= blocks[0], jq <= blocks[-1])
            pl.when(in_range)(functools.partial(run_bisection, n_static))
    ans = ans_ref[0:1, :]
    thr = _key_to_f32(jnp.maximum(ans, jnp.int32(KEY_NEG_INF)))

    cnt_ref[...] = jnp.zeros(cnt_ref.shape, F32)
    gt_ref[...] = jnp.zeros(gt_ref.shape, F32)

    def count_ties(kt):
        sc = sc_ref[kt]
        count_into(gt_ref, kt, jnp.where(sc > thr, 1.0, 0.0))
        count_into(cnt_ref, kt, jnp.where(sc >= thr, jnp.where(sc > -jnp.inf, 1.0, 0.0), 0.0))

    for_needed_tiles(count_ties)
    cnt_gt = total(gt_ref)
    cnt_ge = total(cnt_ref)
    qcol1 = lax.broadcasted_iota(jnp.int32, (1, tq), 1)
    tie = jnp.where(cnt_ge > topk_f, jnp.where(qcol1 < tq_in, 1.0, 0.0), 0.0)
    m_ref[...] = jnp.full(m_ref.shape, n_tiles * kt_sz, jnp.int32)

    @pl.when(jnp.max(tie) > 0.0)
    def _():
        nbits = max(1, (n_tiles * kt_sz - 1).bit_length())

        def bisect_pos(i, t):
            cand = t + jnp.left_shift(jnp.int32(1), jnp.int32(nbits - 1) - i)
            cnt_ref[...] = jnp.zeros(cnt_ref.shape, F32)

            def count_below(kt):
                hit = jnp.where(sc_ref[kt] == thr, jnp.where(kt * kt_sz + krow < cand, 1.0, 0.0), 0.0)
                count_into(cnt_ref, kt, hit)

            for_needed_tiles(count_below)
            return jnp.where(cnt_gt + total(cnt_ref) < topk_f, cand, t)

        t_last = lax.fori_loop(0, nbits, bisect_pos, jnp.zeros((1, tq), jnp.int32))
        m_ref[...] = jnp.broadcast_to(t_last, m_ref.shape)

    m_last = m_ref[0:1, :]
    thr_eq = jnp.where(thr > -jnp.inf, thr, jnp.inf)

    def bias_tile(kt):
        sc = sc_ref[kt]
        kpos = kt * kt_sz + krow
        tied = jnp.where(sc == thr_eq, jnp.where(kpos <= m_last, 0.0, NEG_BIG), NEG_BIG)
        bias_ref[kt] = jnp.where(sc > thr, 0.0, tied)

    for_needed_tiles(bias_tile)

    scale = A_HEAD_DIM ** -0.5
    cols = A_GROUP * tq

    def attend(kt, carry):
        bias = jnp.concatenate([bias_ref[kt]] * A_GROUP, axis=1)
        out = []
        for g in range(A_KV_HEADS):
            m_run, l_run, acc = carry[g]
            gsl = slice(A_HEAD_DIM * g, A_HEAD_DIM * (g + 1))
            s = _dot_nt(kb_ref[kt, :, gsl], qs_ref[g]) * scale + bias
            m_new = jnp.maximum(m_run, jnp.max(s, axis=0, keepdims=True))
            alpha = jnp.exp(m_run - m_new)
            p = jnp.exp(s - m_new)
            l_new = alpha * l_run + jnp.sum(p, axis=0, keepdims=True)
            acc_new = alpha * acc + _dot(vb_ref[kt, gsl, :], p.astype(BF16))
            out.append((m_new, l_new, acc_new))
        return tuple(out)

    init = (jnp.full((1, cols), NEG_BIG, F32), jnp.zeros((1, cols), F32), jnp.zeros((A_HEAD_DIM, cols), F32))
    stats = lax.fori_loop(0, n_need, attend, (init,) * A_KV_HEADS)
    for g in range(A_KV_HEADS):
        _, l_run, acc = stats[g]
        o = (acc / l_run).T
        for hh in range(A_GROUP):
            hs = slice(A_HEAD_DIM * (A_GROUP * g + hh), A_HEAD_DIM * (A_GROUP * g + hh + 1))
            res = (o[hh * tq:(hh + 1) * tq, :] * _silu(az_ref[:, hs])).astype(o_ref.dtype)
            o_ref[:, hs] = res[0:tq_in, :]


def _dsa(u, k_all, v_all, kikw_all, *, n_batch, t_len, n_valid_keys, pos0):
    n_keys = k_all.shape[1]
    n_tiles = n_keys // KEY_TILE
    topk = min(TOPK_MAX, n_valid_keys // 4)
    if t_len % 256 == 0:
        tq = tq_in = 256
    elif t_len % LANES == 0:
        tq = tq_in = LANES
    else:
        tq, tq_in = LANES, t_len
    nq = t_len // tq_in
    row = lambda b, j: b * nq + j
    tiled = lambda t: t.reshape(n_batch, n_tiles, KEY_TILE, t.shape[-1])
    key_spec = lambda width: pl.BlockSpec((None, n_tiles, KEY_TILE, width), lambda b, j: (b, 0, 0, 0))
    pad_scratch = []
    if tq_in < tq:
        pad_scratch = [pltpu.VMEM((tq, A_WIDTH), F32), pltpu.VMEM((tq, IDX_HEADS * IDX_DIM), F32),
                       pltpu.VMEM((tq, LANES), F32), pltpu.VMEM((tq, A_WIDTH), F32)]
    kern = functools.partial(_dsa_kernel, tq=tq, tq_in=tq_in, nq=nq, n_tiles=n_tiles,
                             n_valid_keys=n_valid_keys, pos0=pos0, topk=topk)
    return pl.pallas_call(
        kern,
        grid=(n_batch, nq),
        in_specs=[pl.BlockSpec((tq_in, A_WIDTH), lambda b, j: (row(b, j), COL_AQ // A_WIDTH)),
                  pl.BlockSpec((tq_in, A_WIDTH), lambda b, j: (row(b, j), COL_IQ // A_WIDTH)),
                  pl.BlockSpec((tq_in, LANES), lambda b, j: (row(b, j), COL_IKW // LANES)),
                  pl.BlockSpec((tq_in, A_WIDTH), lambda b, j: (row(b, j), COL_AZ // A_WIDTH)),
                  key_spec(A_KV_WIDTH), key_spec(A_KV_WIDTH), key_spec(LANES)],
        out_specs=pl.BlockSpec((tq_in, A_WIDTH), lambda b, j: (row(b, j), 0)),
        out_shape=jax.ShapeDtypeStruct((n_batch * t_len, A_WIDTH), BF16),
        scratch_shapes=[pltpu.VMEM((n_tiles, KEY_TILE, A_KV_WIDTH), BF16),
                        pltpu.VMEM((n_tiles, A_KV_WIDTH, KEY_TILE), BF16),
                        pltpu.VMEM((n_tiles, KEY_TILE, LANES), BF16),
                        pltpu.VMEM((n_tiles, KEY_TILE, LANES), BF16),
                        pltpu.VMEM((tq, IDX_HEADS * IDX_DIM), BF16),
                        pltpu.VMEM((LANES, tq), F32),
                        pltpu.VMEM((A_KV_HEADS, A_GROUP * tq, A_HEAD_DIM), BF16),
                        pltpu.VMEM((n_tiles, KEY_TILE, tq), F32),
                        pltpu.VMEM((n_tiles, KEY_TILE, tq), F32),
                        pltpu.VMEM((SUBLANES, tq), F32), pltpu.VMEM((SUBLANES, tq), F32),
                        pltpu.VMEM((SUBLANES, tq), jnp.int32),
                        pltpu.VMEM((SUBLANES, tq), jnp.int32)] + pad_scratch,
        compiler_params=_cparams(2),
        name="dsa",
    )(u, u, u, u, tiled(k_all), tiled(v_all), tiled(kikw_all))


def _expand_heads(v, e):
    hi = v.astype(BF16)
    lo = (v - hi.astype(F32)).astype(BF16)
    return _dot(hi, e) + _dot(lo, e)


def _mamba_kernel(*refs, lc, n_in, has_init):
    if has_init:
        (z_ref, xbc_ref, dt_ref, cw_ref, cb_ref, dtb_ref, alog_ref, dsk_ref, nrm_ref, e_ref,
         cprev_ref, h0_ref, y_ref, cst_ref, hl_ref, xpad_ref, st_ref, yacc_ref, *pad_refs) = refs
    else:
        (z_ref, xbc_ref, dt_ref, cw_ref, cb_ref, dtb_ref, alog_ref, dsk_ref, nrm_ref, e_ref,
         y_ref, cst_ref, hl_ref, xpad_ref, st_ref, yacc_ref, *pad_refs) = refs
    c = pl.program_id(1)
    n_chunks = pl.num_programs(1)

    @pl.when(c == 0)
    def _():
        xpad_ref[...] = jnp.zeros(xpad_ref.shape, F32)
        if has_init:
            xpad_ref[0:SUBLANES, :] = cprev_ref[...]
            for g in range(B_GROUPS):
                st_ref[g] = h0_ref[B_GROUP_W * g:B_GROUP_W * (g + 1), :].T
        else:
            st_ref[...] = jnp.zeros(st_ref.shape, F32)

    xpad_ref[SUBLANES:SUBLANES + n_in, :] = xbc_ref[...]
    acc = cb_ref[...] + cw_ref[0:1, :] * xpad_ref[SUBLANES - 3:SUBLANES - 3 + lc, :]
    for j in range(1, B_CONV):
        acc = acc + cw_ref[j:j + 1, :] * xpad_ref[SUBLANES - 3 + j:SUBLANES - 3 + j + lc, :]
    xbc = _silu(acc)

    @pl.when(c == n_chunks - 1)
    def _():
        cst_ref[...] = xpad_ref[SUBLANES + n_in - 3:SUBLANES + n_in, :]

    xpad_ref[0:SUBLANES, :] = xpad_ref[lc:lc + SUBLANES, :]

    if n_in < lc:
        dtp_ref, zp_ref = pad_refs
        dtp_ref[...] = jnp.zeros(dtp_ref.shape, F32)
        dtp_ref[0:n_in, :] = dt_ref[...]
        zp_ref[...] = jnp.zeros(zp_ref.shape, F32)
        zp_ref[0:n_in, :] = z_ref[...]
        dt_raw = dtp_ref[...]
        z_all = zp_ref
    else:
        dt_raw = dt_ref[...]
        z_all = z_ref
    pre = dt_raw + dtb_ref[...]
    dt = jnp.maximum(pre, 0.0) + jnp.log1p(jnp.exp(-jnp.abs(pre)))
    row = lax.broadcasted_iota(jnp.int32, (lc, LANES), 0)
    if n_in < lc:
        dt = jnp.where(row < n_in, dt, 0.0)
    a = -jnp.exp(alog_ref[...])
    cum = dt * a
    shift = 1
    while shift < lc:
        cum = cum + jnp.where(row >= shift, pltpu.roll(cum, shift, axis=0), 0.0)
        shift *= 2
    cum_t = cum.T
    dt_t = dt.T
    c_last = cum[lc - 1:lc, :]
    e = e_ref[...]
    x1 = _expand_heads(jnp.exp(cum), e)
    x2 = _expand_heads(jnp.exp(c_last - cum) * dt, e)
    x3 = _expand_heads(jnp.broadcast_to(jnp.exp(c_last), (SUBLANES, LANES)), e)[0:1, :]

    xs = xbc[:, 0:B_WIDTH]
    xd = (xs * x2).astype(BF16)
    li = lax.broadcasted_iota(jnp.int32, (lc, lc), 0)
    si = lax.broadcasted_iota(jnp.int32, (lc, lc), 1)
    causal = li >= si
    lane = lax.broadcasted_iota(jnp.int32, (lc, LANES), 1)
    dsk = dsk_ref[...]
    for g in range(B_GROUPS):
        bg = xbc[:, B_WIDTH + B_STATE * g:B_WIDTH + B_STATE * (g + 1)]
        cg = xbc[:, B_WIDTH + B_GROUPS * B_STATE + B_STATE * g:B_WIDTH + B_GROUPS * B_STATE + B_STATE * (g + 1)]
        bgb = bg.astype(BF16)
        cgb = cg.astype(BF16)
        cb = _dot_nt(cgb, bgb)
        gs = slice(B_GROUP_W * g, B_GROUP_W * (g + 1))
        state = st_ref[g]
        y_off = _dot(cgb, state.astype(BF16)) * x1[:, gs]
        for pp in range(B_GROUP_W // LANES):
            col = B_GROUP_W * g + LANES * pp
            xp = xs[:, col:col + LANES]
            y_pair = y_off[:, LANES * pp:LANES * (pp + 1)] + dsk[:, col:col + LANES] * xp
            for half in range(2):
                head = col // B_HEAD_DIM + half
                seg = cum[:, head:head + 1] - cum_t[head:head + 1, :]
                wgt = cb * jnp.exp(jnp.where(causal, seg, NEG_BIG)) * dt_t[head:head + 1, :]
                in_half = (lane < B_HEAD_DIM) if half == 0 else (lane >= B_HEAD_DIM)
                xh = jnp.where(in_half, xp, 0.0).astype(BF16)
                y_pair = y_pair + _dot(wgt.astype(BF16), xh)
            yacc_ref[:, col:col + LANES] = y_pair
        st_ref[g] = state * x3[:, gs] + _dot(bg.T.astype(BF16), xd[:, gs])

    for g in range(B_GROUPS):
        gs = slice(B_GROUP_W * g, B_GROUP_W * (g + 1))
        yg = yacc_ref[:, gs] * _silu(z_all[:, gs])
        ms = jnp.mean(yg * yg, axis=-1, keepdims=True)
        out = (yg * lax.rsqrt(ms + EPS) * nrm_ref[:, gs]).astype(y_ref.dtype)
        y_ref[:, gs] = out[0:n_in, :]

    @pl.when(c == n_chunks - 1)
    def _():
        for g in range(B_GROUPS):
            hl_ref[B_GROUP_W * g:B_GROUP_W * (g + 1), :] = st_ref[g].T


def _mamba(u, conv_w, conv_b, dt_bias, a_log, d_skip, ssm_norm, conv_prev, h0, *, n_batch, t_len):
    lc = LANES
    if t_len % lc == 0:
        n_in = lc
    else:
        assert t_len < lc
        n_in = t_len
    nc = t_len // n_in
    has_init = conv_prev is not None
    row = lambda b, c: b * nc + c
    pad1 = lambda v: jnp.concatenate([v.astype(F32), jnp.zeros((LANES - B_HEADS,), F32)]).reshape(1, LANES)
    head_of_col = jnp.arange(B_WIDTH, dtype=jnp.int32) // B_HEAD_DIM
    expand = (jnp.arange(LANES, dtype=jnp.int32)[:, None] == head_of_col[None, :]).astype(BF16)
    dsk_row = jnp.repeat(d_skip.astype(F32), B_HEAD_DIM).reshape(1, B_WIDTH)
    const = lambda shape: pl.BlockSpec(shape, lambda b, c: (0,) * len(shape))
    in_specs = [pl.BlockSpec((n_in, B_WIDTH), lambda b, c: (row(b, c), COL_BZ // B_WIDTH)),
                pl.BlockSpec((n_in, B_CONV_DIM), lambda b, c: (row(b, c), COL_XBC // B_CONV_DIM)),
                pl.BlockSpec((n_in, LANES), lambda b, c: (row(b, c), COL_DT // LANES)),
                const((B_CONV, B_CONV_DIM)), const((1, B_CONV_DIM)), const((1, LANES)), const((1, LANES)),
                const((1, B_WIDTH)), const((1, B_WIDTH)), const((LANES, B_WIDTH))]
    args = [u, u, u, conv_w, conv_b.reshape(1, B_CONV_DIM), pad1(dt_bias), pad1(a_log), dsk_row,
            ssm_norm.reshape(1, B_WIDTH), expand]
    if has_init:
        cprev8 = jnp.concatenate([jnp.zeros((n_batch, SUBLANES - (B_CONV - 1), B_CONV_DIM), F32), conv_prev], axis=1)
        in_specs += [pl.BlockSpec((None, SUBLANES, B_CONV_DIM), lambda b, c: (b, 0, 0)),
                     pl.BlockSpec((None, B_WIDTH, B_STATE), lambda b, c: (b, 0, 0))]
        args += [cprev8, h0.reshape(n_batch, B_WIDTH, B_STATE)]
    pad_scratch = []
    if n_in < lc:
        pad_scratch = [pltpu.VMEM((lc, LANES), F32), pltpu.VMEM((lc, B_WIDTH), F32)]
    kern = functools.partial(_mamba_kernel, lc=lc, n_in=n_in, has_init=has_init)
    y, cst, hl = pl.pallas_call(
        kern,
        grid=(n_batch, nc),
        in_specs=in_specs,
        out_specs=[pl.BlockSpec((n_in, B_WIDTH), lambda b, c: (row(b, c), 0)),
                   pl.BlockSpec((None, B_CONV - 1, B_CONV_DIM), lambda b, c: (b, 0, 0)),
                   pl.BlockSpec((None, B_WIDTH, B_STATE), lambda b, c: (b, 0, 0))],
        out_shape=[jax.ShapeDtypeStruct((n_batch * t_len, B_WIDTH), BF16),
                   jax.ShapeDtypeStruct((n_batch, B_CONV - 1, B_CONV_DIM), F32),
                   jax.ShapeDtypeStruct((n_batch, B_WIDTH, B_STATE), F32)],
        scratch_shapes=[pltpu.VMEM((SUBLANES + lc, B_CONV_DIM), F32),
                        pltpu.VMEM((B_GROUPS, B_STATE, B_GROUP_W), F32),
                        pltpu.VMEM((lc, B_WIDTH), F32)] + pad_scratch,
        compiler_params=_cparams(2),
        name="mamba",
    )(*args)
    return y, cst, hl.reshape(n_batch, B_HEADS, B_HEAD_DIM, B_STATE)


def _mem_kernel(q_ref, z_ref, k_ref, v_ref, o_ref):
    q = q_ref[...].astype(BF16)
    scale = M_HEAD_DIM ** -0.5
    for head in range(M_HEADS):
        hs = slice(M_HEAD_DIM * head, M_HEAD_DIM * (head + 1))
        s = _dot_nt(q[:, hs], k_ref[:, hs].astype(BF16)) * scale
        s_max = jnp.max(s, axis=-1, keepdims=True)
        p = jnp.exp(s - s_max)
        denom = jnp.sum(p, axis=-1, keepdims=True)
        o = _dot(p.astype(BF16), v_ref[:, hs].astype(BF16)) / denom
        o_ref[:, hs] = (o * _silu(z_ref[:, hs])).astype(o_ref.dtype)


def _mem_attend(u, mk, mv, *, n_batch, t_len):
    tq = 512 if t_len % 512 == 0 else t_len
    nq = t_len // tq
    row = lambda b, j: b * nq + j
    return pl.pallas_call(
        _mem_kernel,
        grid=(n_batch, nq),
        in_specs=[pl.BlockSpec((tq, M_WIDTH), lambda b, j: (row(b, j), COL_MQ // M_WIDTH)),
                  pl.BlockSpec((tq, M_WIDTH), lambda b, j: (row(b, j), COL_MZ // M_WIDTH)),
                  pl.BlockSpec((None, N_MEM, M_WIDTH), lambda b, j: (b, 0, 0)),
                  pl.BlockSpec((None, N_MEM, M_WIDTH), lambda b, j: (b, 0, 0))],
        out_specs=pl.BlockSpec((tq, M_WIDTH), lambda b, j: (row(b, j), 0)),
        out_shape=jax.ShapeDtypeStruct((n_batch * t_len, M_WIDTH), BF16),
        compiler_params=_cparams(2),
        name="mem_attend",
    )(u, u, mk, mv)


def _merge_kernel(ya_ref, yb_ref, ym_ref, ga_ref, gb_ref, gm_ref, wa_ref, wb_ref, wm_ref, o_ref):
    merged = _sigmoid(ga_ref[...]) * _dot(ya_ref[...], wa_ref[...])
    merged = merged + _sigmoid(gb_ref[...]) * _dot(yb_ref[...], wb_ref[...])
    merged = merged + _sigmoid(gm_ref[...]) * _dot(ym_ref[...], wm_ref[...])
    o_ref[...] = merged.astype(o_ref.dtype)


def _merge(ya, yb, ym, u, w_pa, w_pb, w_pm, tm):
    n = ya.shape[0]
    rows = lambda width: pl.BlockSpec((tm, width), lambda i: (i, 0))
    gate = lambda k: pl.BlockSpec((tm, D_MODEL), lambda i: (i, COL_GATES // D_MODEL + k))
    weight = lambda width: pl.BlockSpec((width, D_MODEL), lambda i: (0, 0), pipeline_mode=pl.Buffered(1))
    return pl.pallas_call(
        _merge_kernel,
        grid=(n // tm,),
        in_specs=[rows(A_WIDTH), rows(B_WIDTH), rows(M_WIDTH), gate(0), gate(1), gate(2),
                  weight(A_WIDTH), weight(B_WIDTH), weight(M_WIDTH)],
        out_specs=rows(D_MODEL),
        out_shape=jax.ShapeDtypeStruct((n, D_MODEL), BF16),
        compiler_params=_cparams(1),
        name="merge",
    )(ya, yb, ym, u, u, u, w_pa, w_pb, w_pm)


def _final_kernel(m_ref, x_ref, wo_ref, g_ref, o_ref):
    y = x_ref[...] + _dot(m_ref[...], wo_ref[...])
    ms = jnp.mean(y * y, axis=-1, keepdims=True)
    o_ref[...] = y * lax.rsqrt(ms + EPS) * g_ref[...]


def _final(merged, x, w_o, g, tm):
    n = x.shape[0]
    rows = pl.BlockSpec((tm, D_MODEL), lambda i: (i, 0))
    return pl.pallas_call(
        _final_kernel,
        grid=(n // tm,),
        in_specs=[rows, rows,
                  pl.BlockSpec((D_MODEL, D_MODEL), lambda i: (0, 0), pipeline_mode=pl.Buffered(1)),
                  pl.BlockSpec((1, D_MODEL), lambda i: (0, 0))],
        out_specs=rows,
        out_shape=jax.ShapeDtypeStruct((n, D_MODEL), F32),
        compiler_params=_cparams(1),
        name="final",
    )(merged, x, w_o, g.reshape(1, D_MODEL))


def _row_tile(n, pref):
    t = pref
    while n % t:
        t //= 2
    return t


def _pad_keys(t, n_keys):
    pad = n_keys - t.shape[1]
    if pad == 0:
        return t
    return jnp.concatenate([t, jnp.zeros((t.shape[0], pad, t.shape[2]), t.dtype)], axis=1)


def _layer(x, pos0, past_k, past_v, past_ki, conv_prev, h0, mem_k, mem_v, lw, norm_final):
    (norm_in, w_in_packed, conv_w, conv_b, dt_bias, a_log, d_skip, ssm_norm, w_pa, w_pb, w_pm, w_o) = lw
    n_batch, t_len, _ = x.shape
    n = n_batch * t_len
    x2 = x.reshape(n, D_MODEL)
    u = _rms_proj(x2, norm_in, w_in_packed, _row_tile(n, 1024), 1024)

    k_new = u[:, COL_AK:COL_AK + A_KV_WIDTH].reshape(n_batch, t_len, A_KV_WIDTH)
    v_new = u[:, COL_AV:COL_AV + A_KV_WIDTH].reshape(n_batch, t_len, A_KV_WIDTH)
    kikw_new = u[:, COL_IKW:COL_IKW + LANES].reshape(n_batch, t_len, LANES)
    if past_k is None:
        k_all, v_all, kikw_all = k_new, v_new, kikw_new
    else:
        n_past = past_k.shape[1]
        past_kikw = jnp.concatenate([past_ki, jnp.zeros((n_batch, n_past, LANES - IDX_DIM), F32)], axis=2)
        k_all = jnp.concatenate([past_k.reshape(n_batch, n_past, A_KV_WIDTH), k_new], axis=1)
        v_all = jnp.concatenate([past_v.reshape(n_batch, n_past, A_KV_WIDTH), v_new], axis=1)
        kikw_all = jnp.concatenate([past_kikw, kikw_new], axis=1)
    n_valid_keys = k_all.shape[1]
    n_keys = -(-n_valid_keys // KEY_TILE) * KEY_TILE
    ya = _dsa(u, _pad_keys(k_all, n_keys), _pad_keys(v_all, n_keys), _pad_keys(kikw_all, n_keys),
              n_batch=n_batch, t_len=t_len, n_valid_keys=n_valid_keys, pos0=pos0)

    yb, conv_state, h_last = _mamba(u, conv_w, conv_b, dt_bias, a_log, d_skip, ssm_norm, conv_prev, h0,
                                    n_batch=n_batch, t_len=t_len)
    ym = _mem_attend(u, mem_k, mem_v, n_batch=n_batch, t_len=t_len)
    merged = _merge(ya, yb, ym, u, w_pa, w_pb, w_pm, _row_tile(n, 256))
    y = _final(merged, x2, w_o, norm_final, _row_tile(n, 512)).reshape(n_batch, t_len, D_MODEL)
    return (y, k_new.reshape(n_batch, t_len, A_KV_HEADS, A_HEAD_DIM),
            v_new.reshape(n_batch, t_len, A_KV_HEADS, A_HEAD_DIM),
            kikw_new[:, :, 0:IDX_DIM], conv_state, h_last)


def kernel(x_prompt, x_sample, mem_prompt, cache_attn_k, cache_attn_v, cache_idx_k, state_conv, state_ssm,
           cache_mem_k, cache_mem_v, norm_in, w_in, conv_w, conv_b, dt_bias, a_log, d_skip, ssm_norm,
           norm_mem, w_mem_kv, w_pa, w_pb, w_pm, w_o, norm_final):
    depth = w_in.shape[0]
    assert depth == 1, "the final RMSNorm is fused into the (single) layer"
    bp = x_prompt.shape[0]
    bs = x_sample.shape[0]
    first = lambda t: t.reshape(t.shape[1:])
    lw = (first(norm_in), _pack_w_in(w_in), first(conv_w), first(conv_b), first(dt_bias), first(a_log),
          first(d_skip), first(ssm_norm), first(w_pa).astype(BF16), first(w_pb).astype(BF16),
          first(w_pm).astype(BF16), first(w_o).astype(BF16))

    mem2 = mem_prompt.reshape(bp * N_MEM, D_MODEL)
    mkv = _rms_proj(mem2, first(norm_mem), first(w_mem_kv).astype(BF16), _row_tile(bp * N_MEM, 1024), 1024)
    mk_p = mkv[:, 0:M_WIDTH].reshape(bp, N_MEM, M_WIDTH)
    mv_p = mkv[:, M_WIDTH:2 * M_WIDTH].reshape(bp, N_MEM, M_WIDTH)

    yp, kp, vp, kip, convp, ssmp = _layer(x_prompt, 0, None, None, None, None, None, mk_p, mv_p, lw, norm_final)
    ys, ks, vs, kis, convs, ssms = _layer(
        x_sample, PAST_LEN, first(cache_attn_k), first(cache_attn_v), first(cache_idx_k), first(state_conv),
        first(state_ssm), first(cache_mem_k).reshape(bs, N_MEM, M_WIDTH),
        first(cache_mem_v).reshape(bs, N_MEM, M_WIDTH), lw, norm_final)

    st = lambda t: t[None]
    return (yp, ys, st(kp), st(vp), st(kip), st(convp), st(ssmp),
            st(mk_p.reshape(bp, N_MEM, M_HEADS, M_HEAD_DIM)), st(mv_p.reshape(bp, N_MEM, M_HEADS, M_HEAD_DIM)),
            st(ks), st(vs), st(kis), st(convs), st(ssms))
```

```python
import functools

import jax
import jax.numpy as jnp
from jax import lax
from jax.experimental import pallas as pl
from jax.experimental.pallas import tpu as pltpu

F32 = jnp.float32
BF16 = jnp.bfloat16

D_MODEL = 2048
CHUNK = 64
CHUNK_SHIFT = 6
assert 1 << CHUNK_SHIFT == CHUNK
N_MEM = 256
EPS = 1e-6
PAST_LEN = 1024

A_HEADS = 8
A_KV_HEADS = 2
A_HEAD_DIM = 128
A_GROUP = A_HEADS // A_KV_HEADS
A_WIDTH = A_HEADS * A_HEAD_DIM
A_KV_WIDTH = A_KV_HEADS * A_HEAD_DIM
IDX_HEADS = 16
IDX_DIM = 64
TOPK_MAX = 256
IDX_SCALE = (IDX_DIM * IDX_HEADS) ** -0.5

B_WIDTH = 2048
B_HEAD_DIM = 64
B_HEADS = B_WIDTH // B_HEAD_DIM
B_GROUPS = 4
B_HPG = B_HEADS // B_GROUPS
B_STATE = 128
B_CONV = 4
B_CONV_DIM = B_WIDTH + 2 * B_GROUPS * B_STATE
B_GROUP_W = B_WIDTH // B_GROUPS

M_HEADS = 4
M_HEAD_DIM = 256
M_WIDTH = M_HEADS * M_HEAD_DIM

N_BRANCH = 3
IN_SPLITS = (A_WIDTH, A_KV_WIDTH, A_KV_WIDTH, IDX_HEADS * IDX_DIM, IDX_DIM, IDX_HEADS, A_WIDTH,
             B_WIDTH, B_CONV_DIM, B_HEADS, M_WIDTH, M_WIDTH, N_BRANCH * D_MODEL)
IN_COLS = sum(IN_SPLITS)
(SRC_AQ, SRC_AK, SRC_AV, SRC_IQ, SRC_IK, SRC_IW, SRC_AZ, SRC_BZ, SRC_XBC, SRC_DT, SRC_MQ, SRC_MZ,
 SRC_GATES) = (sum(IN_SPLITS[:i]) for i in range(len(IN_SPLITS)))

LANES = 128
SUBLANES = 8
VMEM_LIMIT_BYTES = 56 * 1024 * 1024

COL_GATES = 0
COL_XBC = 6144
COL_AQ = 9216
COL_IQ = 10240
COL_AZ = 11264
COL_MQ = 12288
COL_MZ = 13312
COL_BZ = 14336
COL_AK = 16384
COL_AV = 16640
COL_IKW = 16896
COL_DT = 17152
PACKED_COLS = 17408

PACK_CHUNK = 256
PACK_ROW_ALIGN = 2 * SUBLANES
PACK_SEGMENTS = ((SRC_GATES, COL_GATES, N_BRANCH * D_MODEL), (SRC_XBC, COL_XBC, B_CONV_DIM),
                 (SRC_AQ, COL_AQ, A_WIDTH), (SRC_IQ, COL_IQ, IDX_HEADS * IDX_DIM), (SRC_AZ, COL_AZ, A_WIDTH),
                 (SRC_MQ, COL_MQ, M_WIDTH), (SRC_MZ, COL_MZ, M_WIDTH), (SRC_BZ, COL_BZ, B_WIDTH),
                 (SRC_AK, COL_AK, A_KV_WIDTH), (SRC_AV, COL_AV, A_KV_WIDTH),
                 (SRC_IK, COL_IKW, PACK_CHUNK), (SRC_DT, COL_DT, PACK_CHUNK))


def _pack_source_table():
    table = [None] * (PACKED_COLS // PACK_CHUNK)
    for src, dst, width in PACK_SEGMENTS:
        assert dst % PACK_CHUNK == 0 and width % PACK_CHUNK == 0 and src % (2 * SUBLANES) == 0
        for off in range(0, width, PACK_CHUNK):
            assert src + off + PACK_CHUNK <= IN_COLS
            table[(dst + off) // PACK_CHUNK] = src + off
    assert all(t is not None for t in table)
    return table


PACK_SOURCE = _pack_source_table()

KEY_TILE = 256
PROJ_SUB_ROWS = 512
LOG2_E = 1.4426950408889634
INT_MIN = -2 ** 31
KEY_NEG_INF = INT_MIN + 0x7FFFFF
NEG_BIG = -1e30


def _cparams(n_grid):
    return pltpu.CompilerParams(dimension_semantics=("arbitrary",) * n_grid,
                                vmem_limit_bytes=VMEM_LIMIT_BYTES)


def _silu(z):
    return z * (1.0 / (1.0 + jnp.exp(-z)))


def _sigmoid(z):
    return 1.0 / (1.0 + jnp.exp(-z))


def _dot(a, b):
    return jnp.dot(a, b, preferred_element_type=F32)


def _dot_nt(a, b):
    return lax.dot_general(a, b, (((1,), (1,)), ((), ())), preferred_element_type=F32)


def _pack_kernel(src_ref, wt_ref, o_ref):
    del src_ref
    o_ref[...] = wt_ref[...].T.astype(o_ref.dtype)


def _pack_w_in(w):
    wt = jnp.transpose(w.reshape(D_MODEL, IN_COLS))
    grid_spec = pltpu.PrefetchScalarGridSpec(
        num_scalar_prefetch=1,
        grid=(PACKED_COLS // PACK_CHUNK,),
        in_specs=[pl.BlockSpec((pl.Element(PACK_CHUNK), pl.Element(D_MODEL)),
                               lambda i, src: (src[i] * PACK_ROW_ALIGN, 0))],
        out_specs=pl.BlockSpec((D_MODEL, PACK_CHUNK), lambda i, src: (0, i)))
    return pl.pallas_call(
        _pack_kernel,
        grid_spec=grid_spec,
        out_shape=jax.ShapeDtypeStruct((D_MODEL, PACKED_COLS), BF16),
        compiler_params=_cparams(1),
        name="pack_w_in",
    )(jnp.asarray([s // PACK_ROW_ALIGN for s in PACK_SOURCE], jnp.int32), wt)


def _proj_kernel(x_ref, g_ref, w_ref, o_ref, h_ref):
    @pl.when(pl.program_id(1) == 0)
    def _():
        x = x_ref[...]
        ms = jnp.mean(x * x, axis=-1, keepdims=True)
        h_ref[...] = (x * lax.rsqrt(ms + EPS) * g_ref[...]).astype(BF16)

    tm = o_ref.shape[0]
    sub = min(tm, PROJ_SUB_ROWS)
    for r in range(tm // sub):
        o_ref[r * sub:(r + 1) * sub, :] = _dot(h_ref[r * sub:(r + 1) * sub, :], w_ref[...])


def _rms_proj(x, g, w, tm, tn):
    m, d = x.shape
    n = w.shape[1]
    return pl.pallas_call(
        _proj_kernel,
        grid=(m // tm, n // tn),
        in_specs=[pl.BlockSpec((tm, d), lambda i, j: (i, 0)),
                  pl.BlockSpec((1, d), lambda i, j: (0, 0)),
                  pl.BlockSpec((d, tn), lambda i, j: (0, j))],
        out_specs=pl.BlockSpec((tm, tn), lambda i, j: (i, j)),
        out_shape=jax.ShapeDtypeStruct((m, n), F32),
        scratch_shapes=[pltpu.VMEM((tm, d), BF16)],
        compiler_params=_cparams(2),
        name="rms_proj",
    )(x, g.reshape(1, d), w)


def _key_to_f32(key):
    bits = jnp.where(key >= 0, key, key ^ jnp.int32(0x7FFFFFFF))
    return pltpu.bitcast(bits, F32)


def _dsa_kernel(q_ref, iq_ref, ikw_ref, az_ref, k_ref, v_ref, kikw_ref, o_ref,
                kb_ref, vb_ref, kie_ref, kio_ref, iqb_ref, wt_ref, qs_ref, sc_ref, bias_ref,
                cnt_ref, gt_ref, m_ref, ans_ref, *pad_refs,
                tq, tq_in, nq, n_tiles, n_valid_keys, pos0, topk):
    jq = pl.program_id(1)
    kt_sz = KEY_TILE
    topk_f = float(topk)

    @pl.when(jq == 0)
    def _():
        kb_ref[...] = k_ref[...].astype(BF16)
        for kt in range(n_tiles):
            vb_ref[kt] = v_ref[kt].T.astype(BF16)
            ki = kikw_ref[kt]
            lane = lax.broadcasted_iota(jnp.int32, ki.shape, 1)
            kie = jnp.where(lane < IDX_DIM, ki, 0.0)
            kie_ref[kt] = kie.astype(BF16)
            kio_ref[kt] = pltpu.roll(kie, IDX_DIM, axis=1).astype(BF16)

    if tq_in < tq:
        qp_ref, iqp_ref, ikwp_ref, azp_ref = pad_refs
        for dst, src in ((qp_ref, q_ref), (iqp_ref, iq_ref), (ikwp_ref, ikw_ref), (azp_ref, az_ref)):
            dst[...] = jnp.zeros(dst.shape, dst.dtype)
            dst[0:tq_in, :] = src[...]
        q_ref, iq_ref, ikw_ref, az_ref = qp_ref, iqp_ref, ikwp_ref, azp_ref

    iqb_ref[...] = iq_ref[...].astype(BF16)
    wt_ref[...] = ikw_ref[...].T * IDX_SCALE
    q = q_ref[...].astype(BF16)
    for head in range(A_HEADS):
        g, hh = divmod(head, A_GROUP)
        qs_ref[g, hh * tq:(hh + 1) * tq, :] = q[:, A_HEAD_DIM * head:A_HEAD_DIM * (head + 1)]

    q_last = pos0 + (jq + 1) * tq_in - 1
    key_end = jnp.minimum((q_last // CHUNK + 1) * CHUNK, n_valid_keys)
    n_need = (key_end + kt_sz - 1) // kt_sz

    def for_needed_tiles(body):
        for kt in range(n_tiles):
            pl.when(kt < n_need)(functools.partial(body, kt))

    krow = lax.broadcasted_iota(jnp.int32, (kt_sz, tq), 0)
    qcol = lax.broadcasted_iota(jnp.int32, (kt_sz, tq), 1)
    qchunk = jnp.right_shift(pos0 + jq * tq_in + qcol, CHUNK_SHIFT)

    def score_tile(kt):
        acc = None
        for pair in range(IDX_HEADS // 2):
            iq_pair = iqb_ref[:, LANES * pair:LANES * (pair + 1)]
            for half, kref in enumerate((kie_ref, kio_ref)):
                head = 2 * pair + half
                d = _dot_nt(kref[kt], iq_pair)
                contrib = jnp.maximum(d, 0.0) * wt_ref[IDX_DIM + head:IDX_DIM + head + 1, :]
                acc = contrib if acc is None else acc + contrib
        kpos = kt * kt_sz + krow
        if (kt + 1) * kt_sz > n_valid_keys:
            acc = jnp.where(kpos < n_valid_keys, acc, -jnp.inf)
        sc_ref[kt] = jnp.where(jnp.right_shift(kpos, CHUNK_SHIFT) <= qchunk, acc, -jnp.inf)

    for_needed_tiles(score_tile)

    def count_into(ref, kt, hit):
        ref[...] += jnp.sum(hit.reshape(kt_sz // SUBLANES, SUBLANES, tq), axis=0)

    def total(ref):
        return jnp.sum(ref[...], axis=0, keepdims=True)

    def run_bisection(n_static):
        def bisect(i, ans):
            cand = ans + jnp.left_shift(jnp.int32(1), jnp.int32(31) - i)
            cand_f = _key_to_f32(jnp.maximum(cand, jnp.int32(KEY_NEG_INF)))
            part = jnp.zeros((SUBLANES, tq), F32)
            for kt in range(n_static):
                hit = jnp.where(sc_ref[kt] >= cand_f, 1.0, 0.0)
                part = part + jnp.sum(hit.reshape(kt_sz // SUBLANES, SUBLANES, tq), axis=0)
            return jnp.where(jnp.sum(part, axis=0, keepdims=True) >= topk_f, cand, ans)

        ans = lax.fori_loop(0, 32, bisect, jnp.full((1, tq), INT_MIN, jnp.int32))
        ans_ref[...] = jnp.broadcast_to(ans, ans_ref.shape)

    need_of = []
    for j in range(nq):
        end = min(((pos0 + (j + 1) * tq_in - 1) // CHUNK + 1) * CHUNK, n_valid_keys)
        need_of.append(-(-end // kt_sz))
    for n_static in sorted(set(need_of)):
        blocks = [j for j in range(nq) if need_of[j] == n_static]
        if len(blocks) == nq:
            run_bisection(n_static)
        else:
            in_range = jnp.logical_and(jq >= blocks[0], jq <= blocks[-1])
            pl.when(in_range)(functools.partial(run_bisection, n_static))
    ans = ans_ref[0:1, :]
    thr = _key_to_f32(jnp.maximum(ans, jnp.int32(KEY_NEG_INF)))

    cnt_ref[...] = jnp.zeros(cnt_ref.shape, F32)
    gt_ref[...] = jnp.zeros(gt_ref.shape, F32)

    def count_ties(kt):
        sc = sc_ref[kt]
        count_into(gt_ref, kt, jnp.where(sc > thr, 1.0, 0.0))
        count_into(cnt_ref, kt, jnp.where(sc >= thr, jnp.where(sc > -jnp.inf, 1.0, 0.0), 0.0))

    for_needed_tiles(count_ties)
    cnt_gt = total(gt_ref)
    cnt_ge = total(cnt_ref)
    qcol1 = lax.broadcasted_iota(jnp.int32, (1, tq), 1)
    tie = jnp.where(cnt_ge > topk_f, jnp.where(qcol1 < tq_in, 1.0, 0.0), 0.0)
    m_ref[...] = jnp.full(m_ref.shape, n_tiles * kt_sz, jnp.int32)

    @pl.when(jnp.max(tie) > 0.0)
    def _():
        nbits = max(1, (n_tiles * kt_sz - 1).bit_length())

        def bisect_pos(i, t):
            cand = t + jnp.left_shift(jnp.int32(1), jnp.int32(nbits - 1) - i)
            cnt_ref[...] = jnp.zeros(cnt_ref.shape, F32)

            def count_below(kt):
                hit = jnp.where(sc_ref[kt] == thr, jnp.where(kt * kt_sz + krow < cand, 1.0, 0.0), 0.0)
                count_into(cnt_ref, kt, hit)

            for_needed_tiles(count_below)
            return jnp.where(cnt_gt + total(cnt_ref) < topk_f, cand, t)

        t_last = lax.fori_loop(0, nbits, bisect_pos, jnp.zeros((1, tq), jnp.int32))
        m_ref[...] = jnp.broadcast_to(t_last, m_ref.shape)

    m_last = m_ref[0:1, :]
    thr_eq = jnp.where(thr > -jnp.inf, thr, jnp.inf)

    def bias_tile(kt):
        sc = sc_ref[kt]
        kpos = kt * kt_sz + krow
        tied = jnp.where(sc == thr_eq, jnp.where(kpos <= m_last, 0.0, NEG_BIG), NEG_BIG)
        bias_ref[kt] = jnp.where(sc > thr, 0.0, tied)

    for_needed_tiles(bias_tile)

    to_log2 = (A_HEAD_DIM ** -0.5) * LOG2_E
    cols = A_GROUP * tq

    def attend(kt, carry):
        bias = jnp.concatenate([bias_ref[kt]] * A_GROUP, axis=1)
        out = []
        for g in range(A_KV_HEADS):
            m_run, l_run, acc = carry[g]
            gsl = slice(A_HEAD_DIM * g, A_HEAD_DIM * (g + 1))
            s = _dot_nt(kb_ref[kt, :, gsl], qs_ref[g]) * to_log2 + bias
            m_new = jnp.maximum(m_run, jnp.max(s, axis=0, keepdims=True))
            alpha = jnp.exp2(m_run - m_new)
            p = jnp.exp2(s - m_new)
            l_new = alpha * l_run + jnp.sum(p, axis=0, keepdims=True)
            acc_new = alpha * acc + _dot(vb_ref[kt, gsl, :], p.astype(BF16))
            out.append((m_new, l_new, acc_new))
        return tuple(out)

    init = (jnp.full((1, cols), NEG_BIG, F32), jnp.zeros((1, cols), F32), jnp.zeros((A_HEAD_DIM, cols), F32))
    stats = lax.fori_loop(0, n_need, attend, (init,) * A_KV_HEADS)
    for g in range(A_KV_HEADS):
        _, l_run, acc = stats[g]
        o = (acc / l_run).T
        for hh in range(A_GROUP):
            hs = slice(A_HEAD_DIM * (A_GROUP * g + hh), A_HEAD_DIM * (A_GROUP * g + hh + 1))
            res = (o[hh * tq:(hh + 1) * tq, :] * _silu(az_ref[:, hs])).astype(o_ref.dtype)
            o_ref[:, hs] = res[0:tq_in, :]


def _dsa(u, k_all, v_all, kikw_all, *, n_batch, t_len, n_valid_keys, pos0):
    n_keys = k_all.shape[1]
    n_tiles = n_keys // KEY_TILE
    topk = min(TOPK_MAX, n_valid_keys // 4)
    if t_len % 256 == 0:
        tq = tq_in = 256
    elif t_len % LANES == 0:
        tq = tq_in = LANES
    else:
        tq, tq_in = LANES, t_len
    nq = t_len // tq_in
    row = lambda b, j: b * nq + j
    tiled = lambda t: t.reshape(n_batch, n_tiles, KEY_TILE, t.shape[-1])
    key_spec = lambda width: pl.BlockSpec((None, n_tiles, KEY_TILE, width), lambda b, j: (b, 0, 0, 0))
    pad_scratch = []
    if tq_in < tq:
        pad_scratch = [pltpu.VMEM((tq, A_WIDTH), F32), pltpu.VMEM((tq, IDX_HEADS * IDX_DIM), F32),
                       pltpu.VMEM((tq, LANES), F32), pltpu.VMEM((tq, A_WIDTH), F32)]
    kern = functools.partial(_dsa_kernel, tq=tq, tq_in=tq_in, nq=nq, n_tiles=n_tiles,
                             n_valid_keys=n_valid_keys, pos0=pos0, topk=topk)
    return pl.pallas_call(
        kern,
        grid=(n_batch, nq),
        in_specs=[pl.BlockSpec((tq_in, A_WIDTH), lambda b, j: (row(b, j), COL_AQ // A_WIDTH)),
                  pl.BlockSpec((tq_in, A_WIDTH), lambda b, j: (row(b, j), COL_IQ // A_WIDTH)),
                  pl.BlockSpec((tq_in, LANES), lambda b, j: (row(b, j), COL_IKW // LANES)),
                  pl.BlockSpec((tq_in, A_WIDTH), lambda b, j: (row(b, j), COL_AZ // A_WIDTH)),
                  key_spec(A_KV_WIDTH), key_spec(A_KV_WIDTH), key_spec(LANES)],
        out_specs=pl.BlockSpec((tq_in, A_WIDTH), lambda b, j: (row(b, j), 0)),
        out_shape=jax.ShapeDtypeStruct((n_batch * t_len, A_WIDTH), BF16),
        scratch_shapes=[pltpu.VMEM((n_tiles, KEY_TILE, A_KV_WIDTH), BF16),
                        pltpu.VMEM((n_tiles, A_KV_WIDTH, KEY_TILE), BF16),
                        pltpu.VMEM((n_tiles, KEY_TILE, LANES), BF16),
                        pltpu.VMEM((n_tiles, KEY_TILE, LANES), BF16),
                        pltpu.VMEM((tq, IDX_HEADS * IDX_DIM), BF16),
                        pltpu.VMEM((LANES, tq), F32),
                        pltpu.VMEM((A_KV_HEADS, A_GROUP * tq, A_HEAD_DIM), BF16),
                        pltpu.VMEM((n_tiles, KEY_TILE, tq), F32),
                        pltpu.VMEM((n_tiles, KEY_TILE, tq), F32),
                        pltpu.VMEM((SUBLANES, tq), F32), pltpu.VMEM((SUBLANES, tq), F32),
                        pltpu.VMEM((SUBLANES, tq), jnp.int32),
                        pltpu.VMEM((SUBLANES, tq), jnp.int32)] + pad_scratch,
        compiler_params=_cparams(2),
        name="dsa",
    )(u, u, u, u, tiled(k_all), tiled(v_all), tiled(kikw_all))


def _expand_heads(v, e):
    hi = v.astype(BF16)
    lo = (v - hi.astype(F32)).astype(BF16)
    return _dot(hi, e) + _dot(lo, e)


def _mamba_kernel(*refs, lc, n_in, has_init):
    if has_init:
        (z_ref, xbc_ref, dt_ref, cw_ref, cb_ref, dtb_ref, alog_ref, dsk_ref, nrm_ref, e_ref,
         cprev_ref, h0_ref, y_ref, cst_ref, hl_ref, xpad_ref, st_ref, yacc_ref, *pad_refs) = refs
    else:
        (z_ref, xbc_ref, dt_ref, cw_ref, cb_ref, dtb_ref, alog_ref, dsk_ref, nrm_ref, e_ref,
         y_ref, cst_ref, hl_ref, xpad_ref, st_ref, yacc_ref, *pad_refs) = refs
    c = pl.program_id(1)
    n_chunks = pl.num_programs(1)

    @pl.when(c == 0)
    def _():
        xpad_ref[...] = jnp.zeros(xpad_ref.shape, F32)
        if has_init:
            xpad_ref[0:SUBLANES, :] = cprev_ref[...]
            for g in range(B_GROUPS):
                st_ref[g] = h0_ref[B_GROUP_W * g:B_GROUP_W * (g + 1), :].T
        else:
            st_ref[...] = jnp.zeros(st_ref.shape, F32)

    xpad_ref[SUBLANES:SUBLANES + n_in, :] = xbc_ref[...]
    acc = cb_ref[...] + cw_ref[0:1, :] * xpad_ref[SUBLANES - 3:SUBLANES - 3 + lc, :]
    for j in range(1, B_CONV):
        acc = acc + cw_ref[j:j + 1, :] * xpad_ref[SUBLANES - 3 + j:SUBLANES - 3 + j + lc, :]
    xbc = _silu(acc)

    @pl.when(c == n_chunks - 1)
    def _():
        cst_ref[...] = xpad_ref[SUBLANES + n_in - 3:SUBLANES + n_in, :]

    xpad_ref[0:SUBLANES, :] = xpad_ref[lc:lc + SUBLANES, :]

    if n_in < lc:
        dtp_ref, zp_ref = pad_refs
        dtp_ref[...] = jnp.zeros(dtp_ref.shape, F32)
        dtp_ref[0:n_in, :] = dt_ref[...]
        zp_ref[...] = jnp.zeros(zp_ref.shape, F32)
        zp_ref[0:n_in, :] = z_ref[...]
        dt_raw = dtp_ref[...]
        z_all = zp_ref
    else:
        dt_raw = dt_ref[...]
        z_all = z_ref
    pre = dt_raw + dtb_ref[...]
    dt = jnp.maximum(pre, 0.0) + jnp.log1p(jnp.exp(-jnp.abs(pre)))
    row = lax.broadcasted_iota(jnp.int32, (lc, LANES), 0)
    dt = jnp.where(lax.broadcasted_iota(jnp.int32, (lc, LANES), 1) < B_HEADS, dt, 0.0)
    if n_in < lc:
        dt = jnp.where(row < n_in, dt, 0.0)
    a = -jnp.exp(alog_ref[...])
    cum = dt * a
    shift = 1
    while shift < lc:
        cum = cum + jnp.where(row >= shift, pltpu.roll(cum, shift, axis=0), 0.0)
        shift *= 2
    cum_t = cum.T
    dt_t = dt.T
    c_last = cum[lc - 1:lc, :]
    e = e_ref[...]
    x1 = _expand_heads(jnp.exp(cum), e)
    x2 = _expand_heads(jnp.exp(c_last - cum) * dt, e)
    x3 = _expand_heads(jnp.broadcast_to(jnp.exp(c_last), (SUBLANES, LANES)), e)[0:1, :]

    xs = xbc[:, 0:B_WIDTH]
    xd = (xs * x2).astype(BF16)
    li = lax.broadcasted_iota(jnp.int32, (lc, lc), 0)
    si = lax.broadcasted_iota(jnp.int32, (lc, lc), 1)
    causal = li >= si
    lane = lax.broadcasted_iota(jnp.int32, (lc, LANES), 1)
    dsk = dsk_ref[...]
    for g in range(B_GROUPS):
        bg = xbc[:, B_WIDTH + B_STATE * g:B_WIDTH + B_STATE * (g + 1)]
        cg = xbc[:, B_WIDTH + B_GROUPS * B_STATE + B_STATE * g:B_WIDTH + B_GROUPS * B_STATE + B_STATE * (g + 1)]
        bgb = bg.astype(BF16)
        cgb = cg.astype(BF16)
        cb = _dot_nt(cgb, bgb)
        gs = slice(B_GROUP_W * g, B_GROUP_W * (g + 1))
        state = st_ref[g]
        y_off = _dot(cgb, state.astype(BF16)) * x1[:, gs]
        for pp in range(B_GROUP_W // LANES):
            col = B_GROUP_W * g + LANES * pp
            xp = xs[:, col:col + LANES]
            y_pair = y_off[:, LANES * pp:LANES * (pp + 1)] + dsk[:, col:col + LANES] * xp
            for half in range(2):
                head = col // B_HEAD_DIM + half
                seg = cum[:, head:head + 1] - cum_t[head:head + 1, :]
                wgt = cb * jnp.exp(jnp.where(causal, seg, NEG_BIG)) * dt_t[head:head + 1, :]
                in_half = (lane < B_HEAD_DIM) if half == 0 else (lane >= B_HEAD_DIM)
                xh = jnp.where(in_half, xp, 0.0).astype(BF16)
                y_pair = y_pair + _dot(wgt.astype(BF16), xh)
            yacc_ref[:, col:col + LANES] = y_pair
        st_ref[g] = state * x3[:, gs] + _dot(bg.T.astype(BF16), xd[:, gs])

    for g in range(B_GROUPS):
        gs = slice(B_GROUP_W * g, B_GROUP_W * (g + 1))
        yg = yacc_ref[:, gs] * _silu(z_all[:, gs])
        ms = jnp.mean(yg * yg, axis=-1, keepdims=True)
        out = (yg * lax.rsqrt(ms + EPS) * nrm_ref[:, gs]).astype(y_ref.dtype)
        y_ref[:, gs] = out[0:n_in, :]

    @pl.when(c == n_chunks - 1)
    def _():
        for g in range(B_GROUPS):
            hl_ref[B_GROUP_W * g:B_GROUP_W * (g + 1), :] = st_ref[g].T


def _mamba(u, conv_w, conv_b, dt_bias, a_log, d_skip, ssm_norm, conv_prev, h0, *, n_batch, t_len):
    lc = LANES
    if t_len % lc == 0:
        n_in = lc
    else:
        assert t_len < lc
        n_in = t_len
    nc = t_len // n_in
    has_init = conv_prev is not None
    row = lambda b, c: b * nc + c
    pad1 = lambda v: jnp.concatenate([v.astype(F32), jnp.zeros((LANES - B_HEADS,), F32)]).reshape(1, LANES)
    head_of_col = jnp.arange(B_WIDTH, dtype=jnp.int32) // B_HEAD_DIM
    expand = (jnp.arange(LANES, dtype=jnp.int32)[:, None] == head_of_col[None, :]).astype(BF16)
    dsk_row = jnp.repeat(d_skip.astype(F32), B_HEAD_DIM).reshape(1, B_WIDTH)
    const = lambda shape: pl.BlockSpec(shape, lambda b, c: (0,) * len(shape))
    in_specs = [pl.BlockSpec((n_in, B_WIDTH), lambda b, c: (row(b, c), COL_BZ // B_WIDTH)),
                pl.BlockSpec((n_in, B_CONV_DIM), lambda b, c: (row(b, c), COL_XBC // B_CONV_DIM)),
                pl.BlockSpec((n_in, LANES), lambda b, c: (row(b, c), COL_DT // LANES)),
                const((B_CONV, B_CONV_DIM)), const((1, B_CONV_DIM)), const((1, LANES)), const((1, LANES)),
                const((1, B_WIDTH)), const((1, B_WIDTH)), const((LANES, B_WIDTH))]
    args = [u, u, u, conv_w, conv_b.reshape(1, B_CONV_DIM), pad1(dt_bias), pad1(a_log), dsk_row,
            ssm_norm.reshape(1, B_WIDTH), expand]
    if has_init:
        cprev8 = jnp.concatenate([jnp.zeros((n_batch, SUBLANES - (B_CONV - 1), B_CONV_DIM), F32), conv_prev], axis=1)
        in_specs += [pl.BlockSpec((None, SUBLANES, B_CONV_DIM), lambda b, c: (b, 0, 0)),
                     pl.BlockSpec((None, B_WIDTH, B_STATE), lambda b, c: (b, 0, 0))]
        args += [cprev8, h0.reshape(n_batch, B_WIDTH, B_STATE)]
    pad_scratch = []
    if n_in < lc:
        pad_scratch = [pltpu.VMEM((lc, LANES), F32), pltpu.VMEM((lc, B_WIDTH), F32)]
    kern = functools.partial(_mamba_kernel, lc=lc, n_in=n_in, has_init=has_init)
    y, cst, hl = pl.pallas_call(
        kern,
        grid=(n_batch, nc),
        in_specs=in_specs,
        out_specs=[pl.BlockSpec((n_in, B_WIDTH), lambda b, c: (row(b, c), 0)),
                   pl.BlockSpec((None, B_CONV - 1, B_CONV_DIM), lambda b, c: (b, 0, 0)),
                   pl.BlockSpec((None, B_WIDTH, B_STATE), lambda b, c: (b, 0, 0))],
        out_shape=[jax.ShapeDtypeStruct((n_batch * t_len, B_WIDTH), BF16),
                   jax.ShapeDtypeStruct((n_batch, B_CONV - 1, B_CONV_DIM), F32),
                   jax.ShapeDtypeStruct((n_batch, B_WIDTH, B_STATE), F32)],
        scratch_shapes=[pltpu.VMEM((SUBLANES + lc, B_CONV_DIM), F32),
                        pltpu.VMEM((B_GROUPS, B_STATE, B_GROUP_W), F32),
                        pltpu.VMEM((lc, B_WIDTH), F32)] + pad_scratch,
        compiler_params=_cparams(2),
        name="mamba",
    )(*args)
    return y, cst, hl.reshape(n_batch, B_HEADS, B_HEAD_DIM, B_STATE)


def _mem_kernel(q_ref, z_ref, k_ref, v_ref, o_ref):
    q = q_ref[...].astype(BF16)
    scale = M_HEAD_DIM ** -0.5
    for head in range(M_HEADS):
        hs = slice(M_HEAD_DIM * head, M_HEAD_DIM * (head + 1))
        s = _dot_nt(q[:, hs], k_ref[:, hs].astype(BF16)) * scale
        s_max = jnp.max(s, axis=-1, keepdims=True)
        p = jnp.exp(s - s_max)
        denom = jnp.sum(p, axis=-1, keepdims=True)
        o = _dot(p.astype(BF16), v_ref[:, hs].astype(BF16)) / denom
        o_ref[:, hs] = (o * _silu(z_ref[:, hs])).astype(o_ref.dtype)


def _mem_attend(u, mk, mv, *, n_batch, t_len):
    tq = 512 if t_len % 512 == 0 else t_len
    nq = t_len // tq
    row = lambda b, j: b * nq + j
    return pl.pallas_call(
        _mem_kernel,
        grid=(n_batch, nq),
        in_specs=[pl.BlockSpec((tq, M_WIDTH), lambda b, j: (row(b, j), COL_MQ // M_WIDTH)),
                  pl.BlockSpec((tq, M_WIDTH), lambda b, j: (row(b, j), COL_MZ // M_WIDTH)),
                  pl.BlockSpec((None, N_MEM, M_WIDTH), lambda b, j: (b, 0, 0)),
                  pl.BlockSpec((None, N_MEM, M_WIDTH), lambda b, j: (b, 0, 0))],
        out_specs=pl.BlockSpec((tq, M_WIDTH), lambda b, j: (row(b, j), 0)),
        out_shape=jax.ShapeDtypeStruct((n_batch * t_len, M_WIDTH), BF16),
        compiler_params=_cparams(2),
        name="mem_attend",
    )(u, u, mk, mv)


def _merge_kernel(ya_ref, yb_ref, ym_ref, ga_ref, gb_ref, gm_ref, wa_ref, wb_ref, wm_ref, o_ref):
    merged = _sigmoid(ga_ref[...]) * _dot(ya_ref[...], wa_ref[...])
    merged = merged + _sigmoid(gb_ref[...]) * _dot(yb_ref[...], wb_ref[...])
    merged = merged + _sigmoid(gm_ref[...]) * _dot(ym_ref[...], wm_ref[...])
    o_ref[...] = merged.astype(o_ref.dtype)


def _merge(ya, yb, ym, u, w_pa, w_pb, w_pm, tm):
    n = ya.shape[0]
    rows = lambda width: pl.BlockSpec((tm, width), lambda i: (i, 0))
    gate = lambda k: pl.BlockSpec((tm, D_MODEL), lambda i: (i, COL_GATES // D_MODEL + k))
    weight = lambda width: pl.BlockSpec((width, D_MODEL), lambda i: (0, 0), pipeline_mode=pl.Buffered(1))
    return pl.pallas_call(
        _merge_kernel,
        grid=(n // tm,),
        in_specs=[rows(A_WIDTH), rows(B_WIDTH), rows(M_WIDTH), gate(0), gate(1), gate(2),
                  weight(A_WIDTH), weight(B_WIDTH), weight(M_WIDTH)],
        out_specs=rows(D_MODEL),
        out_shape=jax.ShapeDtypeStruct((n, D_MODEL), BF16),
        compiler_params=_cparams(1),
        name="merge",
    )(ya, yb, ym, u, u, u, w_pa, w_pb, w_pm)


def _final_kernel(m_ref, x_ref, wo_ref, g_ref, o_ref):
    y = x_ref[...] + _dot(m_ref[...], wo_ref[...])
    ms = jnp.mean(y * y, axis=-1, keepdims=True)
    o_ref[...] = y * lax.rsqrt(ms + EPS) * g_ref[...]


def _final(merged, x, w_o, g, tm):
    n = x.shape[0]
    rows = pl.BlockSpec((tm, D_MODEL), lambda i: (i, 0))
    return pl.pallas_call(
        _final_kernel,
        grid=(n // tm,),
        in_specs=[rows, rows,
                  pl.BlockSpec((D_MODEL, D_MODEL), lambda i: (0, 0), pipeline_mode=pl.Buffered(1)),
                  pl.BlockSpec((1, D_MODEL), lambda i: (0, 0))],
        out_specs=rows,
        out_shape=jax.ShapeDtypeStruct((n, D_MODEL), F32),
        compiler_params=_cparams(1),
        name="final",
    )(merged, x, w_o, g.reshape(1, D_MODEL))


def _row_tile(n, pref):
    t = pref
    while n % t:
        t //= 2
    return t


def _pad_keys(t, n_keys):
    pad = n_keys - t.shape[1]
    if pad == 0:
        return t
    return jnp.concatenate([t, jnp.zeros((t.shape[0], pad, t.shape[2]), t.dtype)], axis=1)


def _layer(x, pos0, past_k, past_v, past_ki, conv_prev, h0, mem_k, mem_v, lw, norm_final):
    (norm_in, w_in_packed, conv_w, conv_b, dt_bias, a_log, d_skip, ssm_norm, w_pa, w_pb, w_pm, w_o) = lw
    n_batch, t_len, _ = x.shape
    n = n_batch * t_len
    x2 = x.reshape(n, D_MODEL)
    u = _rms_proj(x2, norm_in, w_in_packed, _row_tile(n, 1024), 1024)

    k_new = u[:, COL_AK:COL_AK + A_KV_WIDTH].reshape(n_batch, t_len, A_KV_WIDTH)
    v_new = u[:, COL_AV:COL_AV + A_KV_WIDTH].reshape(n_batch, t_len, A_KV_WIDTH)
    kikw_new = u[:, COL_IKW:COL_IKW + LANES].reshape(n_batch, t_len, LANES)
    if past_k is None:
        k_all, v_all, kikw_all = k_new, v_new, kikw_new
    else:
        n_past = past_k.shape[1]
        past_kikw = jnp.concatenate([past_ki, jnp.zeros((n_batch, n_past, LANES - IDX_DIM), F32)], axis=2)
        k_all = jnp.concatenate([past_k.reshape(n_batch, n_past, A_KV_WIDTH), k_new], axis=1)
        v_all = jnp.concatenate([past_v.reshape(n_batch, n_past, A_KV_WIDTH), v_new], axis=1)
        kikw_all = jnp.concatenate([past_kikw, kikw_new], axis=1)
    n_valid_keys = k_all.shape[1]
    n_keys = -(-n_valid_keys // KEY_TILE) * KEY_TILE
    ya = _dsa(u, _pad_keys(k_all, n_keys), _pad_keys(v_all, n_keys), _pad_keys(kikw_all, n_keys),
              n_batch=n_batch, t_len=t_len, n_valid_keys=n_valid_keys, pos0=pos0)

    yb, conv_state, h_last = _mamba(u, conv_w, conv_b, dt_bias, a_log, d_skip, ssm_norm, conv_prev, h0,
                                    n_batch=n_batch, t_len=t_len)
    ym = _mem_attend(u, mem_k, mem_v, n_batch=n_batch, t_len=t_len)
    merged = _merge(ya, yb, ym, u, w_pa, w_pb, w_pm, _row_tile(n, 256))
    y = _final(merged, x2, w_o, norm_final, _row_tile(n, 512)).reshape(n_batch, t_len, D_MODEL)
    return (y, k_new.reshape(n_batch, t_len, A_KV_HEADS, A_HEAD_DIM),
            v_new.reshape(n_batch, t_len, A_KV_HEADS, A_HEAD_DIM),
            kikw_new[:, :, 0:IDX_DIM], conv_state, h_last)


def kernel(x_prompt, x_sample, mem_prompt, cache_attn_k, cache_attn_v, cache_idx_k, state_conv, state_ssm,
           cache_mem_k, cache_mem_v, norm_in, w_in, conv_w, conv_b, dt_bias, a_log, d_skip, ssm_norm,
           norm_mem, w_mem_kv, w_pa, w_pb, w_pm, w_o, norm_final):
    depth = w_in.shape[0]
    assert depth == 1, "the final RMSNorm is fused into the (single) layer"
    bp = x_prompt.shape[0]
    bs = x_sample.shape[0]
    first = lambda t: t.reshape(t.shape[1:])
    lw = (first(norm_in), _pack_w_in(w_in), first(conv_w), first(conv_b), first(dt_bias), first(a_log),
          first(d_skip), first(ssm_norm), first(w_pa).astype(BF16), first(w_pb).astype(BF16),
          first(w_pm).astype(BF16), first(w_o).astype(BF16))

    mem2 = mem_prompt.reshape(bp * N_MEM, D_MODEL)
    mkv = _rms_proj(mem2, first(norm_mem), first(w_mem_kv).astype(BF16), _row_tile(bp * N_MEM, 1024), 1024)
    mk_p = mkv[:, 0:M_WIDTH].reshape(bp, N_MEM, M_WIDTH)
    mv_p = mkv[:, M_WIDTH:2 * M_WIDTH].reshape(bp, N_MEM, M_WIDTH)

    yp, kp, vp, kip, convp, ssmp = _layer(x_prompt, 0, None, None, None, None, None, mk_p, mv_p, lw, norm_final)
    ys, ks, vs, kis, convs, ssms = _layer(
        x_sample, PAST_LEN, first(cache_attn_k), first(cache_attn_v), first(cache_idx_k), first(state_conv),
        first(state_ssm), first(cache_mem_k).reshape(bs, N_MEM, M_WIDTH),
        first(cache_mem_v).reshape(bs, N_MEM, M_WIDTH), lw, norm_final)

    st = lambda t: t[None]
    return (yp, ys, st(kp), st(vp), st(kip), st(convp), st(ssmp),
            st(mk_p.reshape(bp, N_MEM, M_HEADS, M_HEAD_DIM)), st(mv_p.reshape(bp, N_MEM, M_HEADS, M_HEAD_DIM)),
            st(ks), st(vs), st(kis), st(convs), st(ssms))
```

```python
import functools

import jax
import jax.numpy as jnp
from jax import lax
from jax.experimental import pallas as pl
from jax.experimental.pallas import tpu as pltpu

F32 = jnp.float32
BF16 = jnp.bfloat16

D_MODEL = 2048
CHUNK = 64
CHUNK_SHIFT = 6
assert 1 << CHUNK_SHIFT == CHUNK
N_MEM = 256
EPS = 1e-6
PAST_LEN = 1024

A_HEADS = 8
A_KV_HEADS = 2
A_HEAD_DIM = 128
A_GROUP = A_HEADS // A_KV_HEADS
A_WIDTH = A_HEADS * A_HEAD_DIM
A_KV_WIDTH = A_KV_HEADS * A_HEAD_DIM
IDX_HEADS = 16
IDX_DIM = 64
TOPK_MAX = 256
IDX_SCALE = (IDX_DIM * IDX_HEADS) ** -0.5

B_WIDTH = 2048
B_HEAD_DIM = 64
B_HEADS = B_WIDTH // B_HEAD_DIM
B_GROUPS = 4
B_HPG = B_HEADS // B_GROUPS
B_STATE = 128
B_CONV = 4
B_CONV_DIM = B_WIDTH + 2 * B_GROUPS * B_STATE
B_GROUP_W = B_WIDTH // B_GROUPS

M_HEADS = 4
M_HEAD_DIM = 256
M_WIDTH = M_HEADS * M_HEAD_DIM

N_BRANCH = 3
IN_SPLITS = (A_WIDTH, A_KV_WIDTH, A_KV_WIDTH, IDX_HEADS * IDX_DIM, IDX_DIM, IDX_HEADS, A_WIDTH,
             B_WIDTH, B_CONV_DIM, B_HEADS, M_WIDTH, M_WIDTH, N_BRANCH * D_MODEL)
IN_COLS = sum(IN_SPLITS)
(SRC_AQ, SRC_AK, SRC_AV, SRC_IQ, SRC_IK, SRC_IW, SRC_AZ, SRC_BZ, SRC_XBC, SRC_DT, SRC_MQ, SRC_MZ,
 SRC_GATES) = (sum(IN_SPLITS[:i]) for i in range(len(IN_SPLITS)))

LANES = 128
SUBLANES = 8
VMEM_LIMIT_BYTES = 56 * 1024 * 1024

COL_GATES = 0
COL_XBC = 6144
COL_AQ = 9216
COL_IQ = 10240
COL_AZ = 11264
COL_MQ = 12288
COL_MZ = 13312
COL_BZ = 14336
COL_AK = 16384
COL_AV = 16640
COL_IKW = 16896
COL_DT = 17152
PACKED_COLS = 17408

PACK_CHUNK = 256
PACK_ROW_ALIGN = 2 * SUBLANES
PACK_SEGMENTS = ((SRC_GATES, COL_GATES, N_BRANCH * D_MODEL), (SRC_XBC, COL_XBC, B_CONV_DIM),
                 (SRC_AQ, COL_AQ, A_WIDTH), (SRC_IQ, COL_IQ, IDX_HEADS * IDX_DIM), (SRC_AZ, COL_AZ, A_WIDTH),
                 (SRC_MQ, COL_MQ, M_WIDTH), (SRC_MZ, COL_MZ, M_WIDTH), (SRC_BZ, COL_BZ, B_WIDTH),
                 (SRC_AK, COL_AK, A_KV_WIDTH), (SRC_AV, COL_AV, A_KV_WIDTH),
                 (SRC_IK, COL_IKW, PACK_CHUNK), (SRC_DT, COL_DT, PACK_CHUNK))


def _pack_source_table():
    table = [None] * (PACKED_COLS // PACK_CHUNK)
    for src, dst, width in PACK_SEGMENTS:
        assert dst % PACK_CHUNK == 0 and width % PACK_CHUNK == 0 and src % (2 * SUBLANES) == 0
        for off in range(0, width, PACK_CHUNK):
            assert src + off + PACK_CHUNK <= IN_COLS
            table[(dst + off) // PACK_CHUNK] = src + off
    assert all(t is not None for t in table)
    return table


PACK_SOURCE = _pack_source_table()

KEY_TILE = 256
PROJ_SUB_ROWS = 512
LOG2_E = 1.4426950408889634
INT_MIN = -2 ** 31
KEY_NEG_INF = INT_MIN + 0x7FFFFF
NEG_BIG = -1e30


def _cparams(n_grid):
    return pltpu.CompilerParams(dimension_semantics=("arbitrary",) * n_grid,
                                vmem_limit_bytes=VMEM_LIMIT_BYTES)


def _silu(z):
    return z * (1.0 / (1.0 + jnp.exp(-z)))


def _sigmoid(z):
    return 1.0 / (1.0 + jnp.exp(-z))


def _dot(a, b):
    return jnp.dot(a, b, preferred_element_type=F32)


def _dot_nt(a, b):
    return lax.dot_general(a, b, (((1,), (1,)), ((), ())), preferred_element_type=F32)


def _pack_kernel(src_ref, wt_ref, o_ref):
    del src_ref
    o_ref[...] = wt_ref[...].T.astype(o_ref.dtype)


def _pack_w_in(w):
    wt = jnp.transpose(w.reshape(D_MODEL, IN_COLS))
    grid_spec = pltpu.PrefetchScalarGridSpec(
        num_scalar_prefetch=1,
        grid=(PACKED_COLS // PACK_CHUNK,),
        in_specs=[pl.BlockSpec((pl.Element(PACK_CHUNK), pl.Element(D_MODEL)),
                               lambda i, src: (src[i] * PACK_ROW_ALIGN, 0))],
        out_specs=pl.BlockSpec((D_MODEL, PACK_CHUNK), lambda i, src: (0, i)))
    return pl.pallas_call(
        _pack_kernel,
        grid_spec=grid_spec,
        out_shape=jax.ShapeDtypeStruct((D_MODEL, PACKED_COLS), BF16),
        compiler_params=_cparams(1),
        name="pack_w_in",
    )(jnp.asarray([s // PACK_ROW_ALIGN for s in PACK_SOURCE], jnp.int32), wt)


def _proj_kernel(x_ref, g_ref, w_ref, o_ref, h_ref):
    @pl.when(pl.program_id(1) == 0)
    def _():
        x = x_ref[...]
        ms = jnp.mean(x * x, axis=-1, keepdims=True)
        h_ref[...] = (x * lax.rsqrt(ms + EPS) * g_ref[...]).astype(BF16)

    tm = o_ref.shape[0]
    sub = min(tm, PROJ_SUB_ROWS)
    for r in range(tm // sub):
        o_ref[r * sub:(r + 1) * sub, :] = _dot(h_ref[r * sub:(r + 1) * sub, :], w_ref[...])


def _rms_proj(x, g, w, tm, tn):
    m, d = x.shape
    n = w.shape[1]
    return pl.pallas_call(
        _proj_kernel,
        grid=(m // tm, n // tn),
        in_specs=[pl.BlockSpec((tm, d), lambda i, j: (i, 0)),
                  pl.BlockSpec((1, d), lambda i, j: (0, 0)),
                  pl.BlockSpec((d, tn), lambda i, j: (0, j))],
        out_specs=pl.BlockSpec((tm, tn), lambda i, j: (i, j)),
        out_shape=jax.ShapeDtypeStruct((m, n), F32),
        scratch_shapes=[pltpu.VMEM((tm, d), BF16)],
        compiler_params=_cparams(2),
        name="rms_proj",
    )(x, g.reshape(1, d), w)


def _key_to_f32(key):
    bits = jnp.where(key >= 0, key, key ^ jnp.int32(0x7FFFFFFF))
    return pltpu.bitcast(bits, F32)


def _dsa_kernel(q_ref, iq_ref, ikw_ref, az_ref, k_ref, v_ref, kikw_ref, o_ref,
                kb_ref, vb_ref, kie_ref, kio_ref, iqb_ref, wt_ref, qs_ref, sc_ref, bias_ref,
                cnt_ref, gt_ref, m_ref, ans_ref, s_ref, p_ref, acc_ref, stat_ref, *pad_refs,
                tq, tq_in, nq, n_tiles, n_valid_keys, pos0, topk):
    jq = pl.program_id(1)
    kt_sz = KEY_TILE
    topk_f = float(topk)

    @pl.when(jq == 0)
    def _():
        kb_ref[...] = k_ref[...].astype(BF16)
        for kt in range(n_tiles):
            v_t = v_ref[kt].T.astype(BF16)
            for g in range(A_KV_HEADS):
                vb_ref[kt, g] = v_t[A_HEAD_DIM * g:A_HEAD_DIM * (g + 1), :]
            ki = kikw_ref[kt]
            lane = lax.broadcasted_iota(jnp.int32, ki.shape, 1)
            kie = jnp.where(lane < IDX_DIM, ki, 0.0)
            kie_ref[kt] = kie.astype(BF16)
            kio_ref[kt] = pltpu.roll(kie, IDX_DIM, axis=1).astype(BF16)

    if tq_in < tq:
        qp_ref, iqp_ref, ikwp_ref, azp_ref = pad_refs
        for dst, src in ((qp_ref, q_ref), (iqp_ref, iq_ref), (ikwp_ref, ikw_ref), (azp_ref, az_ref)):
            dst[...] = jnp.zeros(dst.shape, dst.dtype)
            dst[0:tq_in, :] = src[...]
        q_ref, iq_ref, ikw_ref, az_ref = qp_ref, iqp_ref, ikwp_ref, azp_ref

    iqb_ref[...] = iq_ref[...].astype(BF16)
    wt_ref[...] = ikw_ref[...].T * IDX_SCALE
    q = (q_ref[...] * ((A_HEAD_DIM ** -0.5) * LOG2_E)).astype(BF16)
    for head in range(A_HEADS):
        g, hh = divmod(head, A_GROUP)
        qs_ref[g, hh * tq:(hh + 1) * tq, :] = q[:, A_HEAD_DIM * head:A_HEAD_DIM * (head + 1)]

    q_last = pos0 + (jq + 1) * tq_in - 1
    key_end = jnp.minimum((q_last // CHUNK + 1) * CHUNK, n_valid_keys)
    n_need = (key_end + kt_sz - 1) // kt_sz

    def for_needed_tiles(body):
        for kt in range(n_tiles):
            pl.when(kt < n_need)(functools.partial(body, kt))

    krow = lax.broadcasted_iota(jnp.int32, (kt_sz, tq), 0)
    qcol = lax.broadcasted_iota(jnp.int32, (kt_sz, tq), 1)
    qchunk = jnp.right_shift(pos0 + jq * tq_in + qcol, CHUNK_SHIFT)

    def score_tile(kt):
        acc = None
        for pair in range(IDX_HEADS // 2):
            iq_pair = iqb_ref[:, LANES * pair:LANES * (pair + 1)]
            for half, kref in enumerate((kie_ref, kio_ref)):
                head = 2 * pair + half
                d = _dot_nt(kref[kt], iq_pair)
                contrib = jnp.maximum(d, 0.0) * wt_ref[IDX_DIM + head:IDX_DIM + head + 1, :]
                acc = contrib if acc is None else acc + contrib
        kpos = kt * kt_sz + krow
        if (kt + 1) * kt_sz > n_valid_keys:
            acc = jnp.where(kpos < n_valid_keys, acc, -jnp.inf)
        sc_ref[kt] = jnp.where(jnp.right_shift(kpos, CHUNK_SHIFT) <= qchunk, acc, -jnp.inf)

    for_needed_tiles(score_tile)

    def count_into(ref, kt, hit):
        ref[...] += jnp.sum(hit.reshape(kt_sz // SUBLANES, SUBLANES, tq), axis=0)

    def total(ref):
        return jnp.sum(ref[...], axis=0, keepdims=True)

    def run_bisection(n_static):
        def bisect(i, carry):
            ans, cnt_ans = carry
            cand = ans + jnp.left_shift(jnp.int32(1), jnp.int32(31) - i)
            cand_f = _key_to_f32(jnp.maximum(cand, jnp.int32(KEY_NEG_INF)))
            part = jnp.zeros((SUBLANES, tq), F32)
            for kt in range(n_static):
                hit = jnp.where(sc_ref[kt] >= cand_f, 1.0, 0.0)
                part = part + jnp.sum(hit.reshape(kt_sz // SUBLANES, SUBLANES, tq), axis=0)
            cnt = jnp.sum(part, axis=0, keepdims=True)
            keep = cnt >= topk_f
            return jnp.where(keep, cand, ans), jnp.where(keep, cnt, cnt_ans)

        init = (jnp.full((1, tq), INT_MIN, jnp.int32), jnp.zeros((1, tq), F32))
        ans, cnt_ans = lax.fori_loop(0, 32, bisect, init)
        ans_ref[...] = jnp.broadcast_to(ans, ans_ref.shape)
        cnt_ref[...] = jnp.broadcast_to(cnt_ans, cnt_ref.shape)

    need_of = []
    for j in range(nq):
        end = min(((pos0 + (j + 1) * tq_in - 1) // CHUNK + 1) * CHUNK, n_valid_keys)
        need_of.append(-(-end // kt_sz))
    for n_static in sorted(set(need_of)):
        blocks = [j for j in range(nq) if need_of[j] == n_static]
        if len(blocks) == nq:
            run_bisection(n_static)
        else:
            in_range = jnp.logical_and(jq >= blocks[0], jq <= blocks[-1])
            pl.when(in_range)(functools.partial(run_bisection, n_static))
    ans = ans_ref[0:1, :]
    thr = _key_to_f32(jnp.maximum(ans, jnp.int32(KEY_NEG_INF)))

    cnt_ge = cnt_ref[0:1, :]
    qcol1 = lax.broadcasted_iota(jnp.int32, (1, tq), 1)
    tie = jnp.where(cnt_ge > topk_f, jnp.where(thr > -jnp.inf, jnp.where(qcol1 < tq_in, 1.0, 0.0), 0.0), 0.0)
    m_ref[...] = jnp.full(m_ref.shape, n_tiles * kt_sz, jnp.int32)

    @pl.when(jnp.max(tie) > 0.0)
    def _():
        nbits = max(1, (n_tiles * kt_sz - 1).bit_length())
        gt_ref[...] = jnp.zeros(gt_ref.shape, F32)
        for_needed_tiles(lambda kt: count_into(gt_ref, kt, jnp.where(sc_ref[kt] > thr, 1.0, 0.0)))
        cnt_gt = total(gt_ref)

        def bisect_pos(i, t):
            cand = t + jnp.left_shift(jnp.int32(1), jnp.int32(nbits - 1) - i)
            cnt_ref[...] = jnp.zeros(cnt_ref.shape, F32)

            def count_below(kt):
                hit = jnp.where(sc_ref[kt] == thr, jnp.where(kt * kt_sz + krow < cand, 1.0, 0.0), 0.0)
                count_into(cnt_ref, kt, hit)

            for_needed_tiles(count_below)
            return jnp.where(cnt_gt + total(cnt_ref) < topk_f, cand, t)

        t_last = lax.fori_loop(0, nbits, bisect_pos, jnp.zeros((1, tq), jnp.int32))
        m_ref[...] = jnp.broadcast_to(t_last, m_ref.shape)

    m_last = m_ref[0:1, :]
    thr_eq = jnp.where(thr > -jnp.inf, thr, jnp.inf)

    def bias_tile(kt):
        sc = sc_ref[kt]
        kpos = kt * kt_sz + krow
        tied = jnp.where(sc == thr_eq, jnp.where(kpos <= m_last, 0.0, NEG_BIG), NEG_BIG)
        bias_ref[kt] = jnp.where(sc > thr, 0.0, tied)

    for_needed_tiles(bias_tile)

    cols = A_GROUP * tq
    groups = range(A_KV_HEADS)
    gsl = [slice(A_HEAD_DIM * g, A_HEAD_DIM * (g + 1)) for g in groups]
    ROW_MAX, ROW_SUM, ROW_ALPHA, ROW_TILE_MAX = 0, 1, 2, 3

    def stat(g, k):
        return stat_ref[SUBLANES * g + k:SUBLANES * g + k + 1, :]

    def set_stat(g, k, v):
        stat_ref[SUBLANES * g + k:SUBLANES * g + k + 1, :] = v

    def produce(kt, slot):
        bias = jnp.concatenate([bias_ref[kt]] * A_GROUP, axis=1)
        for g in groups:
            s = _dot_nt(kb_ref[kt, :, gsl[g]], qs_ref[g]) + bias
            s_ref[slot, g] = s
            set_stat(g, ROW_TILE_MAX + slot, jnp.max(s, axis=0, keepdims=True))

    def consume(kt, slot):
        k_prev = jnp.maximum(kt - 1, 0)
        for g in groups:
            pv = _dot(vb_ref[k_prev, g], p_ref[g])
            acc_ref[g] = acc_ref[g] * stat(g, ROW_ALPHA) + pv
            m_run = stat(g, ROW_MAX)
            m = jnp.maximum(m_run, stat(g, ROW_TILE_MAX + slot))
            alpha = jnp.exp2(m_run - m)
            p = jnp.exp2(s_ref[slot, g] - m)
            set_stat(g, ROW_SUM, alpha * stat(g, ROW_SUM) + jnp.sum(p, axis=0, keepdims=True))
            p_ref[g] = p.astype(BF16)
            set_stat(g, ROW_MAX, m)
            set_stat(g, ROW_ALPHA, alpha)

    acc_ref[...] = jnp.zeros(acc_ref.shape, F32)
    p_ref[...] = jnp.zeros(p_ref.shape, BF16)
    for g in groups:
        set_stat(g, ROW_MAX, jnp.full((1, cols), NEG_BIG, F32))
        set_stat(g, ROW_ALPHA, jnp.ones((1, cols), F32))
        set_stat(g, ROW_SUM, jnp.zeros((1, cols), F32))
    produce(0, 0)

    def tile_pair(i, carry):
        k0 = 2 * i
        produce(jnp.minimum(k0 + 1, n_need - 1), 1)
        consume(k0, 0)

        @pl.when(k0 + 1 < n_need)
        def _():
            produce(jnp.minimum(k0 + 2, n_need - 1), 0)
            consume(k0 + 1, 1)

        return carry

    lax.fori_loop(0, (n_need + 1) // 2, tile_pair, 0)
    for g in groups:
        acc = acc_ref[g] * stat(g, ROW_ALPHA) + _dot(vb_ref[n_need - 1, g], p_ref[g])
        o = (acc / stat(g, ROW_SUM)).T
        for hh in range(A_GROUP):
            hs = slice(A_HEAD_DIM * (A_GROUP * g + hh), A_HEAD_DIM * (A_GROUP * g + hh + 1))
            res = (o[hh * tq:(hh + 1) * tq, :] * _silu(az_ref[:, hs])).astype(o_ref.dtype)
            o_ref[:, hs] = res[0:tq_in, :]


def _dsa(u, k_all, v_all, kikw_all, *, n_batch, t_len, n_valid_keys, pos0):
    n_keys = k_all.shape[1]
    n_tiles = n_keys // KEY_TILE
    topk = min(TOPK_MAX, n_valid_keys // 4)
    if t_len % 256 == 0:
        tq = tq_in = 256
    elif t_len % LANES == 0:
        tq = tq_in = LANES
    else:
        tq, tq_in = LANES, t_len
    nq = t_len // tq_in
    row = lambda b, j: b * nq + j
    tiled = lambda t: t.reshape(n_batch, n_tiles, KEY_TILE, t.shape[-1])
    key_spec = lambda width: pl.BlockSpec((None, n_tiles, KEY_TILE, width), lambda b, j: (b, 0, 0, 0))
    pad_scratch = []
    if tq_in < tq:
        pad_scratch = [pltpu.VMEM((tq, A_WIDTH), F32), pltpu.VMEM((tq, IDX_HEADS * IDX_DIM), F32),
                       pltpu.VMEM((tq, LANES), F32), pltpu.VMEM((tq, A_WIDTH), F32)]
    kern = functools.partial(_dsa_kernel, tq=tq, tq_in=tq_in, nq=nq, n_tiles=n_tiles,
                             n_valid_keys=n_valid_keys, pos0=pos0, topk=topk)
    return pl.pallas_call(
        kern,
        grid=(n_batch, nq),
        in_specs=[pl.BlockSpec((tq_in, A_WIDTH), lambda b, j: (row(b, j), COL_AQ // A_WIDTH)),
                  pl.BlockSpec((tq_in, A_WIDTH), lambda b, j: (row(b, j), COL_IQ // A_WIDTH)),
                  pl.BlockSpec((tq_in, LANES), lambda b, j: (row(b, j), COL_IKW // LANES)),
                  pl.BlockSpec((tq_in, A_WIDTH), lambda b, j: (row(b, j), COL_AZ // A_WIDTH)),
                  key_spec(A_KV_WIDTH), key_spec(A_KV_WIDTH), key_spec(LANES)],
        out_specs=pl.BlockSpec((tq_in, A_WIDTH), lambda b, j: (row(b, j), 0)),
        out_shape=jax.ShapeDtypeStruct((n_batch * t_len, A_WIDTH), BF16),
        scratch_shapes=[pltpu.VMEM((n_tiles, KEY_TILE, A_KV_WIDTH), BF16),
                        pltpu.VMEM((n_tiles, A_KV_HEADS, A_HEAD_DIM, KEY_TILE), BF16),
                        pltpu.VMEM((n_tiles, KEY_TILE, LANES), BF16),
                        pltpu.VMEM((n_tiles, KEY_TILE, LANES), BF16),
                        pltpu.VMEM((tq, IDX_HEADS * IDX_DIM), BF16),
                        pltpu.VMEM((LANES, tq), F32),
                        pltpu.VMEM((A_KV_HEADS, A_GROUP * tq, A_HEAD_DIM), BF16),
                        pltpu.VMEM((n_tiles, KEY_TILE, tq), F32),
                        pltpu.VMEM((n_tiles, KEY_TILE, tq), F32),
                        pltpu.VMEM((SUBLANES, tq), F32), pltpu.VMEM((SUBLANES, tq), F32),
                        pltpu.VMEM((SUBLANES, tq), jnp.int32),
                        pltpu.VMEM((SUBLANES, tq), jnp.int32),
                        pltpu.VMEM((2, A_KV_HEADS, KEY_TILE, A_GROUP * tq), F32),
                        pltpu.VMEM((A_KV_HEADS, KEY_TILE, A_GROUP * tq), BF16),
                        pltpu.VMEM((A_KV_HEADS, A_HEAD_DIM, A_GROUP * tq), F32),
                        pltpu.VMEM((A_KV_HEADS * SUBLANES, A_GROUP * tq), F32)] + pad_scratch,
        compiler_params=_cparams(2),
        name="dsa",
    )(u, u, u, u, tiled(k_all), tiled(v_all), tiled(kikw_all))


def _expand_heads(v, e):
    hi = v.astype(BF16)
    lo = (v - hi.astype(F32)).astype(BF16)
    return _dot(hi, e) + _dot(lo, e)


def _mamba_kernel(*refs, lc, n_in, has_init):
    if has_init:
        (z_ref, xbc_ref, dt_ref, cw_ref, cb_ref, dtb_ref, alog_ref, dsk_ref, nrm_ref, e_ref,
         cprev_ref, h0_ref, y_ref, cst_ref, hl_ref, xpad_ref, st_ref, yacc_ref, *pad_refs) = refs
    else:
        (z_ref, xbc_ref, dt_ref, cw_ref, cb_ref, dtb_ref, alog_ref, dsk_ref, nrm_ref, e_ref,
         y_ref, cst_ref, hl_ref, xpad_ref, st_ref, yacc_ref, *pad_refs) = refs
    c = pl.program_id(1)
    n_chunks = pl.num_programs(1)

    @pl.when(c == 0)
    def _():
        xpad_ref[...] = jnp.zeros(xpad_ref.shape, F32)
        if has_init:
            xpad_ref[0:SUBLANES, :] = cprev_ref[...]
            for g in range(B_GROUPS):
                st_ref[g] = h0_ref[B_GROUP_W * g:B_GROUP_W * (g + 1), :].T
        else:
            st_ref[...] = jnp.zeros(st_ref.shape, F32)

    xpad_ref[SUBLANES:SUBLANES + n_in, :] = xbc_ref[...]
    acc = cb_ref[...] + cw_ref[0:1, :] * xpad_ref[SUBLANES - 3:SUBLANES - 3 + lc, :]
    for j in range(1, B_CONV):
        acc = acc + cw_ref[j:j + 1, :] * xpad_ref[SUBLANES - 3 + j:SUBLANES - 3 + j + lc, :]
    xbc = _silu(acc)

    @pl.when(c == n_chunks - 1)
    def _():
        cst_ref[...] = xpad_ref[SUBLANES + n_in - 3:SUBLANES + n_in, :]

    xpad_ref[0:SUBLANES, :] = xpad_ref[lc:lc + SUBLANES, :]

    if n_in < lc:
        dtp_ref, zp_ref = pad_refs
        dtp_ref[...] = jnp.zeros(dtp_ref.shape, F32)
        dtp_ref[0:n_in, :] = dt_ref[...]
        zp_ref[...] = jnp.zeros(zp_ref.shape, F32)
        zp_ref[0:n_in, :] = z_ref[...]
        dt_raw = dtp_ref[...]
        z_all = zp_ref
    else:
        dt_raw = dt_ref[...]
        z_all = z_ref
    pre = dt_raw + dtb_ref[...]
    dt = jnp.maximum(pre, 0.0) + jnp.log1p(jnp.exp(-jnp.abs(pre)))
    row = lax.broadcasted_iota(jnp.int32, (lc, LANES), 0)
    dt = jnp.where(lax.broadcasted_iota(jnp.int32, (lc, LANES), 1) < B_HEADS, dt, 0.0)
    if n_in < lc:
        dt = jnp.where(row < n_in, dt, 0.0)
    a = -jnp.exp(alog_ref[...])
    cum = dt * a
    shift = 1
    while shift < lc:
        cum = cum + jnp.where(row >= shift, pltpu.roll(cum, shift, axis=0), 0.0)
        shift *= 2
    cum_t = cum.T
    dt_t = dt.T
    c_last = cum[lc - 1:lc, :]
    e = e_ref[...]
    x1 = _expand_heads(jnp.exp(cum), e)
    x2 = _expand_heads(jnp.exp(c_last - cum) * dt, e)
    x3 = _expand_heads(jnp.broadcast_to(jnp.exp(c_last), (SUBLANES, LANES)), e)[0:1, :]

    xs = xbc[:, 0:B_WIDTH]
    xd = (xs * x2).astype(BF16)
    li = lax.broadcasted_iota(jnp.int32, (lc, lc), 0)
    si = lax.broadcasted_iota(jnp.int32, (lc, lc), 1)
    causal = li >= si
    lane = lax.broadcasted_iota(jnp.int32, (lc, LANES), 1)
    dsk = dsk_ref[...]
    for g in range(B_GROUPS):
        bg = xbc[:, B_WIDTH + B_STATE * g:B_WIDTH + B_STATE * (g + 1)]
        cg = xbc[:, B_WIDTH + B_GROUPS * B_STATE + B_STATE * g:B_WIDTH + B_GROUPS * B_STATE + B_STATE * (g + 1)]
        bgb = bg.astype(BF16)
        cgb = cg.astype(BF16)
        cb = _dot_nt(cgb, bgb)
        gs = slice(B_GROUP_W * g, B_GROUP_W * (g + 1))
        state = st_ref[g]
        y_off = _dot(cgb, state.astype(BF16)) * x1[:, gs]
        for pp in range(B_GROUP_W // LANES):
            col = B_GROUP_W * g + LANES * pp
            xp = xs[:, col:col + LANES]
            y_pair = y_off[:, LANES * pp:LANES * (pp + 1)] + dsk[:, col:col + LANES] * xp
            for half in range(2):
                head = col // B_HEAD_DIM + half
                seg = cum[:, head:head + 1] - cum_t[head:head + 1, :]
                wgt = cb * jnp.exp(jnp.where(causal, seg, NEG_BIG)) * dt_t[head:head + 1, :]
                in_half = (lane < B_HEAD_DIM) if half == 0 else (lane >= B_HEAD_DIM)
                xh = jnp.where(in_half, xp, 0.0).astype(BF16)
                y_pair = y_pair + _dot(wgt.astype(BF16), xh)
            yacc_ref[:, col:col + LANES] = y_pair
        st_ref[g] = state * x3[:, gs] + _dot(bg.T.astype(BF16), xd[:, gs])

    for g in range(B_GROUPS):
        gs = slice(B_GROUP_W * g, B_GROUP_W * (g + 1))
        yg = yacc_ref[:, gs] * _silu(z_all[:, gs])
        ms = jnp.mean(yg * yg, axis=-1, keepdims=True)
        out = (yg * lax.rsqrt(ms + EPS) * nrm_ref[:, gs]).astype(y_ref.dtype)
        y_ref[:, gs] = out[0:n_in, :]

    @pl.when(c == n_chunks - 1)
    def _():
        for g in range(B_GROUPS):
            hl_ref[B_GROUP_W * g:B_GROUP_W * (g + 1), :] = st_ref[g].T


def _mamba(u, conv_w, conv_b, dt_bias, a_log, d_skip, ssm_norm, conv_prev, h0, *, n_batch, t_len):
    lc = LANES
    if t_len % lc == 0:
        n_in = lc
    else:
        assert t_len < lc
        n_in = t_len
    nc = t_len // n_in
    has_init = conv_prev is not None
    row = lambda b, c: b * nc + c
    pad1 = lambda v: jnp.concatenate([v.astype(F32), jnp.zeros((LANES - B_HEADS,), F32)]).reshape(1, LANES)
    head_of_col = jnp.arange(B_WIDTH, dtype=jnp.int32) // B_HEAD_DIM
    expand = (jnp.arange(LANES, dtype=jnp.int32)[:, None] == head_of_col[None, :]).astype(BF16)
    dsk_row = jnp.repeat(d_skip.astype(F32), B_HEAD_DIM).reshape(1, B_WIDTH)
    const = lambda shape: pl.BlockSpec(shape, lambda b, c: (0,) * len(shape))
    in_specs = [pl.BlockSpec((n_in, B_WIDTH), lambda b, c: (row(b, c), COL_BZ // B_WIDTH)),
                pl.BlockSpec((n_in, B_CONV_DIM), lambda b, c: (row(b, c), COL_XBC // B_CONV_DIM)),
                pl.BlockSpec((n_in, LANES), lambda b, c: (row(b, c), COL_DT // LANES)),
                const((B_CONV, B_CONV_DIM)), const((1, B_CONV_DIM)), const((1, LANES)), const((1, LANES)),
                const((1, B_WIDTH)), const((1, B_WIDTH)), const((LANES, B_WIDTH))]
    args = [u, u, u, conv_w, conv_b.reshape(1, B_CONV_DIM), pad1(dt_bias), pad1(a_log), dsk_row,
            ssm_norm.reshape(1, B_WIDTH), expand]
    if has_init:
        cprev8 = jnp.concatenate([jnp.zeros((n_batch, SUBLANES - (B_CONV - 1), B_CONV_DIM), F32), conv_prev], axis=1)
        in_specs += [pl.BlockSpec((None, SUBLANES, B_CONV_DIM), lambda b, c: (b, 0, 0)),
                     pl.BlockSpec((None, B_WIDTH, B_STATE), lambda b, c: (b, 0, 0))]
        args += [cprev8, h0.reshape(n_batch, B_WIDTH, B_STATE)]
    pad_scratch = []
    if n_in < lc:
        pad_scratch = [pltpu.VMEM((lc, LANES), F32), pltpu.VMEM((lc, B_WIDTH), F32)]
    kern = functools.partial(_mamba_kernel, lc=lc, n_in=n_in, has_init=has_init)
    y, cst, hl = pl.pallas_call(
        kern,
        grid=(n_batch, nc),
        in_specs=in_specs,
        out_specs=[pl.BlockSpec((n_in, B_WIDTH), lambda b, c: (row(b, c), 0)),
                   pl.BlockSpec((None, B_CONV - 1, B_CONV_DIM), lambda b, c: (b, 0, 0)),
                   pl.BlockSpec((None, B_WIDTH, B_STATE), lambda b, c: (b, 0, 0))],
        out_shape=[jax.ShapeDtypeStruct((n_batch * t_len, B_WIDTH), BF16),
                   jax.ShapeDtypeStruct((n_batch, B_CONV - 1, B_CONV_DIM), F32),
                   jax.ShapeDtypeStruct((n_batch, B_WIDTH, B_STATE), F32)],
        scratch_shapes=[pltpu.VMEM((SUBLANES + lc, B_CONV_DIM), F32),
                        pltpu.VMEM((B_GROUPS, B_STATE, B_GROUP_W), F32),
                        pltpu.VMEM((lc, B_WIDTH), F32)] + pad_scratch,
        compiler_params=_cparams(2),
        name="mamba",
    )(*args)
    return y, cst, hl.reshape(n_batch, B_HEADS, B_HEAD_DIM, B_STATE)


def _mem_kernel(q_ref, z_ref, k_ref, v_ref, o_ref):
    q = q_ref[...].astype(BF16)
    scale = M_HEAD_DIM ** -0.5
    for head in range(M_HEADS):
        hs = slice(M_HEAD_DIM * head, M_HEAD_DIM * (head + 1))
        s = _dot_nt(q[:, hs], k_ref[:, hs].astype(BF16)) * scale
        s_max = jnp.max(s, axis=-1, keepdims=True)
        p = jnp.exp(s - s_max)
        denom = jnp.sum(p, axis=-1, keepdims=True)
        o = _dot(p.astype(BF16), v_ref[:, hs].astype(BF16)) / denom
        o_ref[:, hs] = (o * _silu(z_ref[:, hs])).astype(o_ref.dtype)


def _mem_attend(u, mk, mv, *, n_batch, t_len):
    tq = 512 if t_len % 512 == 0 else t_len
    nq = t_len // tq
    row = lambda b, j: b * nq + j
    return pl.pallas_call(
        _mem_kernel,
        grid=(n_batch, nq),
        in_specs=[pl.BlockSpec((tq, M_WIDTH), lambda b, j: (row(b, j), COL_MQ // M_WIDTH)),
                  pl.BlockSpec((tq, M_WIDTH), lambda b, j: (row(b, j), COL_MZ // M_WIDTH)),
                  pl.BlockSpec((None, N_MEM, M_WIDTH), lambda b, j: (b, 0, 0)),
                  pl.BlockSpec((None, N_MEM, M_WIDTH), lambda b, j: (b, 0, 0))],
        out_specs=pl.BlockSpec((tq, M_WIDTH), lambda b, j: (row(b, j), 0)),
        out_shape=jax.ShapeDtypeStruct((n_batch * t_len, M_WIDTH), BF16),
        compiler_params=_cparams(2),
        name="mem_attend",
    )(u, u, mk, mv)


def _merge_kernel(ya_ref, yb_ref, ym_ref, ga_ref, gb_ref, gm_ref, wa_ref, wb_ref, wm_ref, o_ref):
    merged = _sigmoid(ga_ref[...]) * _dot(ya_ref[...], wa_ref[...])
    merged = merged + _sigmoid(gb_ref[...]) * _dot(yb_ref[...], wb_ref[...])
    merged = merged + _sigmoid(gm_ref[...]) * _dot(ym_ref[...], wm_ref[...])
    o_ref[...] = merged.astype(o_ref.dtype)


def _merge(ya, yb, ym, u, w_pa, w_pb, w_pm, tm):
    n = ya.shape[0]
    rows = lambda width: pl.BlockSpec((tm, width), lambda i: (i, 0))
    gate = lambda k: pl.BlockSpec((tm, D_MODEL), lambda i: (i, COL_GATES // D_MODEL + k))
    weight = lambda width: pl.BlockSpec((width, D_MODEL), lambda i: (0, 0), pipeline_mode=pl.Buffered(1))
    return pl.pallas_call(
        _merge_kernel,
        grid=(n // tm,),
        in_specs=[rows(A_WIDTH), rows(B_WIDTH), rows(M_WIDTH), gate(0), gate(1), gate(2),
                  weight(A_WIDTH), weight(B_WIDTH), weight(M_WIDTH)],
        out_specs=rows(D_MODEL),
        out_shape=jax.ShapeDtypeStruct((n, D_MODEL), BF16),
        compiler_params=_cparams(1),
        name="merge",
    )(ya, yb, ym, u, u, u, w_pa, w_pb, w_pm)


def _final_kernel(m_ref, x_ref, wo_ref, g_ref, o_ref):
    y = x_ref[...] + _dot(m_ref[...], wo_ref[...])
    ms = jnp.mean(y * y, axis=-1, keepdims=True)
    o_ref[...] = y * lax.rsqrt(ms + EPS) * g_ref[...]


def _final(merged, x, w_o, g, tm):
    n = x.shape[0]
    rows = pl.BlockSpec((tm, D_MODEL), lambda i: (i, 0))
    return pl.pallas_call(
        _final_kernel,
        grid=(n // tm,),
        in_specs=[rows, rows,
                  pl.BlockSpec((D_MODEL, D_MODEL), lambda i: (0, 0), pipeline_mode=pl.Buffered(1)),
                  pl.BlockSpec((1, D_MODEL), lambda i: (0, 0))],
        out_specs=rows,
        out_shape=jax.ShapeDtypeStruct((n, D_MODEL), F32),
        compiler_params=_cparams(1),
        name="final",
    )(merged, x, w_o, g.reshape(1, D_MODEL))


def _row_tile(n, pref):
    t = pref
    while n % t:
        t //= 2
    return t


def _pad_keys(t, n_keys):
    pad = n_keys - t.shape[1]
    if pad == 0:
        return t
    return jnp.concatenate([t, jnp.zeros((t.shape[0], pad, t.shape[2]), t.dtype)], axis=1)


def _layer(x, pos0, past_k, past_v, past_ki, conv_prev, h0, mem_k, mem_v, lw, norm_final):
    (norm_in, w_in_packed, conv_w, conv_b, dt_bias, a_log, d_skip, ssm_norm, w_pa, w_pb, w_pm, w_o) = lw
    n_batch, t_len, _ = x.shape
    n = n_batch * t_len
    x2 = x.reshape(n, D_MODEL)
    u = _rms_proj(x2, norm_in, w_in_packed, _row_tile(n, 1024), 1024)

    k_new = u[:, COL_AK:COL_AK + A_KV_WIDTH].reshape(n_batch, t_len, A_KV_WIDTH)
    v_new = u[:, COL_AV:COL_AV + A_KV_WIDTH].reshape(n_batch, t_len, A_KV_WIDTH)
    kikw_new = u[:, COL_IKW:COL_IKW + LANES].reshape(n_batch, t_len, LANES)
    if past_k is None:
        k_all, v_all, kikw_all = k_new, v_new, kikw_new
    else:
        n_past = past_k.shape[1]
        past_kikw = jnp.concatenate([past_ki, jnp.zeros((n_batch, n_past, LANES - IDX_DIM), F32)], axis=2)
        k_all = jnp.concatenate([past_k.reshape(n_batch, n_past, A_KV_WIDTH), k_new], axis=1)
        v_all = jnp.concatenate([past_v.reshape(n_batch, n_past, A_KV_WIDTH), v_new], axis=1)
        kikw_all = jnp.concatenate([past_kikw, kikw_new], axis=1)
    n_valid_keys = k_all.shape[1]
    n_keys = -(-n_valid_keys // KEY_TILE) * KEY_TILE
    ya = _dsa(u, _pad_keys(k_all, n_keys), _pad_keys(v_all, n_keys), _pad_keys(kikw_all, n_keys),
              n_batch=n_batch, t_len=t_len, n_valid_keys=n_valid_keys, pos0=pos0)

    yb, conv_state, h_last = _mamba(u, conv_w, conv_b, dt_bias, a_log, d_skip, ssm_norm, conv_prev, h0,
                                    n_batch=n_batch, t_len=t_len)
    ym = _mem_attend(u, mem_k, mem_v, n_batch=n_batch, t_len=t_len)
    merged = _merge(ya, yb, ym, u, w_pa, w_pb, w_pm, _row_tile(n, 256))
    y = _final(merged, x2, w_o, norm_final, _row_tile(n, 512)).reshape(n_batch, t_len, D_MODEL)
    return (y, k_new.reshape(n_batch, t_len, A_KV_HEADS, A_HEAD_DIM),
            v_new.reshape(n_batch, t_len, A_KV_HEADS, A_HEAD_DIM),
            kikw_new[:, :, 0:IDX_DIM], conv_state, h_last)


def kernel(x_prompt, x_sample, mem_prompt, cache_attn_k, cache_attn_v, cache_idx_k, state_conv, state_ssm,
           cache_mem_k, cache_mem_v, norm_in, w_in, conv_w, conv_b, dt_bias, a_log, d_skip, ssm_norm,
           norm_mem, w_mem_kv, w_pa, w_pb, w_pm, w_o, norm_final):
    depth = w_in.shape[0]
    assert depth == 1, "the final RMSNorm is fused into the (single) layer"
    bp = x_prompt.shape[0]
    bs = x_sample.shape[0]
    first = lambda t: t.reshape(t.shape[1:])
    lw = (first(norm_in), _pack_w_in(w_in), first(conv_w), first(conv_b), first(dt_bias), first(a_log),
          first(d_skip), first(ssm_norm), first(w_pa).astype(BF16), first(w_pb).astype(BF16),
          first(w_pm).astype(BF16), first(w_o).astype(BF16))

    mem2 = mem_prompt.reshape(bp * N_MEM, D_MODEL)
    mkv = _rms_proj(mem2, first(norm_mem), first(w_mem_kv).astype(BF16), _row_tile(bp * N_MEM, 1024), 1024)
    mk_p = mkv[:, 0:M_WIDTH].reshape(bp, N_MEM, M_WIDTH)
    mv_p = mkv[:, M_WIDTH:2 * M_WIDTH].reshape(bp, N_MEM, M_WIDTH)

    yp, kp, vp, kip, convp, ssmp = _layer(x_prompt, 0, None, None, None, None, None, mk_p, mv_p, lw, norm_final)
    ys, ks, vs, kis, convs, ssms = _layer(
        x_sample, PAST_LEN, first(cache_attn_k), first(cache_attn_v), first(cache_idx_k), first(state_conv),
        first(state_ssm), first(cache_mem_k).reshape(bs, N_MEM, M_WIDTH),
        first(cache_mem_v).reshape(bs, N_MEM, M_WIDTH), lw, norm_final)

    st = lambda t: t[None]
    return (yp, ys, st(kp), st(vp), st(kip), st(convp), st(ssmp),
            st(mk_p.reshape(bp, N_MEM, M_HEADS, M_HEAD_DIM)), st(mv_p.reshape(bp, N_MEM, M_HEADS, M_HEAD_DIM)),
            st(ks), st(vs), st(kis), st(convs), st(ssms))
```

```python
import functools

import jax
import jax.numpy as jnp
from jax import lax
from jax.experimental import pallas as pl
from jax.experimental.pallas import tpu as pltpu

F32 = jnp.float32
BF16 = jnp.bfloat16

D_MODEL = 2048
CHUNK = 64
CHUNK_SHIFT = 6
assert 1 << CHUNK_SHIFT == CHUNK
N_MEM = 256
EPS = 1e-6
PAST_LEN = 1024

A_HEADS = 8
A_KV_HEADS = 2
A_HEAD_DIM = 128
A_GROUP = A_HEADS // A_KV_HEADS
A_WIDTH = A_HEADS * A_HEAD_DIM
A_KV_WIDTH = A_KV_HEADS * A_HEAD_DIM
IDX_HEADS = 16
IDX_DIM = 64
TOPK_MAX = 256
IDX_SCALE = (IDX_DIM * IDX_HEADS) ** -0.5

B_WIDTH = 2048
B_HEAD_DIM = 64
B_HEADS = B_WIDTH // B_HEAD_DIM
B_GROUPS = 4
B_HPG = B_HEADS // B_GROUPS
B_STATE = 128
B_CONV = 4
B_CONV_DIM = B_WIDTH + 2 * B_GROUPS * B_STATE
B_GROUP_W = B_WIDTH // B_GROUPS

M_HEADS = 4
M_HEAD_DIM = 256
M_WIDTH = M_HEADS * M_HEAD_DIM

N_BRANCH = 3
IN_SPLITS = (A_WIDTH, A_KV_WIDTH, A_KV_WIDTH, IDX_HEADS * IDX_DIM, IDX_DIM, IDX_HEADS, A_WIDTH,
             B_WIDTH, B_CONV_DIM, B_HEADS, M_WIDTH, M_WIDTH, N_BRANCH * D_MODEL)
IN_COLS = sum(IN_SPLITS)
(SRC_AQ, SRC_AK, SRC_AV, SRC_IQ, SRC_IK, SRC_IW, SRC_AZ, SRC_BZ, SRC_XBC, SRC_DT, SRC_MQ, SRC_MZ,
 SRC_GATES) = (sum(IN_SPLITS[:i]) for i in range(len(IN_SPLITS)))

LANES = 128
SUBLANES = 8
VMEM_LIMIT_BYTES = 56 * 1024 * 1024

COL_GATES = 0
COL_XBC = 6144
COL_AQ = 9216
COL_IQ = 10240
COL_AZ = 11264
COL_MQ = 12288
COL_MZ = 13312
COL_BZ = 14336
COL_AK = 16384
COL_AV = 16640
COL_IKW = 16896
COL_DT = 17152
PACKED_COLS = 17408

PACK_CHUNK = 256
PACK_CHUNKS_PER_STEP = 4
PACK_ROW_ALIGN = 2 * SUBLANES
PACK_SEGMENTS = ((SRC_GATES, COL_GATES, N_BRANCH * D_MODEL), (SRC_XBC, COL_XBC, B_CONV_DIM),
                 (SRC_AQ, COL_AQ, A_WIDTH), (SRC_IQ, COL_IQ, IDX_HEADS * IDX_DIM), (SRC_AZ, COL_AZ, A_WIDTH),
                 (SRC_MQ, COL_MQ, M_WIDTH), (SRC_MZ, COL_MZ, M_WIDTH), (SRC_BZ, COL_BZ, B_WIDTH),
                 (SRC_AK, COL_AK, A_KV_WIDTH), (SRC_AV, COL_AV, A_KV_WIDTH),
                 (SRC_IK, COL_IKW, PACK_CHUNK), (SRC_DT, COL_DT, PACK_CHUNK))


def _pack_source_table():
    table = [None] * (PACKED_COLS // PACK_CHUNK)
    for src, dst, width in PACK_SEGMENTS:
        assert dst % PACK_CHUNK == 0 and width % PACK_CHUNK == 0 and src % (2 * SUBLANES) == 0
        for off in range(0, width, PACK_CHUNK):
            assert src + off + PACK_CHUNK <= IN_COLS
            table[(dst + off) // PACK_CHUNK] = src + off
    assert all(t is not None for t in table)
    return table


PACK_SOURCE = _pack_source_table()

KEY_TILE = 256
PROJ_SUB_ROWS = 512
LOG2_E = 1.4426950408889634
INT_MIN = -2 ** 31
KEY_NEG_INF = INT_MIN + 0x7FFFFF
NEG_BIG = -1e30


def _cparams(n_grid):
    return pltpu.CompilerParams(dimension_semantics=("arbitrary",) * n_grid,
                                vmem_limit_bytes=VMEM_LIMIT_BYTES)


def _sigmoid(z):
    return 0.5 * jnp.tanh(0.5 * z) + 0.5


def _silu(z):
    half = 0.5 * z
    return half * jnp.tanh(half) + half


def _dot(a, b):
    return jnp.dot(a, b, preferred_element_type=F32)


def _dot_nt(a, b):
    return lax.dot_general(a, b, (((1,), (1,)), ((), ())), preferred_element_type=F32)


def _pack_kernel(src_ref, *refs):
    del src_ref
    *wt_refs, o_ref = refs
    for k, wt_ref in enumerate(wt_refs):
        o_ref[:, PACK_CHUNK * k:PACK_CHUNK * (k + 1)] = wt_ref[...].T.astype(o_ref.dtype)


def _pack_w_in(w):
    wt = jnp.transpose(w.reshape(D_MODEL, IN_COLS))
    per_step = PACK_CHUNKS_PER_STEP
    window = lambda k: pl.BlockSpec((pl.Element(PACK_CHUNK), pl.Element(D_MODEL)),
                                    lambda i, src: (src[per_step * i + k] * PACK_ROW_ALIGN, 0))
    grid_spec = pltpu.PrefetchScalarGridSpec(
        num_scalar_prefetch=1,
        grid=(PACKED_COLS // (PACK_CHUNK * per_step),),
        in_specs=[window(k) for k in range(per_step)],
        out_specs=pl.BlockSpec((D_MODEL, PACK_CHUNK * per_step), lambda i, src: (0, i)))
    return pl.pallas_call(
        _pack_kernel,
        grid_spec=grid_spec,
        out_shape=jax.ShapeDtypeStruct((D_MODEL, PACKED_COLS), BF16),
        compiler_params=_cparams(1),
        name="pack_w_in",
    )(jnp.asarray([s // PACK_ROW_ALIGN for s in PACK_SOURCE], jnp.int32), *([wt] * per_step))


def _proj_kernel(x_ref, g_ref, w_ref, o_ref, h_ref):
    @pl.when(pl.program_id(1) == 0)
    def _():
        x = x_ref[...]
        ms = jnp.mean(x * x, axis=-1, keepdims=True)
        h_ref[...] = (x * lax.rsqrt(ms + EPS) * g_ref[...]).astype(BF16)

    tm = o_ref.shape[0]
    sub = min(tm, PROJ_SUB_ROWS)
    for r in range(tm // sub):
        o_ref[r * sub:(r + 1) * sub, :] = _dot(h_ref[r * sub:(r + 1) * sub, :], w_ref[...])


def _rms_proj(x, g, w, tm, tn):
    m, d = x.shape
    n = w.shape[1]
    return pl.pallas_call(
        _proj_kernel,
        grid=(m // tm, n // tn),
        in_specs=[pl.BlockSpec((tm, d), lambda i, j: (i, 0)),
                  pl.BlockSpec((1, d), lambda i, j: (0, 0)),
                  pl.BlockSpec((d, tn), lambda i, j: (0, j))],
        out_specs=pl.BlockSpec((tm, tn), lambda i, j: (i, j)),
        out_shape=jax.ShapeDtypeStruct((m, n), F32),
        scratch_shapes=[pltpu.VMEM((tm, d), BF16)],
        compiler_params=_cparams(2),
        name="rms_proj",
    )(x, g.reshape(1, d), w)


def _key_to_f32(key):
    bits = jnp.where(key >= 0, key, key ^ jnp.int32(0x7FFFFFFF))
    return pltpu.bitcast(bits, F32)


def _dsa_kernel(q_ref, iq_ref, ikw_ref, az_ref, k_ref, v_ref, kikw_ref, o_ref,
                kb_ref, vb_ref, kie_ref, kio_ref, iqb_ref, wt_ref, qs_ref, sc_ref, bias_ref,
                cnt_ref, gt_ref, m_ref, ans_ref, s_ref, p_ref, acc_ref, stat_ref, *pad_refs,
                tq, tq_in, nq, n_tiles, n_valid_keys, pos0, topk):
    jq = pl.program_id(1)
    kt_sz = KEY_TILE
    topk_f = float(topk)

    @pl.when(jq == 0)
    def _():
        kb_ref[...] = k_ref[...].astype(BF16)
        for kt in range(n_tiles):
            v_t = v_ref[kt].T.astype(BF16)
            for g in range(A_KV_HEADS):
                vb_ref[kt, g] = v_t[A_HEAD_DIM * g:A_HEAD_DIM * (g + 1), :]
            ki = kikw_ref[kt]
            lane = lax.broadcasted_iota(jnp.int32, ki.shape, 1)
            kie = jnp.where(lane < IDX_DIM, ki, 0.0)
            kie_ref[kt] = kie.astype(BF16)
            kio_ref[kt] = pltpu.roll(kie, IDX_DIM, axis=1).astype(BF16)

    if tq_in < tq:
        qp_ref, iqp_ref, ikwp_ref, azp_ref = pad_refs
        for dst, src in ((qp_ref, q_ref), (iqp_ref, iq_ref), (ikwp_ref, ikw_ref), (azp_ref, az_ref)):
            dst[...] = jnp.zeros(dst.shape, dst.dtype)
            dst[0:tq_in, :] = src[...]
        q_ref, iq_ref, ikw_ref, az_ref = qp_ref, iqp_ref, ikwp_ref, azp_ref

    iqb_ref[...] = iq_ref[...].astype(BF16)
    wt_ref[...] = ikw_ref[...].T * IDX_SCALE
    q = (q_ref[...] * ((A_HEAD_DIM ** -0.5) * LOG2_E)).astype(BF16)
    for head in range(A_HEADS):
        g, hh = divmod(head, A_GROUP)
        qs_ref[g, hh * tq:(hh + 1) * tq, :] = q[:, A_HEAD_DIM * head:A_HEAD_DIM * (head + 1)]

    q_last = pos0 + (jq + 1) * tq_in - 1
    key_end = jnp.minimum((q_last // CHUNK + 1) * CHUNK, n_valid_keys)
    n_need = (key_end + kt_sz - 1) // kt_sz

    def for_needed_tiles(body):
        for kt in range(n_tiles):
            pl.when(kt < n_need)(functools.partial(body, kt))

    krow = lax.broadcasted_iota(jnp.int32, (kt_sz, tq), 0)
    qcol = lax.broadcasted_iota(jnp.int32, (kt_sz, tq), 1)
    qchunk = jnp.right_shift(pos0 + jq * tq_in + qcol, CHUNK_SHIFT)

    def score_tile(kt):
        acc = None
        for pair in range(IDX_HEADS // 2):
            iq_pair = iqb_ref[:, LANES * pair:LANES * (pair + 1)]
            for half, kref in enumerate((kie_ref, kio_ref)):
                head = 2 * pair + half
                d = _dot_nt(kref[kt], iq_pair)
                contrib = jnp.maximum(d, 0.0) * wt_ref[IDX_DIM + head:IDX_DIM + head + 1, :]
                acc = contrib if acc is None else acc + contrib
        kpos = kt * kt_sz + krow
        if (kt + 1) * kt_sz > n_valid_keys:
            acc = jnp.where(kpos < n_valid_keys, acc, -jnp.inf)
        sc_ref[kt] = jnp.where(jnp.right_shift(kpos, CHUNK_SHIFT) <= qchunk, acc, -jnp.inf)

    for_needed_tiles(score_tile)

    def count_into(ref, kt, hit):
        ref[...] += jnp.sum(hit.reshape(kt_sz // SUBLANES, SUBLANES, tq), axis=0)

    def total(ref):
        return jnp.sum(ref[...], axis=0, keepdims=True)

    def run_bisection(n_static):
        def bisect(i, carry):
            ans, cnt_ans = carry
            cand = ans + jnp.left_shift(jnp.int32(1), jnp.int32(31) - i)
            cand_f = _key_to_f32(jnp.maximum(cand, jnp.int32(KEY_NEG_INF)))
            part = jnp.zeros((SUBLANES, tq), F32)
            for kt in range(n_static):
                hit = jnp.where(sc_ref[kt] >= cand_f, 1.0, 0.0)
                part = part + jnp.sum(hit.reshape(kt_sz // SUBLANES, SUBLANES, tq), axis=0)
            cnt = jnp.sum(part, axis=0, keepdims=True)
            keep = cnt >= topk_f
            return jnp.where(keep, cand, ans), jnp.where(keep, cnt, cnt_ans)

        init = (jnp.full((1, tq), INT_MIN, jnp.int32), jnp.zeros((1, tq), F32))
        ans, cnt_ans = lax.fori_loop(0, 32, bisect, init)
        ans_ref[...] = jnp.broadcast_to(ans, ans_ref.shape)
        cnt_ref[...] = jnp.broadcast_to(cnt_ans, cnt_ref.shape)

    need_of = []
    for j in range(nq):
        end = min(((pos0 + (j + 1) * tq_in - 1) // CHUNK + 1) * CHUNK, n_valid_keys)
        need_of.append(-(-end // kt_sz))
    for n_static in sorted(set(need_of)):
        blocks = [j for j in range(nq) if need_of[j] == n_static]
        if len(blocks) == nq:
            run_bisection(n_static)
        else:
            in_range = jnp.logical_and(jq >= blocks[0], jq <= blocks[-1])
            pl.when(in_range)(functools.partial(run_bisection, n_static))
    ans = ans_ref[0:1, :]
    thr = _key_to_f32(jnp.maximum(ans, jnp.int32(KEY_NEG_INF)))

    cnt_ge = cnt_ref[0:1, :]
    qcol1 = lax.broadcasted_iota(jnp.int32, (1, tq), 1)
    tie = jnp.where(cnt_ge > topk_f, jnp.where(thr > -jnp.inf, jnp.where(qcol1 < tq_in, 1.0, 0.0), 0.0), 0.0)
    m_ref[...] = jnp.full(m_ref.shape, n_tiles * kt_sz, jnp.int32)

    @pl.when(jnp.max(tie) > 0.0)
    def _():
        nbits = max(1, (n_tiles * kt_sz - 1).bit_length())
        gt_ref[...] = jnp.zeros(gt_ref.shape, F32)
        for_needed_tiles(lambda kt: count_into(gt_ref, kt, jnp.where(sc_ref[kt] > thr, 1.0, 0.0)))
        cnt_gt = total(gt_ref)

        def bisect_pos(i, t):
            cand = t + jnp.left_shift(jnp.int32(1), jnp.int32(nbits - 1) - i)
            cnt_ref[...] = jnp.zeros(cnt_ref.shape, F32)

            def count_below(kt):
                hit = jnp.where(sc_ref[kt] == thr, jnp.where(kt * kt_sz + krow < cand, 1.0, 0.0), 0.0)
                count_into(cnt_ref, kt, hit)

            for_needed_tiles(count_below)
            return jnp.where(cnt_gt + total(cnt_ref) < topk_f, cand, t)

        t_last = lax.fori_loop(0, nbits, bisect_pos, jnp.zeros((1, tq), jnp.int32))
        m_ref[...] = jnp.broadcast_to(t_last, m_ref.shape)

    m_last = m_ref[0:1, :]
    thr_eq = jnp.where(thr > -jnp.inf, thr, jnp.inf)

    def bias_tile(kt):
        sc = sc_ref[kt]
        kpos = kt * kt_sz + krow
        tied = jnp.where(sc == thr_eq, jnp.where(kpos <= m_last, 0.0, NEG_BIG), NEG_BIG)
        bias_ref[kt] = jnp.where(sc > thr, 0.0, tied)

    for_needed_tiles(bias_tile)

    cols = A_GROUP * tq
    groups = range(A_KV_HEADS)
    gsl = [slice(A_HEAD_DIM * g, A_HEAD_DIM * (g + 1)) for g in groups]
    ROW_MAX, ROW_SUM, ROW_ALPHA, ROW_TILE_MAX = 0, 1, 2, 3

    def stat(g, k):
        return stat_ref[SUBLANES * g + k:SUBLANES * g + k + 1, :]

    def set_stat(g, k, v):
        stat_ref[SUBLANES * g + k:SUBLANES * g + k + 1, :] = v

    def produce(kt, slot):
        bias = jnp.concatenate([bias_ref[kt]] * A_GROUP, axis=1)
        for g in groups:
            s = _dot_nt(kb_ref[kt, :, gsl[g]], qs_ref[g]) + bias
            s_ref[slot, g] = s
            set_stat(g, ROW_TILE_MAX + slot, jnp.max(s, axis=0, keepdims=True))

    def consume(kt, slot):
        k_prev = jnp.maximum(kt - 1, 0)
        for g in groups:
            pv = _dot(vb_ref[k_prev, g], p_ref[g])
            acc_ref[g] = acc_ref[g] * stat(g, ROW_ALPHA) + pv
            m_run = stat(g, ROW_MAX)
            m = jnp.maximum(m_run, stat(g, ROW_TILE_MAX + slot))
            alpha = jnp.exp2(m_run - m)
            p = jnp.exp2(s_ref[slot, g] - m)
            set_stat(g, ROW_SUM, alpha * stat(g, ROW_SUM) + jnp.sum(p, axis=0, keepdims=True))
            p_ref[g] = p.astype(BF16)
            set_stat(g, ROW_MAX, m)
            set_stat(g, ROW_ALPHA, alpha)

    acc_ref[...] = jnp.zeros(acc_ref.shape, F32)
    p_ref[...] = jnp.zeros(p_ref.shape, BF16)
    for g in groups:
        set_stat(g, ROW_MAX, jnp.full((1, cols), NEG_BIG, F32))
        set_stat(g, ROW_ALPHA, jnp.ones((1, cols), F32))
        set_stat(g, ROW_SUM, jnp.zeros((1, cols), F32))
    produce(0, 0)

    def tile_pair(i, carry):
        k0 = 2 * i
        produce(jnp.minimum(k0 + 1, n_need - 1), 1)
        consume(k0, 0)

        @pl.when(k0 + 1 < n_need)
        def _():
            produce(jnp.minimum(k0 + 2, n_need - 1), 0)
            consume(k0 + 1, 1)

        return carry

    lax.fori_loop(0, (n_need + 1) // 2, tile_pair, 0)
    for g in groups:
        acc = acc_ref[g] * stat(g, ROW_ALPHA) + _dot(vb_ref[n_need - 1, g], p_ref[g])
        o = (acc / stat(g, ROW_SUM)).T
        for hh in range(A_GROUP):
            hs = slice(A_HEAD_DIM * (A_GROUP * g + hh), A_HEAD_DIM * (A_GROUP * g + hh + 1))
            res = (o[hh * tq:(hh + 1) * tq, :] * _silu(az_ref[:, hs])).astype(o_ref.dtype)
            o_ref[:, hs] = res[0:tq_in, :]


def _dsa(u, k_all, v_all, kikw_all, *, n_batch, t_len, n_valid_keys, pos0):
    n_keys = k_all.shape[1]
    n_tiles = n_keys // KEY_TILE
    topk = min(TOPK_MAX, n_valid_keys // 4)
    if t_len % 256 == 0:
        tq = tq_in = 256
    elif t_len % LANES == 0:
        tq = tq_in = LANES
    else:
        tq, tq_in = LANES, t_len
    nq = t_len // tq_in
    row = lambda b, j: b * nq + j
    tiled = lambda t: t.reshape(n_batch, n_tiles, KEY_TILE, t.shape[-1])
    key_spec = lambda width: pl.BlockSpec((None, n_tiles, KEY_TILE, width), lambda b, j: (b, 0, 0, 0))
    pad_scratch = []
    if tq_in < tq:
        pad_scratch = [pltpu.VMEM((tq, A_WIDTH), F32), pltpu.VMEM((tq, IDX_HEADS * IDX_DIM), F32),
                       pltpu.VMEM((tq, LANES), F32), pltpu.VMEM((tq, A_WIDTH), F32)]
    kern = functools.partial(_dsa_kernel, tq=tq, tq_in=tq_in, nq=nq, n_tiles=n_tiles,
                             n_valid_keys=n_valid_keys, pos0=pos0, topk=topk)
    return pl.pallas_call(
        kern,
        grid=(n_batch, nq),
        in_specs=[pl.BlockSpec((tq_in, A_WIDTH), lambda b, j: (row(b, j), COL_AQ // A_WIDTH)),
                  pl.BlockSpec((tq_in, A_WIDTH), lambda b, j: (row(b, j), COL_IQ // A_WIDTH)),
                  pl.BlockSpec((tq_in, LANES), lambda b, j: (row(b, j), COL_IKW // LANES)),
                  pl.BlockSpec((tq_in, A_WIDTH), lambda b, j: (row(b, j), COL_AZ // A_WIDTH)),
                  key_spec(A_KV_WIDTH), key_spec(A_KV_WIDTH), key_spec(LANES)],
        out_specs=pl.BlockSpec((tq_in, A_WIDTH), lambda b, j: (row(b, j), 0)),
        out_shape=jax.ShapeDtypeStruct((n_batch * t_len, A_WIDTH), BF16),
        scratch_shapes=[pltpu.VMEM((n_tiles, KEY_TILE, A_KV_WIDTH), BF16),
                        pltpu.VMEM((n_tiles, A_KV_HEADS, A_HEAD_DIM, KEY_TILE), BF16),
                        pltpu.VMEM((n_tiles, KEY_TILE, LANES), BF16),
                        pltpu.VMEM((n_tiles, KEY_TILE, LANES), BF16),
                        pltpu.VMEM((tq, IDX_HEADS * IDX_DIM), BF16),
                        pltpu.VMEM((LANES, tq), F32),
                        pltpu.VMEM((A_KV_HEADS, A_GROUP * tq, A_HEAD_DIM), BF16),
                        pltpu.VMEM((n_tiles, KEY_TILE, tq), F32),
                        pltpu.VMEM((n_tiles, KEY_TILE, tq), F32),
                        pltpu.VMEM((SUBLANES, tq), F32), pltpu.VMEM((SUBLANES, tq), F32),
                        pltpu.VMEM((SUBLANES, tq), jnp.int32),
                        pltpu.VMEM((SUBLANES, tq), jnp.int32),
                        pltpu.VMEM((2, A_KV_HEADS, KEY_TILE, A_GROUP * tq), F32),
                        pltpu.VMEM((A_KV_HEADS, KEY_TILE, A_GROUP * tq), BF16),
                        pltpu.VMEM((A_KV_HEADS, A_HEAD_DIM, A_GROUP * tq), F32),
                        pltpu.VMEM((A_KV_HEADS * SUBLANES, A_GROUP * tq), F32)] + pad_scratch,
        compiler_params=_cparams(2),
        name="dsa",
    )(u, u, u, u, tiled(k_all), tiled(v_all), tiled(kikw_all))


def _expand_heads(v, e):
    hi = v.astype(BF16)
    lo = (v - hi.astype(F32)).astype(BF16)
    return _dot(hi, e) + _dot(lo, e)


def _mamba_kernel(*refs, lc, n_in, has_init):
    if has_init:
        (z_ref, xbc_ref, dt_ref, cw_ref, cb_ref, dtb_ref, alog_ref, dsk_ref, nrm_ref, e_ref,
         cprev_ref, h0_ref, y_ref, cst_ref, hl_ref, xpad_ref, st_ref, yacc_ref, *pad_refs) = refs
    else:
        (z_ref, xbc_ref, dt_ref, cw_ref, cb_ref, dtb_ref, alog_ref, dsk_ref, nrm_ref, e_ref,
         y_ref, cst_ref, hl_ref, xpad_ref, st_ref, yacc_ref, *pad_refs) = refs
    c = pl.program_id(1)
    n_chunks = pl.num_programs(1)

    @pl.when(c == 0)
    def _():
        xpad_ref[...] = jnp.zeros(xpad_ref.shape, F32)
        if has_init:
            xpad_ref[0:SUBLANES, :] = cprev_ref[...]
            for g in range(B_GROUPS):
                st_ref[g] = h0_ref[B_GROUP_W * g:B_GROUP_W * (g + 1), :].T
        else:
            st_ref[...] = jnp.zeros(st_ref.shape, F32)

    xpad_ref[SUBLANES:SUBLANES + n_in, :] = xbc_ref[...]
    acc = cb_ref[...] + cw_ref[0:1, :] * xpad_ref[SUBLANES - 3:SUBLANES - 3 + lc, :]
    for j in range(1, B_CONV):
        acc = acc + cw_ref[j:j + 1, :] * xpad_ref[SUBLANES - 3 + j:SUBLANES - 3 + j + lc, :]
    xbc = _silu(acc)

    @pl.when(c == n_chunks - 1)
    def _():
        cst_ref[...] = xpad_ref[SUBLANES + n_in - 3:SUBLANES + n_in, :]

    xpad_ref[0:SUBLANES, :] = xpad_ref[lc:lc + SUBLANES, :]

    if n_in < lc:
        dtp_ref, zp_ref = pad_refs
        dtp_ref[...] = jnp.zeros(dtp_ref.shape, F32)
        dtp_ref[0:n_in, :] = dt_ref[...]
        zp_ref[...] = jnp.zeros(zp_ref.shape, F32)
        zp_ref[0:n_in, :] = z_ref[...]
        dt_raw = dtp_ref[...]
        z_all = zp_ref
    else:
        dt_raw = dt_ref[...]
        z_all = z_ref
    pre = dt_raw + dtb_ref[...]
    dt = jnp.maximum(pre, 0.0) + jnp.log1p(jnp.exp(-jnp.abs(pre)))
    row = lax.broadcasted_iota(jnp.int32, (lc, LANES), 0)
    dt = jnp.where(lax.broadcasted_iota(jnp.int32, (lc, LANES), 1) < B_HEADS, dt, 0.0)
    if n_in < lc:
        dt = jnp.where(row < n_in, dt, 0.0)
    a = -jnp.exp(alog_ref[...])
    cum = dt * a
    shift = 1
    while shift < lc:
        cum = cum + jnp.where(row >= shift, pltpu.roll(cum, shift, axis=0), 0.0)
        shift *= 2
    cum_t = cum.T
    dt_t = dt.T
    c_last = cum[lc - 1:lc, :]
    e = e_ref[...]
    x1 = _expand_heads(jnp.exp(cum), e)
    x2 = _expand_heads(jnp.exp(c_last - cum) * dt, e)
    x3 = _expand_heads(jnp.broadcast_to(jnp.exp(c_last), (SUBLANES, LANES)), e)[0:1, :]

    xs = xbc[:, 0:B_WIDTH]
    xd = (xs * x2).astype(BF16)
    li = lax.broadcasted_iota(jnp.int32, (lc, lc), 0)
    si = lax.broadcasted_iota(jnp.int32, (lc, lc), 1)
    causal = li >= si
    lane = lax.broadcasted_iota(jnp.int32, (lc, LANES), 1)
    dsk = dsk_ref[...]
    for g in range(B_GROUPS):
        bg = xbc[:, B_WIDTH + B_STATE * g:B_WIDTH + B_STATE * (g + 1)]
        cg = xbc[:, B_WIDTH + B_GROUPS * B_STATE + B_STATE * g:B_WIDTH + B_GROUPS * B_STATE + B_STATE * (g + 1)]
        bgb = bg.astype(BF16)
        cgb = cg.astype(BF16)
        cb = _dot_nt(cgb, bgb)
        gs = slice(B_GROUP_W * g, B_GROUP_W * (g + 1))
        state = st_ref[g]
        y_off = _dot(cgb, state.astype(BF16)) * x1[:, gs]
        for pp in range(B_GROUP_W // LANES):
            col = B_GROUP_W * g + LANES * pp
            xp = xs[:, col:col + LANES]
            y_pair = y_off[:, LANES * pp:LANES * (pp + 1)] + dsk[:, col:col + LANES] * xp
            xpb = xp.astype(BF16)
            y_half = []
            for half in range(2):
                head = col // B_HEAD_DIM + half
                seg = cum[:, head:head + 1] - cum_t[head:head + 1, :]
                wgt = cb * jnp.exp(jnp.where(causal, seg, NEG_BIG)) * dt_t[head:head + 1, :]
                y_half.append(_dot(wgt.astype(BF16), xpb))
            y_pair = y_pair + jnp.where(lane < B_HEAD_DIM, y_half[0], y_half[1])
            yacc_ref[:, col:col + LANES] = y_pair
        st_ref[g] = state * x3[:, gs] + _dot(bg.T.astype(BF16), xd[:, gs])

    for g in range(B_GROUPS):
        gs = slice(B_GROUP_W * g, B_GROUP_W * (g + 1))
        yg = yacc_ref[:, gs] * _silu(z_all[:, gs])
        ms = jnp.mean(yg * yg, axis=-1, keepdims=True)
        out = (yg * lax.rsqrt(ms + EPS) * nrm_ref[:, gs]).astype(y_ref.dtype)
        y_ref[:, gs] = out[0:n_in, :]

    @pl.when(c == n_chunks - 1)
    def _():
        for g in range(B_GROUPS):
            hl_ref[B_GROUP_W * g:B_GROUP_W * (g + 1), :] = st_ref[g].T


def _mamba(u, conv_w, conv_b, dt_bias, a_log, d_skip, ssm_norm, conv_prev, h0, *, n_batch, t_len):
    lc = LANES
    if t_len % lc == 0:
        n_in = lc
    else:
        assert t_len < lc
        n_in = t_len
    nc = t_len // n_in
    has_init = conv_prev is not None
    row = lambda b, c: b * nc + c
    pad1 = lambda v: jnp.concatenate([v.astype(F32), jnp.zeros((LANES - B_HEADS,), F32)]).reshape(1, LANES)
    head_of_col = jnp.arange(B_WIDTH, dtype=jnp.int32) // B_HEAD_DIM
    expand = (jnp.arange(LANES, dtype=jnp.int32)[:, None] == head_of_col[None, :]).astype(BF16)
    dsk_row = jnp.repeat(d_skip.astype(F32), B_HEAD_DIM).reshape(1, B_WIDTH)
    const = lambda shape: pl.BlockSpec(shape, lambda b, c: (0,) * len(shape))
    in_specs = [pl.BlockSpec((n_in, B_WIDTH), lambda b, c: (row(b, c), COL_BZ // B_WIDTH)),
                pl.BlockSpec((n_in, B_CONV_DIM), lambda b, c: (row(b, c), COL_XBC // B_CONV_DIM)),
                pl.BlockSpec((n_in, LANES), lambda b, c: (row(b, c), COL_DT // LANES)),
                const((B_CONV, B_CONV_DIM)), const((1, B_CONV_DIM)), const((1, LANES)), const((1, LANES)),
                const((1, B_WIDTH)), const((1, B_WIDTH)), const((LANES, B_WIDTH))]
    args = [u, u, u, conv_w, conv_b.reshape(1, B_CONV_DIM), pad1(dt_bias), pad1(a_log), dsk_row,
            ssm_norm.reshape(1, B_WIDTH), expand]
    if has_init:
        cprev8 = jnp.concatenate([jnp.zeros((n_batch, SUBLANES - (B_CONV - 1), B_CONV_DIM), F32), conv_prev], axis=1)
        in_specs += [pl.BlockSpec((None, SUBLANES, B_CONV_DIM), lambda b, c: (b, 0, 0)),
                     pl.BlockSpec((None, B_WIDTH, B_STATE), lambda b, c: (b, 0, 0))]
        args += [cprev8, h0.reshape(n_batch, B_WIDTH, B_STATE)]
    pad_scratch = []
    if n_in < lc:
        pad_scratch = [pltpu.VMEM((lc, LANES), F32), pltpu.VMEM((lc, B_WIDTH), F32)]
    kern = functools.partial(_mamba_kernel, lc=lc, n_in=n_in, has_init=has_init)
    y, cst, hl = pl.pallas_call(
        kern,
        grid=(n_batch, nc),
        in_specs=in_specs,
        out_specs=[pl.BlockSpec((n_in, B_WIDTH), lambda b, c: (row(b, c), 0)),
                   pl.BlockSpec((None, B_CONV - 1, B_CONV_DIM), lambda b, c: (b, 0, 0)),
                   pl.BlockSpec((None, B_WIDTH, B_STATE), lambda b, c: (b, 0, 0))],
        out_shape=[jax.ShapeDtypeStruct((n_batch * t_len, B_WIDTH), BF16),
                   jax.ShapeDtypeStruct((n_batch, B_CONV - 1, B_CONV_DIM), F32),
                   jax.ShapeDtypeStruct((n_batch, B_WIDTH, B_STATE), F32)],
        scratch_shapes=[pltpu.VMEM((SUBLANES + lc, B_CONV_DIM), F32),
                        pltpu.VMEM((B_GROUPS, B_STATE, B_GROUP_W), F32),
                        pltpu.VMEM((lc, B_WIDTH), F32)] + pad_scratch,
        compiler_params=_cparams(2),
        name="mamba",
    )(*args)
    return y, cst, hl.reshape(n_batch, B_HEADS, B_HEAD_DIM, B_STATE)


def _mem_kernel(q_ref, z_ref, k_ref, v_ref, o_ref):
    q = q_ref[...].astype(BF16)
    scale = M_HEAD_DIM ** -0.5
    for head in range(M_HEADS):
        hs = slice(M_HEAD_DIM * head, M_HEAD_DIM * (head + 1))
        s = _dot_nt(q[:, hs], k_ref[:, hs].astype(BF16)) * scale
        s_max = jnp.max(s, axis=-1, keepdims=True)
        p = jnp.exp(s - s_max)
        denom = jnp.sum(p, axis=-1, keepdims=True)
        o = _dot(p.astype(BF16), v_ref[:, hs].astype(BF16)) / denom
        o_ref[:, hs] = (o * _silu(z_ref[:, hs])).astype(o_ref.dtype)


def _mem_attend(u, mk, mv, *, n_batch, t_len):
    tq = 512 if t_len % 512 == 0 else t_len
    nq = t_len // tq
    row = lambda b, j: b * nq + j
    return pl.pallas_call(
        _mem_kernel,
        grid=(n_batch, nq),
        in_specs=[pl.BlockSpec((tq, M_WIDTH), lambda b, j: (row(b, j), COL_MQ // M_WIDTH)),
                  pl.BlockSpec((tq, M_WIDTH), lambda b, j: (row(b, j), COL_MZ // M_WIDTH)),
                  pl.BlockSpec((None, N_MEM, M_WIDTH), lambda b, j: (b, 0, 0)),
                  pl.BlockSpec((None, N_MEM, M_WIDTH), lambda b, j: (b, 0, 0))],
        out_specs=pl.BlockSpec((tq, M_WIDTH), lambda b, j: (row(b, j), 0)),
        out_shape=jax.ShapeDtypeStruct((n_batch * t_len, M_WIDTH), BF16),
        compiler_params=_cparams(2),
        name="mem_attend",
    )(u, u, mk, mv)


def _merge_kernel(ya_ref, yb_ref, ym_ref, ga_ref, gb_ref, gm_ref, wa_ref, wb_ref, wm_ref, o_ref):
    merged = _sigmoid(ga_ref[...]) * _dot(ya_ref[...], wa_ref[...])
    merged = merged + _sigmoid(gb_ref[...]) * _dot(yb_ref[...], wb_ref[...])
    merged = merged + _sigmoid(gm_ref[...]) * _dot(ym_ref[...], wm_ref[...])
    o_ref[...] = merged.astype(o_ref.dtype)


def _merge(ya, yb, ym, u, w_pa, w_pb, w_pm, tm):
    n = ya.shape[0]
    rows = lambda width: pl.BlockSpec((tm, width), lambda i: (i, 0))
    gate = lambda k: pl.BlockSpec((tm, D_MODEL), lambda i: (i, COL_GATES // D_MODEL + k))
    weight = lambda width: pl.BlockSpec((width, D_MODEL), lambda i: (0, 0), pipeline_mode=pl.Buffered(1))
    return pl.pallas_call(
        _merge_kernel,
        grid=(n // tm,),
        in_specs=[rows(A_WIDTH), rows(B_WIDTH), rows(M_WIDTH), gate(0), gate(1), gate(2),
                  weight(A_WIDTH), weight(B_WIDTH), weight(M_WIDTH)],
        out_specs=rows(D_MODEL),
        out_shape=jax.ShapeDtypeStruct((n, D_MODEL), BF16),
        compiler_params=_cparams(1),
        name="merge",
    )(ya, yb, ym, u, u, u, w_pa, w_pb, w_pm)


def _final_kernel(m_ref, x_ref, wo_ref, g_ref, o_ref):
    y = x_ref[...] + _dot(m_ref[...], wo_ref[...])
    ms = jnp.mean(y * y, axis=-1, keepdims=True)
    o_ref[...] = y * lax.rsqrt(ms + EPS) * g_ref[...]


def _final(merged, x, w_o, g, tm):
    n = x.shape[0]
    rows = pl.BlockSpec((tm, D_MODEL), lambda i: (i, 0))
    return pl.pallas_call(
        _final_kernel,
        grid=(n // tm,),
        in_specs=[rows, rows,
                  pl.BlockSpec((D_MODEL, D_MODEL), lambda i: (0, 0), pipeline_mode=pl.Buffered(1)),
                  pl.BlockSpec((1, D_MODEL), lambda i: (0, 0))],
        out_specs=rows,
        out_shape=jax.ShapeDtypeStruct((n, D_MODEL), F32),
        compiler_params=_cparams(1),
        name="final",
    )(merged, x, w_o, g.reshape(1, D_MODEL))


def _row_tile(n, pref):
    t = pref
    while n % t:
        t //= 2
    return t


def _pad_keys(t, n_keys):
    pad = n_keys - t.shape[1]
    if pad == 0:
        return t
    return jnp.concatenate([t, jnp.zeros((t.shape[0], pad, t.shape[2]), t.dtype)], axis=1)


def _layer(x, pos0, past_k, past_v, past_ki, conv_prev, h0, mem_k, mem_v, lw, norm_final):
    (norm_in, w_in_packed, conv_w, conv_b, dt_bias, a_log, d_skip, ssm_norm, w_pa, w_pb, w_pm, w_o) = lw
    n_batch, t_len, _ = x.shape
    n = n_batch * t_len
    x2 = x.reshape(n, D_MODEL)
    u = _rms_proj(x2, norm_in, w_in_packed, _row_tile(n, 1024), 1024)

    k_new = u[:, COL_AK:COL_AK + A_KV_WIDTH].reshape(n_batch, t_len, A_KV_WIDTH)
    v_new = u[:, COL_AV:COL_AV + A_KV_WIDTH].reshape(n_batch, t_len, A_KV_WIDTH)
    kikw_new = u[:, COL_IKW:COL_IKW + LANES].reshape(n_batch, t_len, LANES)
    if past_k is None:
        k_all, v_all, kikw_all = k_new, v_new, kikw_new
    else:
        n_past = past_k.shape[1]
        past_kikw = jnp.concatenate([past_ki, jnp.zeros((n_batch, n_past, LANES - IDX_DIM), F32)], axis=2)
        k_all = jnp.concatenate([past_k.reshape(n_batch, n_past, A_KV_WIDTH), k_new], axis=1)
        v_all = jnp.concatenate([past_v.reshape(n_batch, n_past, A_KV_WIDTH), v_new], axis=1)
        kikw_all = jnp.concatenate([past_kikw, kikw_new], axis=1)
    n_valid_keys = k_all.shape[1]
    n_keys = -(-n_valid_keys // KEY_TILE) * KEY_TILE
    ya = _dsa(u, _pad_keys(k_all, n_keys), _pad_keys(v_all, n_keys), _pad_keys(kikw_all, n_keys),
              n_batch=n_batch, t_len=t_len, n_valid_keys=n_valid_keys, pos0=pos0)

    yb, conv_state, h_last = _mamba(u, conv_w, conv_b, dt_bias, a_log, d_skip, ssm_norm, conv_prev, h0,
                                    n_batch=n_batch, t_len=t_len)
    ym = _mem_attend(u, mem_k, mem_v, n_batch=n_batch, t_len=t_len)
    merged = _merge(ya, yb, ym, u, w_pa, w_pb, w_pm, _row_tile(n, 256))
    y = _final(merged, x2, w_o, norm_final, _row_tile(n, 512)).reshape(n_batch, t_len, D_MODEL)
    return (y, k_new.reshape(n_batch, t_len, A_KV_HEADS, A_HEAD_DIM),
            v_new.reshape(n_batch, t_len, A_KV_HEADS, A_HEAD_DIM),
            kikw_new[:, :, 0:IDX_DIM], conv_state, h_last)


def kernel(x_prompt, x_sample, mem_prompt, cache_attn_k, cache_attn_v, cache_idx_k, state_conv, state_ssm,
           cache_mem_k, cache_mem_v, norm_in, w_in, conv_w, conv_b, dt_bias, a_log, d_skip, ssm_norm,
           norm_mem, w_mem_kv, w_pa, w_pb, w_pm, w_o, norm_final):
    depth = w_in.shape[0]
    assert depth == 1, "the final RMSNorm is fused into the (single) layer"
    bp = x_prompt.shape[0]
    bs = x_sample.shape[0]
    first = lambda t: t.reshape(t.shape[1:])
    lw = (first(norm_in), _pack_w_in(w_in), first(conv_w), first(conv_b), first(dt_bias), first(a_log),
          first(d_skip), first(ssm_norm), first(w_pa).astype(BF16), first(w_pb).astype(BF16),
          first(w_pm).astype(BF16), first(w_o).astype(BF16))

    mem2 = mem_prompt.reshape(bp * N_MEM, D_MODEL)
    mkv = _rms_proj(mem2, first(norm_mem), first(w_mem_kv).astype(BF16), _row_tile(bp * N_MEM, 1024), 1024)
    mk_p = mkv[:, 0:M_WIDTH].reshape(bp, N_MEM, M_WIDTH)
    mv_p = mkv[:, M_WIDTH:2 * M_WIDTH].reshape(bp, N_MEM, M_WIDTH)

    yp, kp, vp, kip, convp, ssmp = _layer(x_prompt, 0, None, None, None, None, None, mk_p, mv_p, lw, norm_final)
    ys, ks, vs, kis, convs, ssms = _layer(
        x_sample, PAST_LEN, first(cache_attn_k), first(cache_attn_v), first(cache_idx_k), first(state_conv),
        first(state_ssm), first(cache_mem_k).reshape(bs, N_MEM, M_WIDTH),
        first(cache_mem_v).reshape(bs, N_MEM, M_WIDTH), lw, norm_final)

    st = lambda t: t[None]
    return (yp, ys, st(kp), st(vp), st(kip), st(convp), st(ssmp),
            st(mk_p.reshape(bp, N_MEM, M_HEADS, M_HEAD_DIM)), st(mv_p.reshape(bp, N_MEM, M_HEADS, M_HEAD_DIM)),
            st(ks), st(vs), st(kis), st(convs), st(ssms))
```

```python
import functools

import jax
import jax.numpy as jnp
from jax import lax
from jax.experimental import pallas as pl
from jax.experimental.pallas import tpu as pltpu

F32 = jnp.float32
BF16 = jnp.bfloat16

D_MODEL = 2048
CHUNK = 64
CHUNK_SHIFT = 6
assert 1 << CHUNK_SHIFT == CHUNK
N_MEM = 256
EPS = 1e-6
PAST_LEN = 1024

A_HEADS = 8
A_KV_HEADS = 2
A_HEAD_DIM = 128
A_GROUP = A_HEADS // A_KV_HEADS
A_WIDTH = A_HEADS * A_HEAD_DIM
A_KV_WIDTH = A_KV_HEADS * A_HEAD_DIM
IDX_HEADS = 16
IDX_DIM = 64
TOPK_MAX = 256
IDX_SCALE = (IDX_DIM * IDX_HEADS) ** -0.5

B_WIDTH = 2048
B_HEAD_DIM = 64
B_HEADS = B_WIDTH // B_HEAD_DIM
B_GROUPS = 4
B_HPG = B_HEADS // B_GROUPS
B_STATE = 128
B_CONV = 4
B_CONV_DIM = B_WIDTH + 2 * B_GROUPS * B_STATE
B_GROUP_W = B_WIDTH // B_GROUPS

M_HEADS = 4
M_HEAD_DIM = 256
M_WIDTH = M_HEADS * M_HEAD_DIM

N_BRANCH = 3
IN_SPLITS = (A_WIDTH, A_KV_WIDTH, A_KV_WIDTH, IDX_HEADS * IDX_DIM, IDX_DIM, IDX_HEADS, A_WIDTH,
             B_WIDTH, B_CONV_DIM, B_HEADS, M_WIDTH, M_WIDTH, N_BRANCH * D_MODEL)
IN_COLS = sum(IN_SPLITS)
(SRC_AQ, SRC_AK, SRC_AV, SRC_IQ, SRC_IK, SRC_IW, SRC_AZ, SRC_BZ, SRC_XBC, SRC_DT, SRC_MQ, SRC_MZ,
 SRC_GATES) = (sum(IN_SPLITS[:i]) for i in range(len(IN_SPLITS)))

LANES = 128
SUBLANES = 8
VMEM_LIMIT_BYTES = 56 * 1024 * 1024

COL_GATES = 0
COL_XBC = 6144
COL_AQ = 9216
COL_IQ = 10240
COL_AZ = 11264
COL_MQ = 12288
COL_MZ = 13312
COL_BZ = 14336
COL_AK = 16384
COL_AV = 16640
COL_IKW = 16896
COL_DT = 17152
PACKED_COLS = 17408

PACK_CHUNK = 256
PROJ_COL_TILE = 1024
PACK_CHUNKS_PER_STEP = PROJ_COL_TILE // PACK_CHUNK
PACK_ROW_ALIGN = 2 * SUBLANES
PACK_SEGMENTS = ((SRC_GATES, COL_GATES, N_BRANCH * D_MODEL), (SRC_XBC, COL_XBC, B_CONV_DIM),
                 (SRC_AQ, COL_AQ, A_WIDTH), (SRC_IQ, COL_IQ, IDX_HEADS * IDX_DIM), (SRC_AZ, COL_AZ, A_WIDTH),
                 (SRC_MQ, COL_MQ, M_WIDTH), (SRC_MZ, COL_MZ, M_WIDTH), (SRC_BZ, COL_BZ, B_WIDTH),
                 (SRC_AK, COL_AK, A_KV_WIDTH), (SRC_AV, COL_AV, A_KV_WIDTH),
                 (SRC_IK, COL_IKW, PACK_CHUNK), (SRC_DT, COL_DT, PACK_CHUNK))


def _pack_source_table():
    table = [None] * (PACKED_COLS // PACK_CHUNK)
    for src, dst, width in PACK_SEGMENTS:
        assert dst % PACK_CHUNK == 0 and width % PACK_CHUNK == 0 and src % (2 * SUBLANES) == 0
        for off in range(0, width, PACK_CHUNK):
            assert src + off + PACK_CHUNK <= IN_COLS
            table[(dst + off) // PACK_CHUNK] = src + off
    assert all(t is not None for t in table)
    return table


PACK_SOURCE = _pack_source_table()

KEY_TILE = 256
PROJ_SUB_ROWS = 512
LOG2_E = 1.4426950408889634
INT_MIN = -2 ** 31
KEY_NEG_INF = INT_MIN + 0x7FFFFF
NEG_BIG = -1e30


def _cparams(n_grid):
    return pltpu.CompilerParams(dimension_semantics=("arbitrary",) * n_grid,
                                vmem_limit_bytes=VMEM_LIMIT_BYTES)


def _sigmoid(z):
    return 0.5 * jnp.tanh(0.5 * z) + 0.5


def _silu(z):
    half = 0.5 * z
    return half * jnp.tanh(half) + half


def _dot(a, b):
    return jnp.dot(a, b, preferred_element_type=F32)


def _dot_nt(a, b):
    return lax.dot_general(a, b, (((1,), (1,)), ((), ())), preferred_element_type=F32)


def _pack_kernel(src_ref, *refs):
    del src_ref
    *wt_refs, o_ref = refs
    for k, wt_ref in enumerate(wt_refs):
        o_ref[:, PACK_CHUNK * k:PACK_CHUNK * (k + 1)] = wt_ref[...].T.astype(o_ref.dtype)


def _pack_w_in(w):
    wt = jnp.transpose(w.reshape(D_MODEL, IN_COLS))
    per_step = PACK_CHUNKS_PER_STEP
    window = lambda k: pl.BlockSpec((pl.Element(PACK_CHUNK), pl.Element(D_MODEL)),
                                    lambda i, src: (src[per_step * i + k] * PACK_ROW_ALIGN, 0))
    grid_spec = pltpu.PrefetchScalarGridSpec(
        num_scalar_prefetch=1,
        grid=(PACKED_COLS // (PACK_CHUNK * per_step),),
        in_specs=[window(k) for k in range(per_step)],
        out_specs=pl.BlockSpec((None, D_MODEL, PROJ_COL_TILE), lambda i, src: (i, 0, 0)))
    return pl.pallas_call(
        _pack_kernel,
        grid_spec=grid_spec,
        out_shape=jax.ShapeDtypeStruct((PACKED_COLS // PROJ_COL_TILE, D_MODEL, PROJ_COL_TILE), BF16),
        compiler_params=_cparams(1),
        name="pack_w_in",
    )(jnp.asarray([s // PACK_ROW_ALIGN for s in PACK_SOURCE], jnp.int32), *([wt] * per_step))


def _proj_kernel(x_ref, g_ref, w_ref, o_ref, h_ref):
    @pl.when(pl.program_id(1) == 0)
    def _():
        x = x_ref[...]
        ms = jnp.mean(x * x, axis=-1, keepdims=True)
        h_ref[...] = (x * lax.rsqrt(ms + EPS) * g_ref[...]).astype(BF16)

    tm = o_ref.shape[0]
    sub = min(tm, PROJ_SUB_ROWS)
    for r in range(tm // sub):
        o_ref[r * sub:(r + 1) * sub, :] = _dot(h_ref[r * sub:(r + 1) * sub, :], w_ref[...])


def _rms_proj(x, g, w, tm):
    m, d = x.shape
    n_tiles, _, tn = w.shape
    return pl.pallas_call(
        _proj_kernel,
        grid=(m // tm, n_tiles),
        in_specs=[pl.BlockSpec((tm, d), lambda i, j: (i, 0)),
                  pl.BlockSpec((1, d), lambda i, j: (0, 0)),
                  pl.BlockSpec((None, d, tn), lambda i, j: (j, 0, 0))],
        out_specs=pl.BlockSpec((tm, tn), lambda i, j: (i, j)),
        out_shape=jax.ShapeDtypeStruct((m, n_tiles * tn), F32),
        scratch_shapes=[pltpu.VMEM((tm, d), BF16)],
        compiler_params=_cparams(2),
        name="rms_proj",
    )(x, g.reshape(1, d), w)


def _key_to_f32(key):
    bits = jnp.where(key >= 0, key, key ^ jnp.int32(0x7FFFFFFF))
    return pltpu.bitcast(bits, F32)


def _dsa_kernel(q_ref, iq_ref, ikw_ref, az_ref, k_ref, v_ref, kikw_ref, o_ref,
                kb_ref, vb_ref, kie_ref, kio_ref, iqb_ref, wt_ref, qs_ref, sc_ref, bias_ref,
                cnt_ref, gt_ref, m_ref, ans_ref, s_ref, p_ref, acc_ref, stat_ref, *pad_refs,
                tq, tq_in, nq, n_tiles, n_valid_keys, pos0, topk):
    jq = pl.program_id(1)
    kt_sz = KEY_TILE
    topk_f = float(topk)

    @pl.when(jq == 0)
    def _():
        for kt in range(n_tiles):
            tile = slice(kt_sz * kt, kt_sz * (kt + 1))
            kb_ref[kt] = k_ref[tile, :].astype(BF16)
            v_t = v_ref[tile, :].T.astype(BF16)
            for g in range(A_KV_HEADS):
                vb_ref[kt, g] = v_t[A_HEAD_DIM * g:A_HEAD_DIM * (g + 1), :]
            ki = kikw_ref[tile, :]
            lane = lax.broadcasted_iota(jnp.int32, ki.shape, 1)
            kie = jnp.where(lane < IDX_DIM, ki, 0.0)
            kie_ref[kt] = kie.astype(BF16)
            kio_ref[kt] = pltpu.roll(kie, IDX_DIM, axis=1).astype(BF16)

    if tq_in < tq:
        qp_ref, iqp_ref, ikwp_ref, azp_ref = pad_refs
        for dst, src in ((qp_ref, q_ref), (iqp_ref, iq_ref), (ikwp_ref, ikw_ref), (azp_ref, az_ref)):
            dst[...] = jnp.zeros(dst.shape, dst.dtype)
            dst[0:tq_in, :] = src[...]
        q_ref, iq_ref, ikw_ref, az_ref = qp_ref, iqp_ref, ikwp_ref, azp_ref

    iqb_ref[...] = iq_ref[...].astype(BF16)
    wt_ref[...] = ikw_ref[...].T * IDX_SCALE
    q = (q_ref[...] * ((A_HEAD_DIM ** -0.5) * LOG2_E)).astype(BF16)
    for head in range(A_HEADS):
        g, hh = divmod(head, A_GROUP)
        qs_ref[g, hh * tq:(hh + 1) * tq, :] = q[:, A_HEAD_DIM * head:A_HEAD_DIM * (head + 1)]

    q_last = pos0 + (jq + 1) * tq_in - 1
    key_end = jnp.minimum((q_last // CHUNK + 1) * CHUNK, n_valid_keys)
    n_need = (key_end + kt_sz - 1) // kt_sz

    def for_needed_tiles(body):
        for kt in range(n_tiles):
            pl.when(kt < n_need)(functools.partial(body, kt))

    krow = lax.broadcasted_iota(jnp.int32, (kt_sz, tq), 0)
    qcol = lax.broadcasted_iota(jnp.int32, (kt_sz, tq), 1)
    qchunk = jnp.right_shift(pos0 + jq * tq_in + qcol, CHUNK_SHIFT)

    def score_tile(kt):
        acc = None
        for pair in range(IDX_HEADS // 2):
            iq_pair = iqb_ref[:, LANES * pair:LANES * (pair + 1)]
            for half, kref in enumerate((kie_ref, kio_ref)):
                head = 2 * pair + half
                d = _dot_nt(kref[kt], iq_pair)
                contrib = jnp.maximum(d, 0.0) * wt_ref[IDX_DIM + head:IDX_DIM + head + 1, :]
                acc = contrib if acc is None else acc + contrib
        kpos = kt * kt_sz + krow
        if (kt + 1) * kt_sz > n_valid_keys:
            acc = jnp.where(kpos < n_valid_keys, acc, -jnp.inf)
        sc_ref[kt] = jnp.where(jnp.right_shift(kpos, CHUNK_SHIFT) <= qchunk, acc, -jnp.inf)

    for_needed_tiles(score_tile)

    def count_into(ref, kt, hit):
        ref[...] += jnp.sum(hit.reshape(kt_sz // SUBLANES, SUBLANES, tq), axis=0)

    def total(ref):
        return jnp.sum(ref[...], axis=0, keepdims=True)

    def run_bisection(n_static):
        def bisect(i, carry):
            ans, cnt_ans = carry
            cand = ans + jnp.left_shift(jnp.int32(1), jnp.int32(31) - i)
            cand_f = _key_to_f32(jnp.maximum(cand, jnp.int32(KEY_NEG_INF)))
            part = jnp.zeros((SUBLANES, tq), F32)
            for kt in range(n_static):
                hit = jnp.where(sc_ref[kt] >= cand_f, 1.0, 0.0)
                part = part + jnp.sum(hit.reshape(kt_sz // SUBLANES, SUBLANES, tq), axis=0)
            cnt = jnp.sum(part, axis=0, keepdims=True)
            keep = cnt >= topk_f
            return jnp.where(keep, cand, ans), jnp.where(keep, cnt, cnt_ans)

        init = (jnp.full((1, tq), INT_MIN, jnp.int32), jnp.zeros((1, tq), F32))
        ans, cnt_ans = lax.fori_loop(0, 32, bisect, init)
        ans_ref[...] = jnp.broadcast_to(ans, ans_ref.shape)
        cnt_ref[...] = jnp.broadcast_to(cnt_ans, cnt_ref.shape)

    need_of = []
    for j in range(nq):
        end = min(((pos0 + (j + 1) * tq_in - 1) // CHUNK + 1) * CHUNK, n_valid_keys)
        need_of.append(-(-end // kt_sz))
    for n_static in sorted(set(need_of)):
        blocks = [j for j in range(nq) if need_of[j] == n_static]
        if len(blocks) == nq:
            run_bisection(n_static)
        else:
            in_range = jnp.logical_and(jq >= blocks[0], jq <= blocks[-1])
            pl.when(in_range)(functools.partial(run_bisection, n_static))
    ans = ans_ref[0:1, :]
    thr = _key_to_f32(jnp.maximum(ans, jnp.int32(KEY_NEG_INF)))

    cnt_ge = cnt_ref[0:1, :]
    qcol1 = lax.broadcasted_iota(jnp.int32, (1, tq), 1)
    tie = jnp.where(cnt_ge > topk_f, jnp.where(thr > -jnp.inf, jnp.where(qcol1 < tq_in, 1.0, 0.0), 0.0), 0.0)
    m_ref[...] = jnp.full(m_ref.shape, n_tiles * kt_sz, jnp.int32)

    @pl.when(jnp.max(tie) > 0.0)
    def _():
        nbits = max(1, (n_tiles * kt_sz - 1).bit_length())
        gt_ref[...] = jnp.zeros(gt_ref.shape, F32)
        for_needed_tiles(lambda kt: count_into(gt_ref, kt, jnp.where(sc_ref[kt] > thr, 1.0, 0.0)))
        cnt_gt = total(gt_ref)

        def bisect_pos(i, t):
            cand = t + jnp.left_shift(jnp.int32(1), jnp.int32(nbits - 1) - i)
            cnt_ref[...] = jnp.zeros(cnt_ref.shape, F32)

            def count_below(kt):
                hit = jnp.where(sc_ref[kt] == thr, jnp.where(kt * kt_sz + krow < cand, 1.0, 0.0), 0.0)
                count_into(cnt_ref, kt, hit)

            for_needed_tiles(count_below)
            return jnp.where(cnt_gt + total(cnt_ref) < topk_f, cand, t)

        t_last = lax.fori_loop(0, nbits, bisect_pos, jnp.zeros((1, tq), jnp.int32))
        m_ref[...] = jnp.broadcast_to(t_last, m_ref.shape)

    m_last = m_ref[0:1, :]
    thr_eq = jnp.where(thr > -jnp.inf, thr, jnp.inf)

    def bias_tile(kt):
        sc = sc_ref[kt]
        kpos = kt * kt_sz + krow
        tied = jnp.where(sc == thr_eq, jnp.where(kpos <= m_last, 0.0, NEG_BIG), NEG_BIG)
        bias_ref[kt] = jnp.where(sc > thr, 0.0, tied)

    for_needed_tiles(bias_tile)

    cols = A_GROUP * tq
    groups = range(A_KV_HEADS)
    gsl = [slice(A_HEAD_DIM * g, A_HEAD_DIM * (g + 1)) for g in groups]
    ROW_MAX, ROW_SUM, ROW_ALPHA, ROW_TILE_MAX = 0, 1, 2, 3

    def stat(g, k):
        return stat_ref[SUBLANES * g + k:SUBLANES * g + k + 1, :]

    def set_stat(g, k, v):
        stat_ref[SUBLANES * g + k:SUBLANES * g + k + 1, :] = v

    def produce(kt, slot):
        bias = jnp.concatenate([bias_ref[kt]] * A_GROUP, axis=1)
        for g in groups:
            s = _dot_nt(kb_ref[kt, :, gsl[g]], qs_ref[g]) + bias
            s_ref[slot, g] = s
            set_stat(g, ROW_TILE_MAX + slot, jnp.max(s, axis=0, keepdims=True))

    def consume(kt, slot):
        k_prev = jnp.maximum(kt - 1, 0)
        for g in groups:
            pv = _dot(vb_ref[k_prev, g], p_ref[g])
            acc_ref[g] = acc_ref[g] * stat(g, ROW_ALPHA) + pv
            m_run = stat(g, ROW_MAX)
            m = jnp.maximum(m_run, stat(g, ROW_TILE_MAX + slot))
            alpha = jnp.exp2(m_run - m)
            p = jnp.exp2(s_ref[slot, g] - m)
            set_stat(g, ROW_SUM, alpha * stat(g, ROW_SUM) + jnp.sum(p, axis=0, keepdims=True))
            p_ref[g] = p.astype(BF16)
            set_stat(g, ROW_MAX, m)
            set_stat(g, ROW_ALPHA, alpha)

    acc_ref[...] = jnp.zeros(acc_ref.shape, F32)
    p_ref[...] = jnp.zeros(p_ref.shape, BF16)
    for g in groups:
        set_stat(g, ROW_MAX, jnp.full((1, cols), NEG_BIG, F32))
        set_stat(g, ROW_ALPHA, jnp.ones((1, cols), F32))
        set_stat(g, ROW_SUM, jnp.zeros((1, cols), F32))
    produce(0, 0)

    def tile_pair(i, carry):
        k0 = 2 * i
        produce(jnp.minimum(k0 + 1, n_need - 1), 1)
        consume(k0, 0)

        @pl.when(k0 + 1 < n_need)
        def _():
            produce(jnp.minimum(k0 + 2, n_need - 1), 0)
            consume(k0 + 1, 1)

        return carry

    lax.fori_loop(0, (n_need + 1) // 2, tile_pair, 0)
    for g in groups:
        acc = acc_ref[g] * stat(g, ROW_ALPHA) + _dot(vb_ref[n_need - 1, g], p_ref[g])
        o = (acc / stat(g, ROW_SUM)).T
        for hh in range(A_GROUP):
            hs = slice(A_HEAD_DIM * (A_GROUP * g + hh), A_HEAD_DIM * (A_GROUP * g + hh + 1))
            res = (o[hh * tq:(hh + 1) * tq, :] * _silu(az_ref[:, hs])).astype(o_ref.dtype)
            o_ref[:, hs] = res[0:tq_in, :]


def _dsa(u, keys, *, n_batch, t_len, n_valid_keys, pos0):
    n_keys = t_len if keys is None else keys[0].shape[1]
    assert n_keys % KEY_TILE == 0
    n_tiles = n_keys // KEY_TILE
    topk = min(TOPK_MAX, n_valid_keys // 4)
    if t_len % 256 == 0:
        tq = tq_in = 256
    elif t_len % LANES == 0:
        tq = tq_in = LANES
    else:
        tq, tq_in = LANES, t_len
    nq = t_len // tq_in
    row = lambda b, j: b * nq + j
    if keys is None:
        key_args = (u, u, u)
        key_specs = [pl.BlockSpec((n_keys, A_KV_WIDTH), lambda b, j: (b, COL_AK // A_KV_WIDTH)),
                     pl.BlockSpec((n_keys, A_KV_WIDTH), lambda b, j: (b, COL_AV // A_KV_WIDTH)),
                     pl.BlockSpec((n_keys, LANES), lambda b, j: (b, COL_IKW // LANES))]
    else:
        key_args = keys
        key_specs = [pl.BlockSpec((None, n_keys, t.shape[-1]), lambda b, j: (b, 0, 0)) for t in keys]
    pad_scratch = []
    if tq_in < tq:
        pad_scratch = [pltpu.VMEM((tq, A_WIDTH), F32), pltpu.VMEM((tq, IDX_HEADS * IDX_DIM), F32),
                       pltpu.VMEM((tq, LANES), F32), pltpu.VMEM((tq, A_WIDTH), F32)]
    kern = functools.partial(_dsa_kernel, tq=tq, tq_in=tq_in, nq=nq, n_tiles=n_tiles,
                             n_valid_keys=n_valid_keys, pos0=pos0, topk=topk)
    return pl.pallas_call(
        kern,
        grid=(n_batch, nq),
        in_specs=[pl.BlockSpec((tq_in, A_WIDTH), lambda b, j: (row(b, j), COL_AQ // A_WIDTH)),
                  pl.BlockSpec((tq_in, A_WIDTH), lambda b, j: (row(b, j), COL_IQ // A_WIDTH)),
                  pl.BlockSpec((tq_in, LANES), lambda b, j: (row(b, j), COL_IKW // LANES)),
                  pl.BlockSpec((tq_in, A_WIDTH), lambda b, j: (row(b, j), COL_AZ // A_WIDTH)),
                  *key_specs],
        out_specs=pl.BlockSpec((tq_in, A_WIDTH), lambda b, j: (row(b, j), 0)),
        out_shape=jax.ShapeDtypeStruct((n_batch * t_len, A_WIDTH), BF16),
        scratch_shapes=[pltpu.VMEM((n_tiles, KEY_TILE, A_KV_WIDTH), BF16),
                        pltpu.VMEM((n_tiles, A_KV_HEADS, A_HEAD_DIM, KEY_TILE), BF16),
                        pltpu.VMEM((n_tiles, KEY_TILE, LANES), BF16),
                        pltpu.VMEM((n_tiles, KEY_TILE, LANES), BF16),
                        pltpu.VMEM((tq, IDX_HEADS * IDX_DIM), BF16),
                        pltpu.VMEM((LANES, tq), F32),
                        pltpu.VMEM((A_KV_HEADS, A_GROUP * tq, A_HEAD_DIM), BF16),
                        pltpu.VMEM((n_tiles, KEY_TILE, tq), F32),
                        pltpu.VMEM((n_tiles, KEY_TILE, tq), F32),
                        pltpu.VMEM((SUBLANES, tq), F32), pltpu.VMEM((SUBLANES, tq), F32),
                        pltpu.VMEM((SUBLANES, tq), jnp.int32),
                        pltpu.VMEM((SUBLANES, tq), jnp.int32),
                        pltpu.VMEM((2, A_KV_HEADS, KEY_TILE, A_GROUP * tq), F32),
                        pltpu.VMEM((A_KV_HEADS, KEY_TILE, A_GROUP * tq), BF16),
                        pltpu.VMEM((A_KV_HEADS, A_HEAD_DIM, A_GROUP * tq), F32),
                        pltpu.VMEM((A_KV_HEADS * SUBLANES, A_GROUP * tq), F32)] + pad_scratch,
        compiler_params=_cparams(2),
        name="dsa",
    )(u, u, u, u, *key_args)


def _expand_heads(v, e):
    hi = v.astype(BF16)
    lo = (v - hi.astype(F32)).astype(BF16)
    return _dot(hi, e) + _dot(lo, e)


def _mamba_kernel(*refs, lc, n_in, has_init):
    if has_init:
        (z_ref, xbc_ref, dt_ref, cw_ref, cb_ref, dtb_ref, alog_ref, dsk_ref, nrm_ref, e_ref,
         cprev_ref, h0_ref, y_ref, cst_ref, hl_ref, xpad_ref, st_ref, yacc_ref, *pad_refs) = refs
    else:
        (z_ref, xbc_ref, dt_ref, cw_ref, cb_ref, dtb_ref, alog_ref, dsk_ref, nrm_ref, e_ref,
         y_ref, cst_ref, hl_ref, xpad_ref, st_ref, yacc_ref, *pad_refs) = refs
    c = pl.program_id(1)
    n_chunks = pl.num_programs(1)

    @pl.when(c == 0)
    def _():
        xpad_ref[...] = jnp.zeros(xpad_ref.shape, F32)
        if has_init:
            xpad_ref[0:SUBLANES, :] = cprev_ref[...]
            for g in range(B_GROUPS):
                st_ref[g] = h0_ref[B_GROUP_W * g:B_GROUP_W * (g + 1), :].T
        else:
            st_ref[...] = jnp.zeros(st_ref.shape, F32)

    xpad_ref[SUBLANES:SUBLANES + n_in, :] = xbc_ref[...]
    acc = cb_ref[...] + cw_ref[0:1, :] * xpad_ref[SUBLANES - 3:SUBLANES - 3 + lc, :]
    for j in range(1, B_CONV):
        acc = acc + cw_ref[j:j + 1, :] * xpad_ref[SUBLANES - 3 + j:SUBLANES - 3 + j + lc, :]
    xbc = _silu(acc)

    @pl.when(c == n_chunks - 1)
    def _():
        cst_ref[...] = xpad_ref[SUBLANES + n_in - 3:SUBLANES + n_in, :]

    xpad_ref[0:SUBLANES, :] = xpad_ref[lc:lc + SUBLANES, :]

    if n_in < lc:
        dtp_ref, zp_ref = pad_refs
        dtp_ref[...] = jnp.zeros(dtp_ref.shape, F32)
        dtp_ref[0:n_in, :] = dt_ref[...]
        zp_ref[...] = jnp.zeros(zp_ref.shape, F32)
        zp_ref[0:n_in, :] = z_ref[...]
        dt_raw = dtp_ref[...]
        z_all = zp_ref
    else:
        dt_raw = dt_ref[...]
        z_all = z_ref
    pre = dt_raw + dtb_ref[...]
    dt = jnp.maximum(pre, 0.0) + jnp.log1p(jnp.exp(-jnp.abs(pre)))
    row = lax.broadcasted_iota(jnp.int32, (lc, LANES), 0)
    dt = jnp.where(lax.broadcasted_iota(jnp.int32, (lc, LANES), 1) < B_HEADS, dt, 0.0)
    if n_in < lc:
        dt = jnp.where(row < n_in, dt, 0.0)
    a = -jnp.exp(alog_ref[...])
    cum = dt * a
    shift = 1
    while shift < lc:
        cum = cum + jnp.where(row >= shift, pltpu.roll(cum, shift, axis=0), 0.0)
        shift *= 2
    cum_t = cum.T
    dt_t = dt.T
    c_last = cum[lc - 1:lc, :]
    e = e_ref[...]
    x1 = _expand_heads(jnp.exp(cum), e)
    x2 = _expand_heads(jnp.exp(c_last - cum) * dt, e)
    x3 = _expand_heads(jnp.broadcast_to(jnp.exp(c_last), (SUBLANES, LANES)), e)[0:1, :]

    xs = xbc[:, 0:B_WIDTH]
    xd = (xs * x2).astype(BF16)
    li = lax.broadcasted_iota(jnp.int32, (lc, lc), 0)
    si = lax.broadcasted_iota(jnp.int32, (lc, lc), 1)
    causal = li >= si
    lane = lax.broadcasted_iota(jnp.int32, (lc, LANES), 1)
    dsk = dsk_ref[...]
    for g in range(B_GROUPS):
        bg = xbc[:, B_WIDTH + B_STATE * g:B_WIDTH + B_STATE * (g + 1)]
        cg = xbc[:, B_WIDTH + B_GROUPS * B_STATE + B_STATE * g:B_WIDTH + B_GROUPS * B_STATE + B_STATE * (g + 1)]
        bgb = bg.astype(BF16)
        cgb = cg.astype(BF16)
        cb = _dot_nt(cgb, bgb)
        gs = slice(B_GROUP_W * g, B_GROUP_W * (g + 1))
        state = st_ref[g]
        y_off = _dot(cgb, state.astype(BF16)) * x1[:, gs]
        for pp in range(B_GROUP_W // LANES):
            col = B_GROUP_W * g + LANES * pp
            xp = xs[:, col:col + LANES]
            y_pair = y_off[:, LANES * pp:LANES * (pp + 1)] + dsk[:, col:col + LANES] * xp
            xpb = xp.astype(BF16)
            y_half = []
            for half in range(2):
                head = col // B_HEAD_DIM + half
                seg = cum[:, head:head + 1] - cum_t[head:head + 1, :]
                wgt = cb * jnp.exp(jnp.where(causal, seg, NEG_BIG)) * dt_t[head:head + 1, :]
                y_half.append(_dot(wgt.astype(BF16), xpb))
            y_pair = y_pair + jnp.where(lane < B_HEAD_DIM, y_half[0], y_half[1])
            yacc_ref[:, col:col + LANES] = y_pair
        st_ref[g] = state * x3[:, gs] + _dot(bg.T.astype(BF16), xd[:, gs])

    for g in range(B_GROUPS):
        gs = slice(B_GROUP_W * g, B_GROUP_W * (g + 1))
        yg = yacc_ref[:, gs] * _silu(z_all[:, gs])
        ms = jnp.mean(yg * yg, axis=-1, keepdims=True)
        out = (yg * lax.rsqrt(ms + EPS) * nrm_ref[:, gs]).astype(y_ref.dtype)
        y_ref[:, gs] = out[0:n_in, :]

    @pl.when(c == n_chunks - 1)
    def _():
        for g in range(B_GROUPS):
            hl_ref[B_GROUP_W * g:B_GROUP_W * (g + 1), :] = st_ref[g].T


def _mamba(u, conv_w, conv_b, dt_bias, a_log, d_skip, ssm_norm, conv_prev, h0, *, n_batch, t_len):
    lc = LANES
    if t_len % lc == 0:
        n_in = lc
    else:
        assert t_len < lc
        n_in = t_len
    nc = t_len // n_in
    has_init = conv_prev is not None
    row = lambda b, c: b * nc + c
    pad1 = lambda v: jnp.concatenate([v.astype(F32), jnp.zeros((LANES - B_HEADS,), F32)]).reshape(1, LANES)
    head_of_col = jnp.arange(B_WIDTH, dtype=jnp.int32) // B_HEAD_DIM
    expand = (jnp.arange(LANES, dtype=jnp.int32)[:, None] == head_of_col[None, :]).astype(BF16)
    dsk_row = jnp.repeat(d_skip.astype(F32), B_HEAD_DIM).reshape(1, B_WIDTH)
    const = lambda shape: pl.BlockSpec(shape, lambda b, c: (0,) * len(shape))
    in_specs = [pl.BlockSpec((n_in, B_WIDTH), lambda b, c: (row(b, c), COL_BZ // B_WIDTH)),
                pl.BlockSpec((n_in, B_CONV_DIM), lambda b, c: (row(b, c), COL_XBC // B_CONV_DIM)),
                pl.BlockSpec((n_in, LANES), lambda b, c: (row(b, c), COL_DT // LANES)),
                const((B_CONV, B_CONV_DIM)), const((1, B_CONV_DIM)), const((1, LANES)), const((1, LANES)),
                const((1, B_WIDTH)), const((1, B_WIDTH)), const((LANES, B_WIDTH))]
    args = [u, u, u, conv_w, conv_b.reshape(1, B_CONV_DIM), pad1(dt_bias), pad1(a_log), dsk_row,
            ssm_norm.reshape(1, B_WIDTH), expand]
    if has_init:
        cprev8 = jnp.concatenate([jnp.zeros((n_batch, SUBLANES - (B_CONV - 1), B_CONV_DIM), F32), conv_prev], axis=1)
        in_specs += [pl.BlockSpec((None, SUBLANES, B_CONV_DIM), lambda b, c: (b, 0, 0)),
                     pl.BlockSpec((None, B_WIDTH, B_STATE), lambda b, c: (b, 0, 0))]
        args += [cprev8, h0.reshape(n_batch, B_WIDTH, B_STATE)]
    pad_scratch = []
    if n_in < lc:
        pad_scratch = [pltpu.VMEM((lc, LANES), F32), pltpu.VMEM((lc, B_WIDTH), F32)]
    kern = functools.partial(_mamba_kernel, lc=lc, n_in=n_in, has_init=has_init)
    y, cst, hl = pl.pallas_call(
        kern,
        grid=(n_batch, nc),
        in_specs=in_specs,
        out_specs=[pl.BlockSpec((n_in, B_WIDTH), lambda b, c: (row(b, c), 0)),
                   pl.BlockSpec((None, B_CONV - 1, B_CONV_DIM), lambda b, c: (b, 0, 0)),
                   pl.BlockSpec((None, B_WIDTH, B_STATE), lambda b, c: (b, 0, 0))],
        out_shape=[jax.ShapeDtypeStruct((n_batch * t_len, B_WIDTH), BF16),
                   jax.ShapeDtypeStruct((n_batch, B_CONV - 1, B_CONV_DIM), F32),
                   jax.ShapeDtypeStruct((n_batch, B_WIDTH, B_STATE), F32)],
        scratch_shapes=[pltpu.VMEM((SUBLANES + lc, B_CONV_DIM), F32),
                        pltpu.VMEM((B_GROUPS, B_STATE, B_GROUP_W), F32),
                        pltpu.VMEM((lc, B_WIDTH), F32)] + pad_scratch,
        compiler_params=_cparams(2),
        name="mamba",
    )(*args)
    return y, cst, hl.reshape(n_batch, B_HEADS, B_HEAD_DIM, B_STATE)


def _mem_kernel(q_ref, z_ref, k_ref, v_ref, o_ref):
    q = q_ref[...].astype(BF16)
    scale = M_HEAD_DIM ** -0.5
    for head in range(M_HEADS):
        hs = slice(M_HEAD_DIM * head, M_HEAD_DIM * (head + 1))
        s = _dot_nt(q[:, hs], k_ref[:, hs].astype(BF16)) * scale
        s_max = jnp.max(s, axis=-1, keepdims=True)
        p = jnp.exp(s - s_max)
        denom = jnp.sum(p, axis=-1, keepdims=True)
        o = _dot(p.astype(BF16), v_ref[:, hs].astype(BF16)) / denom
        o_ref[:, hs] = (o * _silu(z_ref[:, hs])).astype(o_ref.dtype)


def _mem_attend(u, mk, mv, *, n_batch, t_len):
    tq = 512 if t_len % 512 == 0 else t_len
    nq = t_len // tq
    row = lambda b, j: b * nq + j
    return pl.pallas_call(
        _mem_kernel,
        grid=(n_batch, nq),
        in_specs=[pl.BlockSpec((tq, M_WIDTH), lambda b, j: (row(b, j), COL_MQ // M_WIDTH)),
                  pl.BlockSpec((tq, M_WIDTH), lambda b, j: (row(b, j), COL_MZ // M_WIDTH)),
                  pl.BlockSpec((None, N_MEM, M_WIDTH), lambda b, j: (b, 0, 0)),
                  pl.BlockSpec((None, N_MEM, M_WIDTH), lambda b, j: (b, 0, 0))],
        out_specs=pl.BlockSpec((tq, M_WIDTH), lambda b, j: (row(b, j), 0)),
        out_shape=jax.ShapeDtypeStruct((n_batch * t_len, M_WIDTH), BF16),
        compiler_params=_cparams(2),
        name="mem_attend",
    )(u, u, mk, mv)


def _merge_kernel(ya_ref, yb_ref, ym_ref, ga_ref, gb_ref, gm_ref, wa_ref, wb_ref, wm_ref, o_ref):
    merged = _sigmoid(ga_ref[...]) * _dot(ya_ref[...], wa_ref[...])
    merged = merged + _sigmoid(gb_ref[...]) * _dot(yb_ref[...], wb_ref[...])
    merged = merged + _sigmoid(gm_ref[...]) * _dot(ym_ref[...], wm_ref[...])
    o_ref[...] = merged.astype(o_ref.dtype)


def _merge(ya, yb, ym, u, w_pa, w_pb, w_pm, tm):
    n = ya.shape[0]
    rows = lambda width: pl.BlockSpec((tm, width), lambda i: (i, 0))
    gate = lambda k: pl.BlockSpec((tm, D_MODEL), lambda i: (i, COL_GATES // D_MODEL + k))
    weight = lambda width: pl.BlockSpec((width, D_MODEL), lambda i: (0, 0), pipeline_mode=pl.Buffered(1))
    return pl.pallas_call(
        _merge_kernel,
        grid=(n // tm,),
        in_specs=[rows(A_WIDTH), rows(B_WIDTH), rows(M_WIDTH), gate(0), gate(1), gate(2),
                  weight(A_WIDTH), weight(B_WIDTH), weight(M_WIDTH)],
        out_specs=rows(D_MODEL),
        out_shape=jax.ShapeDtypeStruct((n, D_MODEL), BF16),
        compiler_params=_cparams(1),
        name="merge",
    )(ya, yb, ym, u, u, u, w_pa, w_pb, w_pm)


def _final_kernel(m_ref, x_ref, wo_ref, g_ref, o_ref):
    y = x_ref[...] + _dot(m_ref[...], wo_ref[...])
    ms = jnp.mean(y * y, axis=-1, keepdims=True)
    o_ref[...] = y * lax.rsqrt(ms + EPS) * g_ref[...]


def _final(merged, x, w_o, g, tm):
    n = x.shape[0]
    rows = pl.BlockSpec((tm, D_MODEL), lambda i: (i, 0))
    return pl.pallas_call(
        _final_kernel,
        grid=(n // tm,),
        in_specs=[rows, rows,
                  pl.BlockSpec((D_MODEL, D_MODEL), lambda i: (0, 0), pipeline_mode=pl.Buffered(1)),
                  pl.BlockSpec((1, D_MODEL), lambda i: (0, 0))],
        out_specs=rows,
        out_shape=jax.ShapeDtypeStruct((n, D_MODEL), F32),
        compiler_params=_cparams(1),
        name="final",
    )(merged, x, w_o, g.reshape(1, D_MODEL))


def _row_tile(n, pref):
    t = pref
    while n % t:
        t //= 2
    return t


def _pad_keys(t, n_keys):
    pad = n_keys - t.shape[1]
    if pad == 0:
        return t
    return jnp.concatenate([t, jnp.zeros((t.shape[0], pad, t.shape[2]), t.dtype)], axis=1)


def _layer(x, pos0, past_k, past_v, past_ki, conv_prev, h0, mem_k, mem_v, lw, norm_final):
    (norm_in, w_in_packed, conv_w, conv_b, dt_bias, a_log, d_skip, ssm_norm, w_pa, w_pb, w_pm, w_o) = lw
    n_batch, t_len, _ = x.shape
    n = n_batch * t_len
    x2 = x.reshape(n, D_MODEL)
    u = _rms_proj(x2, norm_in, w_in_packed, _row_tile(n, 1024))

    k_new = u[:, COL_AK:COL_AK + A_KV_WIDTH].reshape(n_batch, t_len, A_KV_WIDTH)
    v_new = u[:, COL_AV:COL_AV + A_KV_WIDTH].reshape(n_batch, t_len, A_KV_WIDTH)
    kikw_new = u[:, COL_IKW:COL_IKW + LANES].reshape(n_batch, t_len, LANES)
    if past_k is None:
        keys, n_valid_keys = None, t_len
    else:
        n_past = past_k.shape[1]
        past_kikw = jnp.concatenate([past_ki, jnp.zeros((n_batch, n_past, LANES - IDX_DIM), F32)], axis=2)
        n_valid_keys = n_past + t_len
        n_keys = -(-n_valid_keys // KEY_TILE) * KEY_TILE
        keys = (_pad_keys(jnp.concatenate([past_k.reshape(n_batch, n_past, A_KV_WIDTH), k_new], axis=1), n_keys),
                _pad_keys(jnp.concatenate([past_v.reshape(n_batch, n_past, A_KV_WIDTH), v_new], axis=1), n_keys),
                _pad_keys(jnp.concatenate([past_kikw, kikw_new], axis=1), n_keys))
    ya = _dsa(u, keys, n_batch=n_batch, t_len=t_len, n_valid_keys=n_valid_keys, pos0=pos0)

    yb, conv_state, h_last = _mamba(u, conv_w, conv_b, dt_bias, a_log, d_skip, ssm_norm, conv_prev, h0,
                                    n_batch=n_batch, t_len=t_len)
    ym = _mem_attend(u, mem_k, mem_v, n_batch=n_batch, t_len=t_len)
    merged = _merge(ya, yb, ym, u, w_pa, w_pb, w_pm, _row_tile(n, 256))
    y = _final(merged, x2, w_o, norm_final, _row_tile(n, 512)).reshape(n_batch, t_len, D_MODEL)
    return (y, k_new.reshape(n_batch, t_len, A_KV_HEADS, A_HEAD_DIM),
            v_new.reshape(n_batch, t_len, A_KV_HEADS, A_HEAD_DIM),
            kikw_new[:, :, 0:IDX_DIM], conv_state, h_last)


def kernel(x_prompt, x_sample, mem_prompt, cache_attn_k, cache_attn_v, cache_idx_k, state_conv, state_ssm,
           cache_mem_k, cache_mem_v, norm_in, w_in, conv_w, conv_b, dt_bias, a_log, d_skip, ssm_norm,
           norm_mem, w_mem_kv, w_pa, w_pb, w_pm, w_o, norm_final):
    depth = w_in.shape[0]
    assert depth == 1, "the final RMSNorm is fused into the (single) layer"
    bp = x_prompt.shape[0]
    bs = x_sample.shape[0]
    first = lambda t: t.reshape(t.shape[1:])
    lw = (first(norm_in), _pack_w_in(w_in), first(conv_w), first(conv_b), first(dt_bias), first(a_log),
          first(d_skip), first(ssm_norm), first(w_pa).astype(BF16), first(w_pb).astype(BF16),
          first(w_pm).astype(BF16), first(w_o).astype(BF16))

    mem2 = mem_prompt.reshape(bp * N_MEM, D_MODEL)
    w_mkv = first(w_mem_kv).astype(BF16).reshape(D_MODEL, 2 * M_WIDTH // PROJ_COL_TILE, PROJ_COL_TILE)
    mkv = _rms_proj(mem2, first(norm_mem), jnp.transpose(w_mkv, (1, 0, 2)), _row_tile(bp * N_MEM, 1024))
    mk_p = mkv[:, 0:M_WIDTH].reshape(bp, N_MEM, M_WIDTH)
    mv_p = mkv[:, M_WIDTH:2 * M_WIDTH].reshape(bp, N_MEM, M_WIDTH)

    yp, kp, vp, kip, convp, ssmp = _layer(x_prompt, 0, None, None, None, None, None, mk_p, mv_p, lw, norm_final)
    ys, ks, vs, kis, convs, ssms = _layer(
        x_sample, PAST_LEN, first(cache_attn_k), first(cache_attn_v), first(cache_idx_k), first(state_conv),
        first(state_ssm), first(cache_mem_k).reshape(bs, N_MEM, M_WIDTH),
        first(cache_mem_v).reshape(bs, N_MEM, M_WIDTH), lw, norm_final)

    st = lambda t: t[None]
    return (yp, ys, st(kp), st(vp), st(kip), st(convp), st(ssmp),
            st(mk_p.reshape(bp, N_MEM, M_HEADS, M_HEAD_DIM)), st(mv_p.reshape(bp, N_MEM, M_HEADS, M_HEAD_DIM)),
            st(ks), st(vs), st(kis), st(convs), st(ssms))
```

```python
import functools

import jax
import jax.numpy as jnp
from jax import lax
from jax.experimental import pallas as pl
from jax.experimental.pallas import tpu as pltpu

F32 = jnp.float32
BF16 = jnp.bfloat16

D_MODEL = 2048
CHUNK = 64
CHUNK_SHIFT = 6
assert 1 << CHUNK_SHIFT == CHUNK
N_MEM = 256
EPS = 1e-6
PAST_LEN = 1024

A_HEADS = 8
A_KV_HEADS = 2
A_HEAD_DIM = 128
A_GROUP = A_HEADS // A_KV_HEADS
A_WIDTH = A_HEADS * A_HEAD_DIM
A_KV_WIDTH = A_KV_HEADS * A_HEAD_DIM
IDX_HEADS = 16
IDX_DIM = 64
TOPK_MAX = 256
IDX_SCALE = (IDX_DIM * IDX_HEADS) ** -0.5

B_WIDTH = 2048
B_HEAD_DIM = 64
B_HEADS = B_WIDTH // B_HEAD_DIM
B_GROUPS = 4
B_HPG = B_HEADS // B_GROUPS
B_STATE = 128
B_CONV = 4
B_CONV_DIM = B_WIDTH + 2 * B_GROUPS * B_STATE
B_GROUP_W = B_WIDTH // B_GROUPS

M_HEADS = 4
M_HEAD_DIM = 256
M_WIDTH = M_HEADS * M_HEAD_DIM

N_BRANCH = 3
IN_SPLITS = (A_WIDTH, A_KV_WIDTH, A_KV_WIDTH, IDX_HEADS * IDX_DIM, IDX_DIM, IDX_HEADS, A_WIDTH,
             B_WIDTH, B_CONV_DIM, B_HEADS, M_WIDTH, M_WIDTH, N_BRANCH * D_MODEL)
IN_COLS = sum(IN_SPLITS)
(SRC_AQ, SRC_AK, SRC_AV, SRC_IQ, SRC_IK, SRC_IW, SRC_AZ, SRC_BZ, SRC_XBC, SRC_DT, SRC_MQ, SRC_MZ,
 SRC_GATES) = (sum(IN_SPLITS[:i]) for i in range(len(IN_SPLITS)))

LANES = 128
SUBLANES = 8
VMEM_LIMIT_BYTES = 56 * 1024 * 1024

COL_GATES = 0
COL_XBC = 6144
COL_AQ = 9216
COL_IQ = 10240
COL_AZ = 11264
COL_MQ = 12288
COL_MZ = 13312
COL_BZ = 14336
COL_AK = 16384
COL_AV = 16640
COL_IKW = 16896
COL_DT = 17152
PACKED_COLS = 17408

PACK_CHUNK = 256
PACK_CHUNKS_PER_STEP = 4
PACK_ROW_ALIGN = 2 * SUBLANES
PACK_SEGMENTS = ((SRC_GATES, COL_GATES, N_BRANCH * D_MODEL), (SRC_XBC, COL_XBC, B_CONV_DIM),
                 (SRC_AQ, COL_AQ, A_WIDTH), (SRC_IQ, COL_IQ, IDX_HEADS * IDX_DIM), (SRC_AZ, COL_AZ, A_WIDTH),
                 (SRC_MQ, COL_MQ, M_WIDTH), (SRC_MZ, COL_MZ, M_WIDTH), (SRC_BZ, COL_BZ, B_WIDTH),
                 (SRC_AK, COL_AK, A_KV_WIDTH), (SRC_AV, COL_AV, A_KV_WIDTH),
                 (SRC_IK, COL_IKW, PACK_CHUNK), (SRC_DT, COL_DT, PACK_CHUNK))


def _pack_source_table():
    table = [None] * (PACKED_COLS // PACK_CHUNK)
    for src, dst, width in PACK_SEGMENTS:
        assert dst % PACK_CHUNK == 0 and width % PACK_CHUNK == 0 and src % (2 * SUBLANES) == 0
        for off in range(0, width, PACK_CHUNK):
            assert src + off + PACK_CHUNK <= IN_COLS
            table[(dst + off) // PACK_CHUNK] = src + off
    assert all(t is not None for t in table)
    return table


PACK_SOURCE = _pack_source_table()

KEY_TILE = 256
PROJ_SUB_ROWS = 512
FINAL_SUB_ROWS = 256
LOG2_E = 1.4426950408889634
INT_MIN = -2 ** 31
KEY_NEG_INF = INT_MIN + 0x7FFFFF
NEG_BIG = -1e30


def _cparams(n_grid):
    return pltpu.CompilerParams(dimension_semantics=("arbitrary",) * n_grid,
                                vmem_limit_bytes=VMEM_LIMIT_BYTES)


def _sigmoid(z):
    return 0.5 * jnp.tanh(0.5 * z) + 0.5


def _silu(z):
    half = 0.5 * z
    return half * jnp.tanh(half) + half


def _dot(a, b):
    return jnp.dot(a, b, preferred_element_type=F32)


def _dot_nt(a, b):
    return lax.dot_general(a, b, (((1,), (1,)), ((), ())), preferred_element_type=F32)


def _pack_kernel(src_ref, *refs):
    del src_ref
    *wt_refs, o_ref = refs
    for k, wt_ref in enumerate(wt_refs):
        o_ref[:, PACK_CHUNK * k:PACK_CHUNK * (k + 1)] = wt_ref[...].T.astype(o_ref.dtype)


def _pack_w_in(w):
    wt = jnp.transpose(w.reshape(D_MODEL, IN_COLS))
    per_step = PACK_CHUNKS_PER_STEP
    window = lambda k: pl.BlockSpec((pl.Element(PACK_CHUNK), pl.Element(D_MODEL)),
                                    lambda i, src: (src[per_step * i + k] * PACK_ROW_ALIGN, 0))
    grid_spec = pltpu.PrefetchScalarGridSpec(
        num_scalar_prefetch=1,
        grid=(PACKED_COLS // (PACK_CHUNK * per_step),),
        in_specs=[window(k) for k in range(per_step)],
        out_specs=pl.BlockSpec((D_MODEL, PACK_CHUNK * per_step), lambda i, src: (0, i)))
    return pl.pallas_call(
        _pack_kernel,
        grid_spec=grid_spec,
        out_shape=jax.ShapeDtypeStruct((D_MODEL, PACKED_COLS), BF16),
        compiler_params=_cparams(1),
        name="pack_w_in",
    )(jnp.asarray([s // PACK_ROW_ALIGN for s in PACK_SOURCE], jnp.int32), *([wt] * per_step))


def _proj_kernel(x_ref, g_ref, w_ref, o_ref, h_ref):
    @pl.when(pl.program_id(1) == 0)
    def _():
        x = x_ref[...]
        ms = jnp.mean(x * x, axis=-1, keepdims=True)
        h_ref[...] = (x * lax.rsqrt(ms + EPS) * g_ref[...]).astype(BF16)

    tm = o_ref.shape[0]
    sub = min(tm, PROJ_SUB_ROWS)
    for r in range(tm // sub):
        o_ref[r * sub:(r + 1) * sub, :] = _dot(h_ref[r * sub:(r + 1) * sub, :], w_ref[...])


def _rms_proj(x, g, w, tm, tn):
    m, d = x.shape
    n = w.shape[1]
    return pl.pallas_call(
        _proj_kernel,
        grid=(m // tm, n // tn),
        in_specs=[pl.BlockSpec((tm, d), lambda i, j: (i, 0)),
                  pl.BlockSpec((1, d), lambda i, j: (0, 0)),
                  pl.BlockSpec((d, tn), lambda i, j: (0, j))],
        out_specs=pl.BlockSpec((tm, tn), lambda i, j: (i, j)),
        out_shape=jax.ShapeDtypeStruct((m, n), F32),
        scratch_shapes=[pltpu.VMEM((tm, d), BF16)],
        compiler_params=_cparams(2),
        name="rms_proj",
    )(x, g.reshape(1, d), w)


def _key_to_f32(key):
    bits = jnp.where(key >= 0, key, key ^ jnp.int32(0x7FFFFFFF))
    return pltpu.bitcast(bits, F32)


def _dsa_kernel(q_ref, iq_ref, ikw_ref, az_ref, k_ref, v_ref, kikw_ref, o_ref,
                kb_ref, vb_ref, kie_ref, kio_ref, iqb_ref, wt_ref, qs_ref, sc_ref, bias_ref,
                cnt_ref, gt_ref, m_ref, ans_ref, s_ref, p_ref, acc_ref, stat_ref, *pad_refs,
                tq, tq_in, nq, n_tiles, n_valid_keys, pos0, topk):
    jq = pl.program_id(1)
    kt_sz = KEY_TILE
    topk_f = float(topk)

    @pl.when(jq == 0)
    def _():
        for kt in range(n_tiles):
            tile = slice(kt_sz * kt, kt_sz * (kt + 1))
            kb_ref[kt] = k_ref[tile, :].astype(BF16)
            v_t = v_ref[tile, :].T.astype(BF16)
            for g in range(A_KV_HEADS):
                vb_ref[kt, g] = v_t[A_HEAD_DIM * g:A_HEAD_DIM * (g + 1), :]
            ki = kikw_ref[tile, :]
            lane = lax.broadcasted_iota(jnp.int32, ki.shape, 1)
            kie = jnp.where(lane < IDX_DIM, ki, 0.0)
            kie_ref[kt] = kie.astype(BF16)
            kio_ref[kt] = pltpu.roll(kie, IDX_DIM, axis=1).astype(BF16)

    if tq_in < tq:
        qp_ref, iqp_ref, ikwp_ref, azp_ref = pad_refs
        for dst, src in ((qp_ref, q_ref), (iqp_ref, iq_ref), (ikwp_ref, ikw_ref), (azp_ref, az_ref)):
            dst[...] = jnp.zeros(dst.shape, dst.dtype)
            dst[0:tq_in, :] = src[...]
        q_ref, iq_ref, ikw_ref, az_ref = qp_ref, iqp_ref, ikwp_ref, azp_ref

    iqb_ref[...] = iq_ref[...].astype(BF16)
    wt_ref[...] = ikw_ref[...].T * IDX_SCALE
    q = (q_ref[...] * ((A_HEAD_DIM ** -0.5) * LOG2_E)).astype(BF16)
    for head in range(A_HEADS):
        g, hh = divmod(head, A_GROUP)
        qs_ref[g, hh * tq:(hh + 1) * tq, :] = q[:, A_HEAD_DIM * head:A_HEAD_DIM * (head + 1)]

    q_last = pos0 + (jq + 1) * tq_in - 1
    key_end = jnp.minimum((q_last // CHUNK + 1) * CHUNK, n_valid_keys)
    n_need = (key_end + kt_sz - 1) // kt_sz

    def for_needed_tiles(body):
        for kt in range(n_tiles):
            pl.when(kt < n_need)(functools.partial(body, kt))

    krow = lax.broadcasted_iota(jnp.int32, (kt_sz, tq), 0)
    qcol = lax.broadcasted_iota(jnp.int32, (kt_sz, tq), 1)
    qchunk = jnp.right_shift(pos0 + jq * tq_in + qcol, CHUNK_SHIFT)

    def score_tile(kt):
        acc = None
        for pair in range(IDX_HEADS // 2):
            iq_pair = iqb_ref[:, LANES * pair:LANES * (pair + 1)]
            for half, kref in enumerate((kie_ref, kio_ref)):
                head = 2 * pair + half
                d = _dot_nt(kref[kt], iq_pair)
                contrib = jnp.maximum(d, 0.0) * wt_ref[IDX_DIM + head:IDX_DIM + head + 1, :]
                acc = contrib if acc is None else acc + contrib
        kpos = kt * kt_sz + krow
        if (kt + 1) * kt_sz > n_valid_keys:
            acc = jnp.where(kpos < n_valid_keys, acc, -jnp.inf)
        sc_ref[kt] = jnp.where(jnp.right_shift(kpos, CHUNK_SHIFT) <= qchunk, acc, -jnp.inf)

    for_needed_tiles(score_tile)

    def count_into(ref, kt, hit):
        ref[...] += jnp.sum(hit.reshape(kt_sz // SUBLANES, SUBLANES, tq), axis=0)

    def total(ref):
        return jnp.sum(ref[...], axis=0, keepdims=True)

    def run_bisection(n_static):
        def bisect(i, carry):
            ans, cnt_ans = carry
            cand = ans + jnp.left_shift(jnp.int32(1), jnp.int32(31) - i)
            cand_f = _key_to_f32(jnp.maximum(cand, jnp.int32(KEY_NEG_INF)))
            part = jnp.zeros((SUBLANES, tq), F32)
            for kt in range(n_static):
                hit = jnp.where(sc_ref[kt] >= cand_f, 1.0, 0.0)
                part = part + jnp.sum(hit.reshape(kt_sz // SUBLANES, SUBLANES, tq), axis=0)
            cnt = jnp.sum(part, axis=0, keepdims=True)
            keep = cnt >= topk_f
            return jnp.where(keep, cand, ans), jnp.where(keep, cnt, cnt_ans)

        init = (jnp.full((1, tq), INT_MIN, jnp.int32), jnp.zeros((1, tq), F32))
        ans, cnt_ans = lax.fori_loop(0, 32, bisect, init)
        ans_ref[...] = jnp.broadcast_to(ans, ans_ref.shape)
        cnt_ref[...] = jnp.broadcast_to(cnt_ans, cnt_ref.shape)

    need_of = []
    for j in range(nq):
        end = min(((pos0 + (j + 1) * tq_in - 1) // CHUNK + 1) * CHUNK, n_valid_keys)
        need_of.append(-(-end // kt_sz))
    for n_static in sorted(set(need_of)):
        blocks = [j for j in range(nq) if need_of[j] == n_static]
        if len(blocks) == nq:
            run_bisection(n_static)
        else:
            in_range = jnp.logical_and(jq >= blocks[0], jq <= blocks[-1])
            pl.when(in_range)(functools.partial(run_bisection, n_static))
    ans = ans_ref[0:1, :]
    thr = _key_to_f32(jnp.maximum(ans, jnp.int32(KEY_NEG_INF)))

    cnt_ge = cnt_ref[0:1, :]
    qcol1 = lax.broadcasted_iota(jnp.int32, (1, tq), 1)
    tie = jnp.where(cnt_ge > topk_f, jnp.where(thr > -jnp.inf, jnp.where(qcol1 < tq_in, 1.0, 0.0), 0.0), 0.0)
    m_ref[...] = jnp.full(m_ref.shape, n_tiles * kt_sz, jnp.int32)

    @pl.when(jnp.max(tie) > 0.0)
    def _():
        nbits = max(1, (n_tiles * kt_sz - 1).bit_length())
        gt_ref[...] = jnp.zeros(gt_ref.shape, F32)
        for_needed_tiles(lambda kt: count_into(gt_ref, kt, jnp.where(sc_ref[kt] > thr, 1.0, 0.0)))
        cnt_gt = total(gt_ref)

        def bisect_pos(i, t):
            cand = t + jnp.left_shift(jnp.int32(1), jnp.int32(nbits - 1) - i)
            cnt_ref[...] = jnp.zeros(cnt_ref.shape, F32)

            def count_below(kt):
                hit = jnp.where(sc_ref[kt] == thr, jnp.where(kt * kt_sz + krow < cand, 1.0, 0.0), 0.0)
                count_into(cnt_ref, kt, hit)

            for_needed_tiles(count_below)
            return jnp.where(cnt_gt + total(cnt_ref) < topk_f, cand, t)

        t_last = lax.fori_loop(0, nbits, bisect_pos, jnp.zeros((1, tq), jnp.int32))
        m_ref[...] = jnp.broadcast_to(t_last, m_ref.shape)

    m_last = m_ref[0:1, :]
    thr_eq = jnp.where(thr > -jnp.inf, thr, jnp.inf)

    def bias_tile(kt):
        sc = sc_ref[kt]
        kpos = kt * kt_sz + krow
        tied = jnp.where(sc == thr_eq, jnp.where(kpos <= m_last, 0.0, NEG_BIG), NEG_BIG)
        bias_ref[kt] = jnp.where(sc > thr, 0.0, tied)

    for_needed_tiles(bias_tile)

    cols = A_GROUP * tq
    groups = range(A_KV_HEADS)
    gsl = [slice(A_HEAD_DIM * g, A_HEAD_DIM * (g + 1)) for g in groups]
    ROW_MAX, ROW_SUM, ROW_ALPHA, ROW_TILE_MAX = 0, 1, 2, 3

    def stat(g, k):
        return stat_ref[SUBLANES * g + k:SUBLANES * g + k + 1, :]

    def set_stat(g, k, v):
        stat_ref[SUBLANES * g + k:SUBLANES * g + k + 1, :] = v

    def produce(kt, slot):
        bias = jnp.concatenate([bias_ref[kt]] * A_GROUP, axis=1)
        for g in groups:
            s = _dot_nt(kb_ref[kt, :, gsl[g]], qs_ref[g]) + bias
            s_ref[slot, g] = s
            set_stat(g, ROW_TILE_MAX + slot, jnp.max(s, axis=0, keepdims=True))

    def consume(kt, slot):
        k_prev = jnp.maximum(kt - 1, 0)
        for g in groups:
            pv = _dot(vb_ref[k_prev, g], p_ref[g])
            acc_ref[g] = acc_ref[g] * stat(g, ROW_ALPHA) + pv
            m_run = stat(g, ROW_MAX)
            m = jnp.maximum(m_run, stat(g, ROW_TILE_MAX + slot))
            alpha = jnp.exp2(m_run - m)
            p = jnp.exp2(s_ref[slot, g] - m)
            set_stat(g, ROW_SUM, alpha * stat(g, ROW_SUM) + jnp.sum(p, axis=0, keepdims=True))
            p_ref[g] = p.astype(BF16)
            set_stat(g, ROW_MAX, m)
            set_stat(g, ROW_ALPHA, alpha)

    acc_ref[...] = jnp.zeros(acc_ref.shape, F32)
    p_ref[...] = jnp.zeros(p_ref.shape, BF16)
    for g in groups:
        set_stat(g, ROW_MAX, jnp.full((1, cols), NEG_BIG, F32))
        set_stat(g, ROW_ALPHA, jnp.ones((1, cols), F32))
        set_stat(g, ROW_SUM, jnp.zeros((1, cols), F32))
    produce(0, 0)

    def tile_pair(i, carry):
        k0 = 2 * i
        produce(jnp.minimum(k0 + 1, n_need - 1), 1)
        consume(k0, 0)

        @pl.when(k0 + 1 < n_need)
        def _():
            produce(jnp.minimum(k0 + 2, n_need - 1), 0)
            consume(k0 + 1, 1)

        return carry

    lax.fori_loop(0, (n_need + 1) // 2, tile_pair, 0)
    for g in groups:
        acc = acc_ref[g] * stat(g, ROW_ALPHA) + _dot(vb_ref[n_need - 1, g], p_ref[g])
        o = (acc / stat(g, ROW_SUM)).T
        for hh in range(A_GROUP):
            hs = slice(A_HEAD_DIM * (A_GROUP * g + hh), A_HEAD_DIM * (A_GROUP * g + hh + 1))
            res = (o[hh * tq:(hh + 1) * tq, :] * _silu(az_ref[:, hs])).astype(o_ref.dtype)
            o_ref[:, hs] = res[0:tq_in, :]


def _dsa(u, keys, *, n_batch, t_len, n_valid_keys, pos0):
    n_keys = t_len if keys is None else keys[0].shape[1]
    assert n_keys % KEY_TILE == 0
    n_tiles = n_keys // KEY_TILE
    topk = min(TOPK_MAX, n_valid_keys // 4)
    if t_len % 256 == 0:
        tq = tq_in = 256
    elif t_len % LANES == 0:
        tq = tq_in = LANES
    else:
        tq, tq_in = LANES, t_len
    nq = t_len // tq_in
    row = lambda b, j: b * nq + j
    if keys is None:
        key_args = (u, u, u)
        key_specs = [pl.BlockSpec((n_keys, A_KV_WIDTH), lambda b, j: (b, COL_AK // A_KV_WIDTH)),
                     pl.BlockSpec((n_keys, A_KV_WIDTH), lambda b, j: (b, COL_AV // A_KV_WIDTH)),
                     pl.BlockSpec((n_keys, LANES), lambda b, j: (b, COL_IKW // LANES))]
    else:
        key_args = keys
        key_specs = [pl.BlockSpec((None, n_keys, t.shape[-1]), lambda b, j: (b, 0, 0)) for t in keys]
    pad_scratch = []
    if tq_in < tq:
        pad_scratch = [pltpu.VMEM((tq, A_WIDTH), F32), pltpu.VMEM((tq, IDX_HEADS * IDX_DIM), F32),
                       pltpu.VMEM((tq, LANES), F32), pltpu.VMEM((tq, A_WIDTH), F32)]
    kern = functools.partial(_dsa_kernel, tq=tq, tq_in=tq_in, nq=nq, n_tiles=n_tiles,
                             n_valid_keys=n_valid_keys, pos0=pos0, topk=topk)
    return pl.pallas_call(
        kern,
        grid=(n_batch, nq),
        in_specs=[pl.BlockSpec((tq_in, A_WIDTH), lambda b, j: (row(b, j), COL_AQ // A_WIDTH)),
                  pl.BlockSpec((tq_in, A_WIDTH), lambda b, j: (row(b, j), COL_IQ // A_WIDTH)),
                  pl.BlockSpec((tq_in, LANES), lambda b, j: (row(b, j), COL_IKW // LANES)),
                  pl.BlockSpec((tq_in, A_WIDTH), lambda b, j: (row(b, j), COL_AZ // A_WIDTH)),
                  *key_specs],
        out_specs=pl.BlockSpec((tq_in, A_WIDTH), lambda b, j: (row(b, j), 0)),
        out_shape=jax.ShapeDtypeStruct((n_batch * t_len, A_WIDTH), BF16),
        scratch_shapes=[pltpu.VMEM((n_tiles, KEY_TILE, A_KV_WIDTH), BF16),
                        pltpu.VMEM((n_tiles, A_KV_HEADS, A_HEAD_DIM, KEY_TILE), BF16),
                        pltpu.VMEM((n_tiles, KEY_TILE, LANES), BF16),
                        pltpu.VMEM((n_tiles, KEY_TILE, LANES), BF16),
                        pltpu.VMEM((tq, IDX_HEADS * IDX_DIM), BF16),
                        pltpu.VMEM((LANES, tq), F32),
                        pltpu.VMEM((A_KV_HEADS, A_GROUP * tq, A_HEAD_DIM), BF16),
                        pltpu.VMEM((n_tiles, KEY_TILE, tq), F32),
                        pltpu.VMEM((n_tiles, KEY_TILE, tq), F32),
                        pltpu.VMEM((SUBLANES, tq), F32), pltpu.VMEM((SUBLANES, tq), F32),
                        pltpu.VMEM((SUBLANES, tq), jnp.int32),
                        pltpu.VMEM((SUBLANES, tq), jnp.int32),
                        pltpu.VMEM((2, A_KV_HEADS, KEY_TILE, A_GROUP * tq), F32),
                        pltpu.VMEM((A_KV_HEADS, KEY_TILE, A_GROUP * tq), BF16),
                        pltpu.VMEM((A_KV_HEADS, A_HEAD_DIM, A_GROUP * tq), F32),
                        pltpu.VMEM((A_KV_HEADS * SUBLANES, A_GROUP * tq), F32)] + pad_scratch,
        compiler_params=_cparams(2),
        name="dsa",
    )(u, u, u, u, *key_args)


def _expand_heads(v, e):
    hi = v.astype(BF16)
    lo = (v - hi.astype(F32)).astype(BF16)
    return _dot(hi, e) + _dot(lo, e)


def _mamba_kernel(*refs, lc, n_in, has_init):
    if has_init:
        (z_ref, xbc_ref, dt_ref, cw_ref, cb_ref, dtb_ref, alog_ref, dsk_ref, nrm_ref, e_ref,
         cprev_ref, h0_ref, y_ref, cst_ref, hl_ref, xpad_ref, st_ref, yacc_ref, xbc_ref2, x1_ref, xd_ref,
         *pad_refs) = refs
    else:
        (z_ref, xbc_ref, dt_ref, cw_ref, cb_ref, dtb_ref, alog_ref, dsk_ref, nrm_ref, e_ref,
         y_ref, cst_ref, hl_ref, xpad_ref, st_ref, yacc_ref, xbc_ref2, x1_ref, xd_ref, *pad_refs) = refs
    c = pl.program_id(1)
    n_chunks = pl.num_programs(1)

    @pl.when(c == 0)
    def _():
        xpad_ref[...] = jnp.zeros(xpad_ref.shape, F32)
        if has_init:
            xpad_ref[0:SUBLANES, :] = cprev_ref[...]
            for g in range(B_GROUPS):
                st_ref[g] = h0_ref[B_GROUP_W * g:B_GROUP_W * (g + 1), :].T
        else:
            st_ref[...] = jnp.zeros(st_ref.shape, F32)

    xpad_ref[SUBLANES:SUBLANES + n_in, :] = xbc_ref[...]
    acc = cb_ref[...] + cw_ref[0:1, :] * xpad_ref[SUBLANES - 3:SUBLANES - 3 + lc, :]
    for j in range(1, B_CONV):
        acc = acc + cw_ref[j:j + 1, :] * xpad_ref[SUBLANES - 3 + j:SUBLANES - 3 + j + lc, :]
    xbc_ref2[...] = _silu(acc)

    @pl.when(c == n_chunks - 1)
    def _():
        cst_ref[...] = xpad_ref[SUBLANES + n_in - 3:SUBLANES + n_in, :]

    xpad_ref[0:SUBLANES, :] = xpad_ref[lc:lc + SUBLANES, :]

    if n_in < lc:
        dtp_ref, zp_ref = pad_refs
        dtp_ref[...] = jnp.zeros(dtp_ref.shape, F32)
        dtp_ref[0:n_in, :] = dt_ref[...]
        zp_ref[...] = jnp.zeros(zp_ref.shape, F32)
        zp_ref[0:n_in, :] = z_ref[...]
        dt_raw = dtp_ref[...]
        z_all = zp_ref
    else:
        dt_raw = dt_ref[...]
        z_all = z_ref
    pre = dt_raw + dtb_ref[...]
    dt = jnp.maximum(pre, 0.0) + jnp.log1p(jnp.exp(-jnp.abs(pre)))
    row = lax.broadcasted_iota(jnp.int32, (lc, LANES), 0)
    dt = jnp.where(lax.broadcasted_iota(jnp.int32, (lc, LANES), 1) < B_HEADS, dt, 0.0)
    if n_in < lc:
        dt = jnp.where(row < n_in, dt, 0.0)
    a = -jnp.exp(alog_ref[...])
    cum = dt * (a * LOG2_E)
    shift = 1
    while shift < lc:
        cum = cum + jnp.where(row >= shift, pltpu.roll(cum, shift, axis=0), 0.0)
        shift *= 2
    cum_t = cum.T
    dt_t = dt.T
    c_last = cum[lc - 1:lc, :]
    e = e_ref[...]
    x1_ref[...] = _expand_heads(jnp.exp2(cum), e)
    x2 = _expand_heads(jnp.exp2(c_last - cum) * dt, e)
    x3 = _expand_heads(jnp.broadcast_to(jnp.exp2(c_last), (SUBLANES, LANES)), e)[0:1, :]

    xd_ref[...] = (xbc_ref2[:, 0:B_WIDTH] * x2).astype(BF16)
    li = lax.broadcasted_iota(jnp.int32, (lc, lc), 0)
    si = lax.broadcasted_iota(jnp.int32, (lc, lc), 1)
    causal = li >= si
    lane = lax.broadcasted_iota(jnp.int32, (lc, LANES), 1)
    dsk = dsk_ref[...]
    for g in range(B_GROUPS):
        bg = xbc_ref2[:, B_WIDTH + B_STATE * g:B_WIDTH + B_STATE * (g + 1)]
        cg = xbc_ref2[:, B_WIDTH + B_GROUPS * B_STATE + B_STATE * g:B_WIDTH + B_GROUPS * B_STATE + B_STATE * (g + 1)]
        bgb = bg.astype(BF16)
        cgb = cg.astype(BF16)
        cb = _dot_nt(cgb, bgb)
        gs = slice(B_GROUP_W * g, B_GROUP_W * (g + 1))
        state = st_ref[g]
        y_off = _dot(cgb, state.astype(BF16)) * x1_ref[:, gs]
        for pp in range(B_GROUP_W // LANES):
            col = B_GROUP_W * g + LANES * pp
            xp = xbc_ref2[:, col:col + LANES]
            y_pair = y_off[:, LANES * pp:LANES * (pp + 1)] + dsk[:, col:col + LANES] * xp
            xpb = xp.astype(BF16)
            y_half = []
            for half in range(2):
                head = col // B_HEAD_DIM + half
                seg = cum[:, head:head + 1] - cum_t[head:head + 1, :]
                wgt = cb * jnp.exp2(jnp.where(causal, seg, NEG_BIG)) * dt_t[head:head + 1, :]
                y_half.append(_dot(wgt.astype(BF16), xpb))
            y_pair = y_pair + jnp.where(lane < B_HEAD_DIM, y_half[0], y_half[1])
            yacc_ref[:, col:col + LANES] = y_pair
        st_ref[g] = state * x3[:, gs] + _dot(bg.T.astype(BF16), xd_ref[:, gs])

    for g in range(B_GROUPS):
        gs = slice(B_GROUP_W * g, B_GROUP_W * (g + 1))
        yg = yacc_ref[:, gs] * _silu(z_all[:, gs])
        ms = jnp.mean(yg * yg, axis=-1, keepdims=True)
        out = (yg * lax.rsqrt(ms + EPS) * nrm_ref[:, gs]).astype(y_ref.dtype)
        y_ref[:, gs] = out[0:n_in, :]

    @pl.when(c == n_chunks - 1)
    def _():
        for g in range(B_GROUPS):
            hl_ref[B_GROUP_W * g:B_GROUP_W * (g + 1), :] = st_ref[g].T


def _mamba(u, conv_w, conv_b, dt_bias, a_log, d_skip, ssm_norm, conv_prev, h0, *, n_batch, t_len):
    lc = LANES
    if t_len % lc == 0:
        n_in = lc
    else:
        assert t_len < lc
        n_in = t_len
    nc = t_len // n_in
    has_init = conv_prev is not None
    row = lambda b, c: b * nc + c
    pad1 = lambda v: jnp.concatenate([v.astype(F32), jnp.zeros((LANES - B_HEADS,), F32)]).reshape(1, LANES)
    head_of_col = jnp.arange(B_WIDTH, dtype=jnp.int32) // B_HEAD_DIM
    expand = (jnp.arange(LANES, dtype=jnp.int32)[:, None] == head_of_col[None, :]).astype(BF16)
    dsk_row = jnp.repeat(d_skip.astype(F32), B_HEAD_DIM).reshape(1, B_WIDTH)
    const = lambda shape: pl.BlockSpec(shape, lambda b, c: (0,) * len(shape))
    in_specs = [pl.BlockSpec((n_in, B_WIDTH), lambda b, c: (row(b, c), COL_BZ // B_WIDTH)),
                pl.BlockSpec((n_in, B_CONV_DIM), lambda b, c: (row(b, c), COL_XBC // B_CONV_DIM)),
                pl.BlockSpec((n_in, LANES), lambda b, c: (row(b, c), COL_DT // LANES)),
                const((B_CONV, B_CONV_DIM)), const((1, B_CONV_DIM)), const((1, LANES)), const((1, LANES)),
                const((1, B_WIDTH)), const((1, B_WIDTH)), const((LANES, B_WIDTH))]
    args = [u, u, u, conv_w, conv_b.reshape(1, B_CONV_DIM), pad1(dt_bias), pad1(a_log), dsk_row,
            ssm_norm.reshape(1, B_WIDTH), expand]
    if has_init:
        cprev8 = jnp.concatenate([jnp.zeros((n_batch, SUBLANES - (B_CONV - 1), B_CONV_DIM), F32), conv_prev], axis=1)
        in_specs += [pl.BlockSpec((None, SUBLANES, B_CONV_DIM), lambda b, c: (b, 0, 0)),
                     pl.BlockSpec((None, B_WIDTH, B_STATE), lambda b, c: (b, 0, 0))]
        args += [cprev8, h0.reshape(n_batch, B_WIDTH, B_STATE)]
    pad_scratch = []
    if n_in < lc:
        pad_scratch = [pltpu.VMEM((lc, LANES), F32), pltpu.VMEM((lc, B_WIDTH), F32)]
    kern = functools.partial(_mamba_kernel, lc=lc, n_in=n_in, has_init=has_init)
    y, cst, hl = pl.pallas_call(
        kern,
        grid=(n_batch, nc),
        in_specs=in_specs,
        out_specs=[pl.BlockSpec((n_in, B_WIDTH), lambda b, c: (row(b, c), 0)),
                   pl.BlockSpec((None, B_CONV - 1, B_CONV_DIM), lambda b, c: (b, 0, 0)),
                   pl.BlockSpec((None, B_WIDTH, B_STATE), lambda b, c: (b, 0, 0))],
        out_shape=[jax.ShapeDtypeStruct((n_batch * t_len, B_WIDTH), BF16),
                   jax.ShapeDtypeStruct((n_batch, B_CONV - 1, B_CONV_DIM), F32),
                   jax.ShapeDtypeStruct((n_batch, B_WIDTH, B_STATE), F32)],
        scratch_shapes=[pltpu.VMEM((SUBLANES + lc, B_CONV_DIM), F32),
                        pltpu.VMEM((B_GROUPS, B_STATE, B_GROUP_W), F32),
                        pltpu.VMEM((lc, B_WIDTH), F32),
                        pltpu.VMEM((lc, B_CONV_DIM), F32), pltpu.VMEM((lc, B_WIDTH), F32),
                        pltpu.VMEM((lc, B_WIDTH), BF16)] + pad_scratch,
        compiler_params=_cparams(2),
        name="mamba",
    )(*args)
    return y, cst, hl.reshape(n_batch, B_HEADS, B_HEAD_DIM, B_STATE)


def _mem_kernel(q_ref, z_ref, k_ref, v_ref, o_ref):
    q = q_ref[...].astype(BF16)
    scale = M_HEAD_DIM ** -0.5
    for head in range(M_HEADS):
        hs = slice(M_HEAD_DIM * head, M_HEAD_DIM * (head + 1))
        s = _dot_nt(q[:, hs], k_ref[:, hs].astype(BF16)) * scale
        s_max = jnp.max(s, axis=-1, keepdims=True)
        p = jnp.exp(s - s_max)
        denom = jnp.sum(p, axis=-1, keepdims=True)
        o = _dot(p.astype(BF16), v_ref[:, hs].astype(BF16)) / denom
        o_ref[:, hs] = (o * _silu(z_ref[:, hs])).astype(o_ref.dtype)


def _mem_attend(u, mk, mv, *, n_batch, t_len):
    tq = 512 if t_len % 512 == 0 else t_len
    nq = t_len // tq
    row = lambda b, j: b * nq + j
    return pl.pallas_call(
        _mem_kernel,
        grid=(n_batch, nq),
        in_specs=[pl.BlockSpec((tq, M_WIDTH), lambda b, j: (row(b, j), COL_MQ // M_WIDTH)),
                  pl.BlockSpec((tq, M_WIDTH), lambda b, j: (row(b, j), COL_MZ // M_WIDTH)),
                  pl.BlockSpec((None, N_MEM, M_WIDTH), lambda b, j: (b, 0, 0)),
                  pl.BlockSpec((None, N_MEM, M_WIDTH), lambda b, j: (b, 0, 0))],
        out_specs=pl.BlockSpec((tq, M_WIDTH), lambda b, j: (row(b, j), 0)),
        out_shape=jax.ShapeDtypeStruct((n_batch * t_len, M_WIDTH), BF16),
        compiler_params=_cparams(2),
        name="mem_attend",
    )(u, u, mk, mv)


def _merge_kernel(ya_ref, yb_ref, ym_ref, ga_ref, gb_ref, gm_ref, wa_ref, wb_ref, wm_ref, o_ref):
    merged = _sigmoid(ga_ref[...]) * _dot(ya_ref[...], wa_ref[...])
    merged = merged + _sigmoid(gb_ref[...]) * _dot(yb_ref[...], wb_ref[...])
    merged = merged + _sigmoid(gm_ref[...]) * _dot(ym_ref[...], wm_ref[...])
    o_ref[...] = merged.astype(o_ref.dtype)


def _merge(ya, yb, ym, u, w_pa, w_pb, w_pm, tm):
    n = ya.shape[0]
    rows = lambda width: pl.BlockSpec((tm, width), lambda i: (i, 0))
    gate = lambda k: pl.BlockSpec((tm, D_MODEL), lambda i: (i, COL_GATES // D_MODEL + k))
    weight = lambda width: pl.BlockSpec((width, D_MODEL), lambda i: (0, 0), pipeline_mode=pl.Buffered(1))
    return pl.pallas_call(
        _merge_kernel,
        grid=(n // tm,),
        in_specs=[rows(A_WIDTH), rows(B_WIDTH), rows(M_WIDTH), gate(0), gate(1), gate(2),
                  weight(A_WIDTH), weight(B_WIDTH), weight(M_WIDTH)],
        out_specs=rows(D_MODEL),
        out_shape=jax.ShapeDtypeStruct((n, D_MODEL), BF16),
        compiler_params=_cparams(1),
        name="merge",
    )(ya, yb, ym, u, u, u, w_pa, w_pb, w_pm)


def _final_kernel(m_ref, x_ref, wo_ref, g_ref, o_ref):
    tm = o_ref.shape[0]
    sub = min(tm, FINAL_SUB_ROWS)
    for r in range(tm // sub):
        rs = slice(r * sub, (r + 1) * sub)
        y = x_ref[rs, :] + _dot(m_ref[rs, :], wo_ref[...])
        ms = jnp.mean(y * y, axis=-1, keepdims=True)
        o_ref[rs, :] = y * lax.rsqrt(ms + EPS) * g_ref[...]


def _final(merged, x, w_o, g, tm):
    n = x.shape[0]
    rows = pl.BlockSpec((tm, D_MODEL), lambda i: (i, 0))
    return pl.pallas_call(
        _final_kernel,
        grid=(n // tm,),
        in_specs=[rows, rows,
                  pl.BlockSpec((D_MODEL, D_MODEL), lambda i: (0, 0), pipeline_mode=pl.Buffered(1)),
                  pl.BlockSpec((1, D_MODEL), lambda i: (0, 0))],
        out_specs=rows,
        out_shape=jax.ShapeDtypeStruct((n, D_MODEL), F32),
        compiler_params=_cparams(1),
        name="final",
    )(merged, x, w_o, g.reshape(1, D_MODEL))


def _row_tile(n, pref):
    t = pref
    while n % t:
        t //= 2
    return t


def _pad_keys(t, n_keys):
    pad = n_keys - t.shape[1]
    if pad == 0:
        return t
    return jnp.concatenate([t, jnp.zeros((t.shape[0], pad, t.shape[2]), t.dtype)], axis=1)


def _layer(x, pos0, past_k, past_v, past_ki, conv_prev, h0, mem_k, mem_v, lw, norm_final):
    (norm_in, w_in_packed, conv_w, conv_b, dt_bias, a_log, d_skip, ssm_norm, w_pa, w_pb, w_pm, w_o) = lw
    n_batch, t_len, _ = x.shape
    n = n_batch * t_len
    x2 = x.reshape(n, D_MODEL)
    u = _rms_proj(x2, norm_in, w_in_packed, _row_tile(n, 1024), 1024)

    k_new = u[:, COL_AK:COL_AK + A_KV_WIDTH].reshape(n_batch, t_len, A_KV_WIDTH)
    v_new = u[:, COL_AV:COL_AV + A_KV_WIDTH].reshape(n_batch, t_len, A_KV_WIDTH)
    kikw_new = u[:, COL_IKW:COL_IKW + LANES].reshape(n_batch, t_len, LANES)
    if past_k is None:
        keys, n_valid_keys = None, t_len
    else:
        n_past = past_k.shape[1]
        past_kikw = jnp.concatenate([past_ki, jnp.zeros((n_batch, n_past, LANES - IDX_DIM), F32)], axis=2)
        n_valid_keys = n_past + t_len
        n_keys = -(-n_valid_keys // KEY_TILE) * KEY_TILE
        keys = (_pad_keys(jnp.concatenate([past_k.reshape(n_batch, n_past, A_KV_WIDTH), k_new], axis=1), n_keys),
                _pad_keys(jnp.concatenate([past_v.reshape(n_batch, n_past, A_KV_WIDTH), v_new], axis=1), n_keys),
                _pad_keys(jnp.concatenate([past_kikw, kikw_new], axis=1), n_keys))
    ya = _dsa(u, keys, n_batch=n_batch, t_len=t_len, n_valid_keys=n_valid_keys, pos0=pos0)

    yb, conv_state, h_last = _mamba(u, conv_w, conv_b, dt_bias, a_log, d_skip, ssm_norm, conv_prev, h0,
                                    n_batch=n_batch, t_len=t_len)
    ym = _mem_attend(u, mem_k, mem_v, n_batch=n_batch, t_len=t_len)
    merged = _merge(ya, yb, ym, u, w_pa, w_pb, w_pm, _row_tile(n, 256))
    y = _final(merged, x2, w_o, norm_final, _row_tile(n, 512)).reshape(n_batch, t_len, D_MODEL)
    return (y, k_new.reshape(n_batch, t_len, A_KV_HEADS, A_HEAD_DIM),
            v_new.reshape(n_batch, t_len, A_KV_HEADS, A_HEAD_DIM),
            kikw_new[:, :, 0:IDX_DIM], conv_state, h_last)


def kernel(x_prompt, x_sample, mem_prompt, cache_attn_k, cache_attn_v, cache_idx_k, state_conv, state_ssm,
           cache_mem_k, cache_mem_v, norm_in, w_in, conv_w, conv_b, dt_bias, a_log, d_skip, ssm_norm,
           norm_mem, w_mem_kv, w_pa, w_pb, w_pm, w_o, norm_final):
    depth = w_in.shape[0]
    assert depth == 1, "the final RMSNorm is fused into the (single) layer"
    bp = x_prompt.shape[0]
    bs = x_sample.shape[0]
    first = lambda t: t.reshape(t.shape[1:])
    lw = (first(norm_in), _pack_w_in(w_in), first(conv_w), first(conv_b), first(dt_bias), first(a_log),
          first(d_skip), first(ssm_norm), first(w_pa).astype(BF16), first(w_pb).astype(BF16),
          first(w_pm).astype(BF16), first(w_o).astype(BF16))

    mem2 = mem_prompt.reshape(bp * N_MEM, D_MODEL)
    mkv = _rms_proj(mem2, first(norm_mem), first(w_mem_kv).astype(BF16), _row_tile(bp * N_MEM, 1024), 1024)
    mk_p = mkv[:, 0:M_WIDTH].reshape(bp, N_MEM, M_WIDTH)
    mv_p = mkv[:, M_WIDTH:2 * M_WIDTH].reshape(bp, N_MEM, M_WIDTH)

    yp, kp, vp, kip, convp, ssmp = _layer(x_prompt, 0, None, None, None, None, None, mk_p, mv_p, lw, norm_final)
    ys, ks, vs, kis, convs, ssms = _layer(
        x_sample, PAST_LEN, first(cache_attn_k), first(cache_attn_v), first(cache_idx_k), first(state_conv),
        first(state_ssm), first(cache_mem_k).reshape(bs, N_MEM, M_WIDTH),
        first(cache_mem_v).reshape(bs, N_MEM, M_WIDTH), lw, norm_final)

    st = lambda t: t[None]
    return (yp, ys, st(kp), st(vp), st(kip), st(convp), st(ssmp),
            st(mk_p.reshape(bp, N_MEM, M_HEADS, M_HEAD_DIM)), st(mv_p.reshape(bp, N_MEM, M_HEADS, M_HEAD_DIM)),
            st(ks), st(vs), st(kis), st(convs), st(ssms))
```

```python
import functools

import jax
import jax.numpy as jnp
from jax import lax
from jax.experimental import pallas as pl
from jax.experimental.pallas import tpu as pltpu

F32 = jnp.float32
BF16 = jnp.bfloat16

D_MODEL = 2048
CHUNK = 64
CHUNK_SHIFT = 6
assert 1 << CHUNK_SHIFT == CHUNK
N_MEM = 256
EPS = 1e-6
PAST_LEN = 1024

A_HEADS = 8
A_KV_HEADS = 2
A_HEAD_DIM = 128
A_GROUP = A_HEADS // A_KV_HEADS
A_WIDTH = A_HEADS * A_HEAD_DIM
A_KV_WIDTH = A_KV_HEADS * A_HEAD_DIM
IDX_HEADS = 16
IDX_DIM = 64
TOPK_MAX = 256
IDX_SCALE = (IDX_DIM * IDX_HEADS) ** -0.5

B_WIDTH = 2048
B_HEAD_DIM = 64
B_HEADS = B_WIDTH // B_HEAD_DIM
B_GROUPS = 4
B_HPG = B_HEADS // B_GROUPS
B_STATE = 128
B_CONV = 4
B_CONV_DIM = B_WIDTH + 2 * B_GROUPS * B_STATE
B_GROUP_W = B_WIDTH // B_GROUPS

M_HEADS = 4
M_HEAD_DIM = 256
M_WIDTH = M_HEADS * M_HEAD_DIM

N_BRANCH = 3
IN_SPLITS = (A_WIDTH, A_KV_WIDTH, A_KV_WIDTH, IDX_HEADS * IDX_DIM, IDX_DIM, IDX_HEADS, A_WIDTH,
             B_WIDTH, B_CONV_DIM, B_HEADS, M_WIDTH, M_WIDTH, N_BRANCH * D_MODEL)
IN_COLS = sum(IN_SPLITS)
(SRC_AQ, SRC_AK, SRC_AV, SRC_IQ, SRC_IK, SRC_IW, SRC_AZ, SRC_BZ, SRC_XBC, SRC_DT, SRC_MQ, SRC_MZ,
 SRC_GATES) = (sum(IN_SPLITS[:i]) for i in range(len(IN_SPLITS)))

LANES = 128
SUBLANES = 8
VMEM_LIMIT_BYTES = 56 * 1024 * 1024

COL_GATES = 0
COL_XBC = 6144
COL_AQ = 9216
COL_IQ = 10240
COL_AZ = 11264
COL_MQ = 12288
COL_MZ = 13312
COL_BZ = 14336
COL_AK = 16384
COL_AV = 16640
COL_IKW = 16896
COL_DT = 17152
PACKED_COLS = 17408

PACK_CHUNK = 256
PACK_CHUNKS_PER_STEP = 4
PACK_ROW_ALIGN = 2 * SUBLANES
PACK_SEGMENTS = ((SRC_GATES, COL_GATES, N_BRANCH * D_MODEL), (SRC_XBC, COL_XBC, B_CONV_DIM),
                 (SRC_AQ, COL_AQ, A_WIDTH), (SRC_IQ, COL_IQ, IDX_HEADS * IDX_DIM), (SRC_AZ, COL_AZ, A_WIDTH),
                 (SRC_MQ, COL_MQ, M_WIDTH), (SRC_MZ, COL_MZ, M_WIDTH), (SRC_BZ, COL_BZ, B_WIDTH),
                 (SRC_AK, COL_AK, A_KV_WIDTH), (SRC_AV, COL_AV, A_KV_WIDTH),
                 (SRC_IK, COL_IKW, PACK_CHUNK), (SRC_DT, COL_DT, PACK_CHUNK))


def _pack_source_table():
    table = [None] * (PACKED_COLS // PACK_CHUNK)
    for src, dst, width in PACK_SEGMENTS:
        assert dst % PACK_CHUNK == 0 and width % PACK_CHUNK == 0 and src % (2 * SUBLANES) == 0
        for off in range(0, width, PACK_CHUNK):
            assert src + off + PACK_CHUNK <= IN_COLS
            table[(dst + off) // PACK_CHUNK] = src + off
    assert all(t is not None for t in table)
    return table


PACK_SOURCE = _pack_source_table()

KEY_TILE = 256
PROJ_SUB_ROWS = 512
CONV_SUB_ROWS = 256
FINAL_SUB_ROWS = 256
LOG2_E = 1.4426950408889634
INT_MIN = -2 ** 31
KEY_NEG_INF = INT_MIN + 0x7FFFFF
NEG_BIG = -1e30


def _cparams(n_grid):
    return pltpu.CompilerParams(dimension_semantics=("arbitrary",) * n_grid,
                                vmem_limit_bytes=VMEM_LIMIT_BYTES)


def _sigmoid(z):
    return 0.5 * jnp.tanh(0.5 * z) + 0.5


def _silu(z):
    half = 0.5 * z
    return half * jnp.tanh(half) + half


def _dot(a, b):
    return jnp.dot(a, b, preferred_element_type=F32)


def _dot_nt(a, b):
    return lax.dot_general(a, b, (((1,), (1,)), ((), ())), preferred_element_type=F32)


def _pack_kernel(src_ref, *refs):
    del src_ref
    *wt_refs, o_ref = refs
    for k, wt_ref in enumerate(wt_refs):
        o_ref[:, PACK_CHUNK * k:PACK_CHUNK * (k + 1)] = wt_ref[...].T.astype(o_ref.dtype)


def _pack_w_in(w):
    wt = jnp.transpose(w.reshape(D_MODEL, IN_COLS))
    per_step = PACK_CHUNKS_PER_STEP
    window = lambda k: pl.BlockSpec((pl.Element(PACK_CHUNK), pl.Element(D_MODEL)),
                                    lambda i, src: (src[per_step * i + k] * PACK_ROW_ALIGN, 0))
    grid_spec = pltpu.PrefetchScalarGridSpec(
        num_scalar_prefetch=1,
        grid=(PACKED_COLS // (PACK_CHUNK * per_step),),
        in_specs=[window(k) for k in range(per_step)],
        out_specs=pl.BlockSpec((D_MODEL, PACK_CHUNK * per_step), lambda i, src: (0, i)))
    return pl.pallas_call(
        _pack_kernel,
        grid_spec=grid_spec,
        out_shape=jax.ShapeDtypeStruct((D_MODEL, PACKED_COLS), BF16),
        compiler_params=_cparams(1),
        name="pack_w_in",
    )(jnp.asarray([s // PACK_ROW_ALIGN for s in PACK_SOURCE], jnp.int32), *([wt] * per_step))


def _proj_kernel(*refs, conv):
    if conv is None:
        x_ref, g_ref, w_ref, o_ref, h_ref = refs
    else:
        x_ref, g_ref, w_ref, cw_ref, cb_ref, o_ref, xc_ref, h_ref, pad_ref, halo_ref = refs
    i = pl.program_id(0)
    j = pl.program_id(1)

    @pl.when(j == 0)
    def _():
        x = x_ref[...]
        ms = jnp.mean(x * x, axis=-1, keepdims=True)
        h_ref[...] = (x * lax.rsqrt(ms + EPS) * g_ref[...]).astype(BF16)

    if conv is not None:
        @pl.when(jnp.logical_and(i == 0, j == 0))
        def _():
            halo_ref[...] = jnp.zeros(halo_ref.shape, F32)

    tm = o_ref.shape[0]

    def plain():
        sub = min(tm, PROJ_SUB_ROWS)
        for r in range(tm // sub):
            o_ref[r * sub:(r + 1) * sub, :] = _dot(h_ref[r * sub:(r + 1) * sub, :], w_ref[...])

    if conv is None:
        plain()
        return
    first_tile, n_conv_tiles, tiles_per_seq = conv
    is_conv = jnp.logical_and(j >= first_tile, j < first_tile + n_conv_tiles)
    pl.when(jnp.logical_not(is_conv))(plain)

    @pl.when(is_conv)
    def _():
        c = j - first_tile
        seq_start = (i % tiles_per_seq) == 0
        sub = pad_ref.shape[1] - SUBLANES
        n_sub = tm // sub
        base = SUBLANES - (B_CONV - 1)
        for r in range(n_sub + 1):
            if r < n_sub:
                slot = r % 2
                if r == 0:
                    pad_ref[slot, 0:SUBLANES, :] = jnp.where(seq_start, 0.0, halo_ref[c])
                else:
                    pad_ref[slot, 0:SUBLANES, :] = pad_ref[1 - slot, sub:sub + SUBLANES, :]
                res = _dot(h_ref[r * sub:(r + 1) * sub, :], w_ref[...])
                o_ref[r * sub:(r + 1) * sub, :] = res
                pad_ref[slot, SUBLANES:SUBLANES + sub, :] = res
            if r >= 1:
                prev = (r - 1) % 2
                acc = cb_ref[...] + cw_ref[0:1, :] * pad_ref[prev, base:base + sub, :]
                for k in range(1, B_CONV):
                    acc = acc + cw_ref[k:k + 1, :] * pad_ref[prev, base + k:base + k + sub, :]
                xc_ref[(r - 1) * sub:r * sub, :] = _silu(acc)
        halo_ref[c] = pad_ref[(n_sub - 1) % 2, sub:sub + SUBLANES, :]


def _rms_proj(x, g, w, tm, tn, conv=None):
    m, d = x.shape
    n = w.shape[1]
    in_specs = [pl.BlockSpec((tm, d), lambda i, j: (i, 0)),
                pl.BlockSpec((1, d), lambda i, j: (0, 0)),
                pl.BlockSpec((d, tn), lambda i, j: (0, j))]
    args = [x, g.reshape(1, d), w]
    out_specs = pl.BlockSpec((tm, tn), lambda i, j: (i, j))
    out_shape = jax.ShapeDtypeStruct((m, n), F32)
    scratch = [pltpu.VMEM((tm, d), BF16)]
    kern_conv = None
    if conv is not None:
        conv_w, conv_b, t_len = conv
        assert COL_XBC % tn == 0 and B_CONV_DIM % tn == 0 and t_len % tm == 0
        first, count = COL_XBC // tn, B_CONV_DIM // tn
        ctile = lambda i, j: jnp.clip(j - first, 0, count - 1)
        in_specs += [pl.BlockSpec((B_CONV, tn), lambda i, j: (0, ctile(i, j))),
                     pl.BlockSpec((1, tn), lambda i, j: (0, ctile(i, j)))]
        args += [conv_w, conv_b.reshape(1, B_CONV_DIM)]
        out_specs = [out_specs, pl.BlockSpec((tm, tn), lambda i, j: (i, ctile(i, j)))]
        out_shape = [out_shape, jax.ShapeDtypeStruct((m, B_CONV_DIM), F32)]
        scratch += [pltpu.VMEM((2, SUBLANES + min(tm, CONV_SUB_ROWS), tn), F32),
                    pltpu.VMEM((count, SUBLANES, tn), F32)]
        kern_conv = (first, count, t_len // tm)
    return pl.pallas_call(
        functools.partial(_proj_kernel, conv=kern_conv),
        grid=(m // tm, n // tn),
        in_specs=in_specs,
        out_specs=out_specs,
        out_shape=out_shape,
        scratch_shapes=scratch,
        compiler_params=_cparams(2),
        name="rms_proj",
    )(*args)


def _key_to_f32(key):
    bits = jnp.where(key >= 0, key, key ^ jnp.int32(0x7FFFFFFF))
    return pltpu.bitcast(bits, F32)


def _dsa_kernel(q_ref, iq_ref, ikw_ref, az_ref, k_ref, v_ref, kikw_ref, o_ref,
                kb_ref, vb_ref, kie_ref, kio_ref, iqb_ref, wt_ref, qs_ref, sc_ref, bias_ref,
                cnt_ref, gt_ref, m_ref, ans_ref, s_ref, p_ref, acc_ref, stat_ref, *pad_refs,
                tq, tq_in, nq, n_tiles, n_valid_keys, pos0, topk):
    jq = pl.program_id(1)
    kt_sz = KEY_TILE
    topk_f = float(topk)

    @pl.when(jq == 0)
    def _():
        for kt in range(n_tiles):
            tile = slice(kt_sz * kt, kt_sz * (kt + 1))
            kb_ref[kt] = k_ref[tile, :].astype(BF16)
            v_t = v_ref[tile, :].T.astype(BF16)
            for g in range(A_KV_HEADS):
                vb_ref[kt, g] = v_t[A_HEAD_DIM * g:A_HEAD_DIM * (g + 1), :]
            ki = kikw_ref[tile, :]
            lane = lax.broadcasted_iota(jnp.int32, ki.shape, 1)
            kie = jnp.where(lane < IDX_DIM, ki, 0.0)
            kie_ref[kt] = kie.astype(BF16)
            kio_ref[kt] = pltpu.roll(kie, IDX_DIM, axis=1).astype(BF16)

    if tq_in < tq:
        qp_ref, iqp_ref, ikwp_ref, azp_ref = pad_refs
        for dst, src in ((qp_ref, q_ref), (iqp_ref, iq_ref), (ikwp_ref, ikw_ref), (azp_ref, az_ref)):
            dst[...] = jnp.zeros(dst.shape, dst.dtype)
            dst[0:tq_in, :] = src[...]
        q_ref, iq_ref, ikw_ref, az_ref = qp_ref, iqp_ref, ikwp_ref, azp_ref

    iqb_ref[...] = iq_ref[...].astype(BF16)
    wt_ref[...] = ikw_ref[...].T * IDX_SCALE
    q = (q_ref[...] * ((A_HEAD_DIM ** -0.5) * LOG2_E)).astype(BF16)
    for head in range(A_HEADS):
        g, hh = divmod(head, A_GROUP)
        qs_ref[g, hh * tq:(hh + 1) * tq, :] = q[:, A_HEAD_DIM * head:A_HEAD_DIM * (head + 1)]

    q_last = pos0 + (jq + 1) * tq_in - 1
    key_end = jnp.minimum((q_last // CHUNK + 1) * CHUNK, n_valid_keys)
    n_need = (key_end + kt_sz - 1) // kt_sz

    def for_needed_tiles(body):
        for kt in range(n_tiles):
            pl.when(kt < n_need)(functools.partial(body, kt))

    krow = lax.broadcasted_iota(jnp.int32, (kt_sz, tq), 0)
    qcol = lax.broadcasted_iota(jnp.int32, (kt_sz, tq), 1)
    qchunk = jnp.right_shift(pos0 + jq * tq_in + qcol, CHUNK_SHIFT)

    def score_tile(kt):
        acc = None
        for pair in range(IDX_HEADS // 2):
            iq_pair = iqb_ref[:, LANES * pair:LANES * (pair + 1)]
            for half, kref in enumerate((kie_ref, kio_ref)):
                head = 2 * pair + half
                d = _dot_nt(kref[kt], iq_pair)
                contrib = jnp.maximum(d, 0.0) * wt_ref[IDX_DIM + head:IDX_DIM + head + 1, :]
                acc = contrib if acc is None else acc + contrib
        kpos = kt * kt_sz + krow
        if (kt + 1) * kt_sz > n_valid_keys:
            acc = jnp.where(kpos < n_valid_keys, acc, -jnp.inf)
        sc_ref[kt] = jnp.where(jnp.right_shift(kpos, CHUNK_SHIFT) <= qchunk, acc, -jnp.inf)

    for_needed_tiles(score_tile)

    def count_into(ref, kt, hit):
        ref[...] += jnp.sum(hit.reshape(kt_sz // SUBLANES, SUBLANES, tq), axis=0)

    def total(ref):
        return jnp.sum(ref[...], axis=0, keepdims=True)

    def run_bisection(n_static):
        def bisect(i, carry):
            ans, cnt_ans = carry
            cand = ans + jnp.left_shift(jnp.int32(1), jnp.int32(31) - i)
            cand_f = _key_to_f32(jnp.maximum(cand, jnp.int32(KEY_NEG_INF)))
            part = jnp.zeros((SUBLANES, tq), F32)
            for kt in range(n_static):
                hit = jnp.where(sc_ref[kt] >= cand_f, 1.0, 0.0)
                part = part + jnp.sum(hit.reshape(kt_sz // SUBLANES, SUBLANES, tq), axis=0)
            cnt = jnp.sum(part, axis=0, keepdims=True)
            keep = cnt >= topk_f
            return jnp.where(keep, cand, ans), jnp.where(keep, cnt, cnt_ans)

        init = (jnp.full((1, tq), INT_MIN, jnp.int32), jnp.zeros((1, tq), F32))
        ans, cnt_ans = lax.fori_loop(0, 32, bisect, init)
        ans_ref[...] = jnp.broadcast_to(ans, ans_ref.shape)
        cnt_ref[...] = jnp.broadcast_to(cnt_ans, cnt_ref.shape)

    need_of = []
    for j in range(nq):
        end = min(((pos0 + (j + 1) * tq_in - 1) // CHUNK + 1) * CHUNK, n_valid_keys)
        need_of.append(-(-end // kt_sz))
    for n_static in sorted(set(need_of)):
        blocks = [j for j in range(nq) if need_of[j] == n_static]
        if len(blocks) == nq:
            run_bisection(n_static)
        else:
            in_range = jnp.logical_and(jq >= blocks[0], jq <= blocks[-1])
            pl.when(in_range)(functools.partial(run_bisection, n_static))
    ans = ans_ref[0:1, :]
    thr = _key_to_f32(jnp.maximum(ans, jnp.int32(KEY_NEG_INF)))

    cnt_ge = cnt_ref[0:1, :]
    qcol1 = lax.broadcasted_iota(jnp.int32, (1, tq), 1)
    tie = jnp.where(cnt_ge > topk_f, jnp.where(thr > -jnp.inf, jnp.where(qcol1 < tq_in, 1.0, 0.0), 0.0), 0.0)
    m_ref[...] = jnp.full(m_ref.shape, n_tiles * kt_sz, jnp.int32)

    @pl.when(jnp.max(tie) > 0.0)
    def _():
        nbits = max(1, (n_tiles * kt_sz - 1).bit_length())
        gt_ref[...] = jnp.zeros(gt_ref.shape, F32)
        for_needed_tiles(lambda kt: count_into(gt_ref, kt, jnp.where(sc_ref[kt] > thr, 1.0, 0.0)))
        cnt_gt = total(gt_ref)

        def bisect_pos(i, t):
            cand = t + jnp.left_shift(jnp.int32(1), jnp.int32(nbits - 1) - i)
            cnt_ref[...] = jnp.zeros(cnt_ref.shape, F32)

            def count_below(kt):
                hit = jnp.where(sc_ref[kt] == thr, jnp.where(kt * kt_sz + krow < cand, 1.0, 0.0), 0.0)
                count_into(cnt_ref, kt, hit)

            for_needed_tiles(count_below)
            return jnp.where(cnt_gt + total(cnt_ref) < topk_f, cand, t)

        t_last = lax.fori_loop(0, nbits, bisect_pos, jnp.zeros((1, tq), jnp.int32))
        m_ref[...] = jnp.broadcast_to(t_last, m_ref.shape)

    m_last = m_ref[0:1, :]
    thr_eq = jnp.where(thr > -jnp.inf, thr, jnp.inf)

    def bias_tile(kt):
        sc = sc_ref[kt]
        kpos = kt * kt_sz + krow
        tied = jnp.where(sc == thr_eq, jnp.where(kpos <= m_last, 0.0, NEG_BIG), NEG_BIG)
        bias_ref[kt] = jnp.where(sc > thr, 0.0, tied)

    for_needed_tiles(bias_tile)

    cols = A_GROUP * tq
    groups = range(A_KV_HEADS)
    gsl = [slice(A_HEAD_DIM * g, A_HEAD_DIM * (g + 1)) for g in groups]
    ROW_MAX, ROW_SUM, ROW_ALPHA, ROW_TILE_MAX = 0, 1, 2, 3

    def stat(g, k):
        return stat_ref[SUBLANES * g + k:SUBLANES * g + k + 1, :]

    def set_stat(g, k, v):
        stat_ref[SUBLANES * g + k:SUBLANES * g + k + 1, :] = v

    def produce(kt, slot):
        bias = jnp.concatenate([bias_ref[kt]] * A_GROUP, axis=1)
        for g in groups:
            s = _dot_nt(kb_ref[kt, :, gsl[g]], qs_ref[g]) + bias
            s_ref[slot, g] = s
            set_stat(g, ROW_TILE_MAX + slot, jnp.max(s, axis=0, keepdims=True))

    def consume(kt, slot):
        k_prev = jnp.maximum(kt - 1, 0)
        for g in groups:
            pv = _dot(vb_ref[k_prev, g], p_ref[g])
            acc_ref[g] = acc_ref[g] * stat(g, ROW_ALPHA) + pv
            m_run = stat(g, ROW_MAX)
            m = jnp.maximum(m_run, stat(g, ROW_TILE_MAX + slot))
            alpha = jnp.exp2(m_run - m)
            p = jnp.exp2(s_ref[slot, g] - m)
            set_stat(g, ROW_SUM, alpha * stat(g, ROW_SUM) + jnp.sum(p, axis=0, keepdims=True))
            p_ref[g] = p.astype(BF16)
            set_stat(g, ROW_MAX, m)
            set_stat(g, ROW_ALPHA, alpha)

    acc_ref[...] = jnp.zeros(acc_ref.shape, F32)
    p_ref[...] = jnp.zeros(p_ref.shape, BF16)
    for g in groups:
        set_stat(g, ROW_MAX, jnp.full((1, cols), NEG_BIG, F32))
        set_stat(g, ROW_ALPHA, jnp.ones((1, cols), F32))
        set_stat(g, ROW_SUM, jnp.zeros((1, cols), F32))
    produce(0, 0)

    def tile_pair(i, carry):
        k0 = 2 * i
        produce(jnp.minimum(k0 + 1, n_need - 1), 1)
        consume(k0, 0)

        @pl.when(k0 + 1 < n_need)
        def _():
            produce(jnp.minimum(k0 + 2, n_need - 1), 0)
            consume(k0 + 1, 1)

        return carry

    lax.fori_loop(0, (n_need + 1) // 2, tile_pair, 0)
    for g in groups:
        acc = acc_ref[g] * stat(g, ROW_ALPHA) + _dot(vb_ref[n_need - 1, g], p_ref[g])
        o = (acc / stat(g, ROW_SUM)).T
        for hh in range(A_GROUP):
            hs = slice(A_HEAD_DIM * (A_GROUP * g + hh), A_HEAD_DIM * (A_GROUP * g + hh + 1))
            res = (o[hh * tq:(hh + 1) * tq, :] * _silu(az_ref[:, hs])).astype(o_ref.dtype)
            o_ref[:, hs] = res[0:tq_in, :]


def _dsa(u, keys, *, n_batch, t_len, n_valid_keys, pos0):
    n_keys = t_len if keys is None else keys[0].shape[1]
    assert n_keys % KEY_TILE == 0
    n_tiles = n_keys // KEY_TILE
    topk = min(TOPK_MAX, n_valid_keys // 4)
    if t_len % 256 == 0:
        tq = tq_in = 256
    elif t_len % LANES == 0:
        tq = tq_in = LANES
    else:
        tq, tq_in = LANES, t_len
    nq = t_len // tq_in
    row = lambda b, j: b * nq + j
    if keys is None:
        key_args = (u, u, u)
        key_specs = [pl.BlockSpec((n_keys, A_KV_WIDTH), lambda b, j: (b, COL_AK // A_KV_WIDTH)),
                     pl.BlockSpec((n_keys, A_KV_WIDTH), lambda b, j: (b, COL_AV // A_KV_WIDTH)),
                     pl.BlockSpec((n_keys, LANES), lambda b, j: (b, COL_IKW // LANES))]
    else:
        key_args = keys
        key_specs = [pl.BlockSpec((None, n_keys, t.shape[-1]), lambda b, j: (b, 0, 0)) for t in keys]
    pad_scratch = []
    if tq_in < tq:
        pad_scratch = [pltpu.VMEM((tq, A_WIDTH), F32), pltpu.VMEM((tq, IDX_HEADS * IDX_DIM), F32),
                       pltpu.VMEM((tq, LANES), F32), pltpu.VMEM((tq, A_WIDTH), F32)]
    kern = functools.partial(_dsa_kernel, tq=tq, tq_in=tq_in, nq=nq, n_tiles=n_tiles,
                             n_valid_keys=n_valid_keys, pos0=pos0, topk=topk)
    return pl.pallas_call(
        kern,
        grid=(n_batch, nq),
        in_specs=[pl.BlockSpec((tq_in, A_WIDTH), lambda b, j: (row(b, j), COL_AQ // A_WIDTH)),
                  pl.BlockSpec((tq_in, A_WIDTH), lambda b, j: (row(b, j), COL_IQ // A_WIDTH)),
                  pl.BlockSpec((tq_in, LANES), lambda b, j: (row(b, j), COL_IKW // LANES)),
                  pl.BlockSpec((tq_in, A_WIDTH), lambda b, j: (row(b, j), COL_AZ // A_WIDTH)),
                  *key_specs],
        out_specs=pl.BlockSpec((tq_in, A_WIDTH), lambda b, j: (row(b, j), 0)),
        out_shape=jax.ShapeDtypeStruct((n_batch * t_len, A_WIDTH), BF16),
        scratch_shapes=[pltpu.VMEM((n_tiles, KEY_TILE, A_KV_WIDTH), BF16),
                        pltpu.VMEM((n_tiles, A_KV_HEADS, A_HEAD_DIM, KEY_TILE), BF16),
                        pltpu.VMEM((n_tiles, KEY_TILE, LANES), BF16),
                        pltpu.VMEM((n_tiles, KEY_TILE, LANES), BF16),
                        pltpu.VMEM((tq, IDX_HEADS * IDX_DIM), BF16),
                        pltpu.VMEM((LANES, tq), F32),
                        pltpu.VMEM((A_KV_HEADS, A_GROUP * tq, A_HEAD_DIM), BF16),
                        pltpu.VMEM((n_tiles, KEY_TILE, tq), F32),
                        pltpu.VMEM((n_tiles, KEY_TILE, tq), F32),
                        pltpu.VMEM((SUBLANES, tq), F32), pltpu.VMEM((SUBLANES, tq), F32),
                        pltpu.VMEM((SUBLANES, tq), jnp.int32),
                        pltpu.VMEM((SUBLANES, tq), jnp.int32),
                        pltpu.VMEM((2, A_KV_HEADS, KEY_TILE, A_GROUP * tq), F32),
                        pltpu.VMEM((A_KV_HEADS, KEY_TILE, A_GROUP * tq), BF16),
                        pltpu.VMEM((A_KV_HEADS, A_HEAD_DIM, A_GROUP * tq), F32),
                        pltpu.VMEM((A_KV_HEADS * SUBLANES, A_GROUP * tq), F32)] + pad_scratch,
        compiler_params=_cparams(2),
        name="dsa",
    )(u, u, u, u, *key_args)


def _expand_heads(v, e):
    hi = v.astype(BF16)
    lo = (v - hi.astype(F32)).astype(BF16)
    return _dot(hi, e) + _dot(lo, e)


def _mamba_kernel(*refs, lc, n_in, has_init, conv_done):
    z_ref, xbc_ref, dt_ref, *rest = refs
    if conv_done:
        tail_ref, *rest = rest
    else:
        cw_ref, cb_ref, *rest = rest
    dtb_ref, alog_ref, dsk_ref, nrm_ref, e_ref, *rest = rest
    if has_init:
        cprev_ref, h0_ref, *rest = rest
    y_ref, cst_ref, hl_ref, xpad_ref, st_ref, yacc_ref, xbc_ref2, x1_ref, xd_ref, *pad_refs = rest
    c = pl.program_id(1)
    n_chunks = pl.num_programs(1)

    @pl.when(c == 0)
    def _():
        xpad_ref[...] = jnp.zeros(xpad_ref.shape, F32)
        if has_init:
            xpad_ref[0:SUBLANES, :] = cprev_ref[...]
            for g in range(B_GROUPS):
                st_ref[g] = h0_ref[B_GROUP_W * g:B_GROUP_W * (g + 1), :].T
        else:
            st_ref[...] = jnp.zeros(st_ref.shape, F32)

    if conv_done:
        xbc_act = xbc_ref

        @pl.when(c == n_chunks - 1)
        def _():
            cst_ref[...] = tail_ref[SUBLANES - (B_CONV - 1):SUBLANES, :]
    else:
        xpad_ref[SUBLANES:SUBLANES + n_in, :] = xbc_ref[...]
        acc = cb_ref[...] + cw_ref[0:1, :] * xpad_ref[SUBLANES - 3:SUBLANES - 3 + lc, :]
        for j in range(1, B_CONV):
            acc = acc + cw_ref[j:j + 1, :] * xpad_ref[SUBLANES - 3 + j:SUBLANES - 3 + j + lc, :]
        xbc_ref2[...] = _silu(acc)
        xbc_act = xbc_ref2

        @pl.when(c == n_chunks - 1)
        def _():
            cst_ref[...] = xpad_ref[SUBLANES + n_in - 3:SUBLANES + n_in, :]

        xpad_ref[0:SUBLANES, :] = xpad_ref[lc:lc + SUBLANES, :]

    if n_in < lc:
        dtp_ref, zp_ref = pad_refs
        dtp_ref[...] = jnp.zeros(dtp_ref.shape, F32)
        dtp_ref[0:n_in, :] = dt_ref[...]
        zp_ref[...] = jnp.zeros(zp_ref.shape, F32)
        zp_ref[0:n_in, :] = z_ref[...]
        dt_raw = dtp_ref[...]
        z_all = zp_ref
    else:
        dt_raw = dt_ref[...]
        z_all = z_ref
    pre = dt_raw + dtb_ref[...]
    dt = jnp.maximum(pre, 0.0) + jnp.log1p(jnp.exp(-jnp.abs(pre)))
    row = lax.broadcasted_iota(jnp.int32, (lc, LANES), 0)
    dt = jnp.where(lax.broadcasted_iota(jnp.int32, (lc, LANES), 1) < B_HEADS, dt, 0.0)
    if n_in < lc:
        dt = jnp.where(row < n_in, dt, 0.0)
    a = -jnp.exp(alog_ref[...])
    cum = dt * (a * LOG2_E)
    shift = 1
    while shift < lc:
        cum = cum + jnp.where(row >= shift, pltpu.roll(cum, shift, axis=0), 0.0)
        shift *= 2
    cum_t = cum.T
    dt_t = dt.T
    c_last = cum[lc - 1:lc, :]
    e = e_ref[...]
    x1_ref[...] = _expand_heads(jnp.exp2(cum), e)
    x2 = _expand_heads(jnp.exp2(c_last - cum) * dt, e)
    x3 = _expand_heads(jnp.broadcast_to(jnp.exp2(c_last), (SUBLANES, LANES)), e)[0:1, :]

    xd_ref[...] = (xbc_act[:, 0:B_WIDTH] * x2).astype(BF16)
    li = lax.broadcasted_iota(jnp.int32, (lc, lc), 0)
    si = lax.broadcasted_iota(jnp.int32, (lc, lc), 1)
    causal = li >= si
    lane = lax.broadcasted_iota(jnp.int32, (lc, LANES), 1)
    dsk = dsk_ref[...]
    for g in range(B_GROUPS):
        bg = xbc_act[:, B_WIDTH + B_STATE * g:B_WIDTH + B_STATE * (g + 1)]
        cg = xbc_act[:, B_WIDTH + B_GROUPS * B_STATE + B_STATE * g:B_WIDTH + B_GROUPS * B_STATE + B_STATE * (g + 1)]
        bgb = bg.astype(BF16)
        cgb = cg.astype(BF16)
        cb = _dot_nt(cgb, bgb)
        gs = slice(B_GROUP_W * g, B_GROUP_W * (g + 1))
        state = st_ref[g]
        y_off = _dot(cgb, state.astype(BF16)) * x1_ref[:, gs]
        for pp in range(B_GROUP_W // LANES):
            col = B_GROUP_W * g + LANES * pp
            xp = xbc_act[:, col:col + LANES]
            y_pair = y_off[:, LANES * pp:LANES * (pp + 1)] + dsk[:, col:col + LANES] * xp
            xpb = xp.astype(BF16)
            y_half = []
            for half in range(2):
                head = col // B_HEAD_DIM + half
                seg = cum[:, head:head + 1] - cum_t[head:head + 1, :]
                wgt = cb * jnp.exp2(jnp.where(causal, seg, NEG_BIG)) * dt_t[head:head + 1, :]
                y_half.append(_dot(wgt.astype(BF16), xpb))
            y_pair = y_pair + jnp.where(lane < B_HEAD_DIM, y_half[0], y_half[1])
            yacc_ref[:, col:col + LANES] = y_pair
        st_ref[g] = state * x3[:, gs] + _dot(bg.T.astype(BF16), xd_ref[:, gs])

    for g in range(B_GROUPS):
        gs = slice(B_GROUP_W * g, B_GROUP_W * (g + 1))
        yg = yacc_ref[:, gs] * _silu(z_all[:, gs])
        ms = jnp.mean(yg * yg, axis=-1, keepdims=True)
        out = (yg * lax.rsqrt(ms + EPS) * nrm_ref[:, gs]).astype(y_ref.dtype)
        y_ref[:, gs] = out[0:n_in, :]

    @pl.when(c == n_chunks - 1)
    def _():
        for g in range(B_GROUPS):
            hl_ref[B_GROUP_W * g:B_GROUP_W * (g + 1), :] = st_ref[g].T


def _mamba(u, xc, conv_w, conv_b, dt_bias, a_log, d_skip, ssm_norm, conv_prev, h0, *, n_batch, t_len):
    lc = LANES
    if t_len % lc == 0:
        n_in = lc
    else:
        assert t_len < lc
        n_in = t_len
    nc = t_len // n_in
    has_init = conv_prev is not None
    row = lambda b, c: b * nc + c
    pad1 = lambda v: jnp.concatenate([v.astype(F32), jnp.zeros((LANES - B_HEADS,), F32)]).reshape(1, LANES)
    head_of_col = jnp.arange(B_WIDTH, dtype=jnp.int32) // B_HEAD_DIM
    expand = (jnp.arange(LANES, dtype=jnp.int32)[:, None] == head_of_col[None, :]).astype(BF16)
    dsk_row = jnp.repeat(d_skip.astype(F32), B_HEAD_DIM).reshape(1, B_WIDTH)
    const = lambda shape: pl.BlockSpec(shape, lambda b, c: (0,) * len(shape))
    conv_done = xc is not None
    z_spec = pl.BlockSpec((n_in, B_WIDTH), lambda b, c: (row(b, c), COL_BZ // B_WIDTH))
    dt_spec = pl.BlockSpec((n_in, LANES), lambda b, c: (row(b, c), COL_DT // LANES))
    if conv_done:
        assert not has_init and n_in == lc and t_len % SUBLANES == 0
        tail_row = lambda b, c: ((b + 1) * t_len) // SUBLANES - 1
        in_specs = [z_spec, pl.BlockSpec((n_in, B_CONV_DIM), lambda b, c: (row(b, c), 0)), dt_spec,
                    pl.BlockSpec((SUBLANES, B_CONV_DIM), lambda b, c: (tail_row(b, c), COL_XBC // B_CONV_DIM))]
        args = [u, xc, u, u]
    else:
        in_specs = [z_spec, pl.BlockSpec((n_in, B_CONV_DIM), lambda b, c: (row(b, c), COL_XBC // B_CONV_DIM)), dt_spec,
                    const((B_CONV, B_CONV_DIM)), const((1, B_CONV_DIM))]
        args = [u, u, u, conv_w, conv_b.reshape(1, B_CONV_DIM)]
    in_specs += [const((1, LANES)), const((1, LANES)), const((1, B_WIDTH)), const((1, B_WIDTH)), const((LANES, B_WIDTH))]
    args += [pad1(dt_bias), pad1(a_log), dsk_row, ssm_norm.reshape(1, B_WIDTH), expand]
    if has_init:
        cprev8 = jnp.concatenate([jnp.zeros((n_batch, SUBLANES - (B_CONV - 1), B_CONV_DIM), F32), conv_prev], axis=1)
        in_specs += [pl.BlockSpec((None, SUBLANES, B_CONV_DIM), lambda b, c: (b, 0, 0)),
                     pl.BlockSpec((None, B_WIDTH, B_STATE), lambda b, c: (b, 0, 0))]
        args += [cprev8, h0.reshape(n_batch, B_WIDTH, B_STATE)]
    pad_scratch = []
    if n_in < lc:
        pad_scratch = [pltpu.VMEM((lc, LANES), F32), pltpu.VMEM((lc, B_WIDTH), F32)]
    kern = functools.partial(_mamba_kernel, lc=lc, n_in=n_in, has_init=has_init, conv_done=conv_done)
    y, cst, hl = pl.pallas_call(
        kern,
        grid=(n_batch, nc),
        in_specs=in_specs,
        out_specs=[pl.BlockSpec((n_in, B_WIDTH), lambda b, c: (row(b, c), 0)),
                   pl.BlockSpec((None, B_CONV - 1, B_CONV_DIM), lambda b, c: (b, 0, 0)),
                   pl.BlockSpec((None, B_WIDTH, B_STATE), lambda b, c: (b, 0, 0))],
        out_shape=[jax.ShapeDtypeStruct((n_batch * t_len, B_WIDTH), BF16),
                   jax.ShapeDtypeStruct((n_batch, B_CONV - 1, B_CONV_DIM), F32),
                   jax.ShapeDtypeStruct((n_batch, B_WIDTH, B_STATE), F32)],
        scratch_shapes=[pltpu.VMEM((SUBLANES + lc, B_CONV_DIM), F32),
                        pltpu.VMEM((B_GROUPS, B_STATE, B_GROUP_W), F32),
                        pltpu.VMEM((lc, B_WIDTH), F32),
                        pltpu.VMEM((lc, B_CONV_DIM), F32), pltpu.VMEM((lc, B_WIDTH), F32),
                        pltpu.VMEM((lc, B_WIDTH), BF16)] + pad_scratch,
        compiler_params=_cparams(2),
        name="mamba",
    )(*args)
    return y, cst, hl.reshape(n_batch, B_HEADS, B_HEAD_DIM, B_STATE)


def _mem_kernel(q_ref, z_ref, k_ref, v_ref, o_ref):
    q = q_ref[...].astype(BF16)
    scale = M_HEAD_DIM ** -0.5
    for head in range(M_HEADS):
        hs = slice(M_HEAD_DIM * head, M_HEAD_DIM * (head + 1))
        s = _dot_nt(q[:, hs], k_ref[:, hs].astype(BF16)) * scale
        s_max = jnp.max(s, axis=-1, keepdims=True)
        p = jnp.exp(s - s_max)
        denom = jnp.sum(p, axis=-1, keepdims=True)
        o = _dot(p.astype(BF16), v_ref[:, hs].astype(BF16)) / denom
        o_ref[:, hs] = (o * _silu(z_ref[:, hs])).astype(o_ref.dtype)


def _mem_attend(u, mk, mv, *, n_batch, t_len):
    tq = 512 if t_len % 512 == 0 else t_len
    nq = t_len // tq
    row = lambda b, j: b * nq + j
    return pl.pallas_call(
        _mem_kernel,
        grid=(n_batch, nq),
        in_specs=[pl.BlockSpec((tq, M_WIDTH), lambda b, j: (row(b, j), COL_MQ // M_WIDTH)),
                  pl.BlockSpec((tq, M_WIDTH), lambda b, j: (row(b, j), COL_MZ // M_WIDTH)),
                  pl.BlockSpec((None, N_MEM, M_WIDTH), lambda b, j: (b, 0, 0)),
                  pl.BlockSpec((None, N_MEM, M_WIDTH), lambda b, j: (b, 0, 0))],
        out_specs=pl.BlockSpec((tq, M_WIDTH), lambda b, j: (row(b, j), 0)),
        out_shape=jax.ShapeDtypeStruct((n_batch * t_len, M_WIDTH), BF16),
        compiler_params=_cparams(2),
        name="mem_attend",
    )(u, u, mk, mv)


def _merge_kernel(ya_ref, yb_ref, ym_ref, ga_ref, gb_ref, gm_ref, wa_ref, wb_ref, wm_ref, o_ref):
    merged = _sigmoid(ga_ref[...]) * _dot(ya_ref[...], wa_ref[...])
    merged = merged + _sigmoid(gb_ref[...]) * _dot(yb_ref[...], wb_ref[...])
    merged = merged + _sigmoid(gm_ref[...]) * _dot(ym_ref[...], wm_ref[...])
    o_ref[...] = merged.astype(o_ref.dtype)


def _merge(ya, yb, ym, u, w_pa, w_pb, w_pm, tm):
    n = ya.shape[0]
    rows = lambda width: pl.BlockSpec((tm, width), lambda i: (i, 0))
    gate = lambda k: pl.BlockSpec((tm, D_MODEL), lambda i: (i, COL_GATES // D_MODEL + k))
    weight = lambda width: pl.BlockSpec((width, D_MODEL), lambda i: (0, 0), pipeline_mode=pl.Buffered(1))
    return pl.pallas_call(
        _merge_kernel,
        grid=(n // tm,),
        in_specs=[rows(A_WIDTH), rows(B_WIDTH), rows(M_WIDTH), gate(0), gate(1), gate(2),
                  weight(A_WIDTH), weight(B_WIDTH), weight(M_WIDTH)],
        out_specs=rows(D_MODEL),
        out_shape=jax.ShapeDtypeStruct((n, D_MODEL), BF16),
        compiler_params=_cparams(1),
        name="merge",
    )(ya, yb, ym, u, u, u, w_pa, w_pb, w_pm)


def _final_kernel(m_ref, x_ref, wo_ref, g_ref, o_ref):
    tm = o_ref.shape[0]
    sub = min(tm, FINAL_SUB_ROWS)
    for r in range(tm // sub):
        rs = slice(r * sub, (r + 1) * sub)
        y = x_ref[rs, :] + _dot(m_ref[rs, :], wo_ref[...])
        ms = jnp.mean(y * y, axis=-1, keepdims=True)
        o_ref[rs, :] = y * lax.rsqrt(ms + EPS) * g_ref[...]


def _final(merged, x, w_o, g, tm):
    n = x.shape[0]
    rows = pl.BlockSpec((tm, D_MODEL), lambda i: (i, 0))
    return pl.pallas_call(
        _final_kernel,
        grid=(n // tm,),
        in_specs=[rows, rows,
                  pl.BlockSpec((D_MODEL, D_MODEL), lambda i: (0, 0), pipeline_mode=pl.Buffered(1)),
                  pl.BlockSpec((1, D_MODEL), lambda i: (0, 0))],
        out_specs=rows,
        out_shape=jax.ShapeDtypeStruct((n, D_MODEL), F32),
        compiler_params=_cparams(1),
        name="final",
    )(merged, x, w_o, g.reshape(1, D_MODEL))


def _row_tile(n, pref):
    t = pref
    while n % t:
        t //= 2
    return t


def _pad_keys(t, n_keys):
    pad = n_keys - t.shape[1]
    if pad == 0:
        return t
    return jnp.concatenate([t, jnp.zeros((t.shape[0], pad, t.shape[2]), t.dtype)], axis=1)


def _layer(x, pos0, past_k, past_v, past_ki, conv_prev, h0, mem_k, mem_v, lw, norm_final):
    (norm_in, w_in_packed, conv_w, conv_b, dt_bias, a_log, d_skip, ssm_norm, w_pa, w_pb, w_pm, w_o) = lw
    n_batch, t_len, _ = x.shape
    n = n_batch * t_len
    x2 = x.reshape(n, D_MODEL)
    tm = _row_tile(n, 1024)
    if conv_prev is None and t_len % tm == 0:
        u, xc = _rms_proj(x2, norm_in, w_in_packed, tm, 1024, conv=(conv_w, conv_b, t_len))
    else:
        u, xc = _rms_proj(x2, norm_in, w_in_packed, tm, 1024), None

    k_new = u[:, COL_AK:COL_AK + A_KV_WIDTH].reshape(n_batch, t_len, A_KV_WIDTH)
    v_new = u[:, COL_AV:COL_AV + A_KV_WIDTH].reshape(n_batch, t_len, A_KV_WIDTH)
    kikw_new = u[:, COL_IKW:COL_IKW + LANES].reshape(n_batch, t_len, LANES)
    if past_k is None:
        keys, n_valid_keys = None, t_len
    else:
        n_past = past_k.shape[1]
        past_kikw = jnp.concatenate([past_ki, jnp.zeros((n_batch, n_past, LANES - IDX_DIM), F32)], axis=2)
        n_valid_keys = n_past + t_len
        n_keys = -(-n_valid_keys // KEY_TILE) * KEY_TILE
        keys = (_pad_keys(jnp.concatenate([past_k.reshape(n_batch, n_past, A_KV_WIDTH), k_new], axis=1), n_keys),
                _pad_keys(jnp.concatenate([past_v.reshape(n_batch, n_past, A_KV_WIDTH), v_new], axis=1), n_keys),
                _pad_keys(jnp.concatenate([past_kikw, kikw_new], axis=1), n_keys))
    ya = _dsa(u, keys, n_batch=n_batch, t_len=t_len, n_valid_keys=n_valid_keys, pos0=pos0)

    yb, conv_state, h_last = _mamba(u, xc, conv_w, conv_b, dt_bias, a_log, d_skip, ssm_norm, conv_prev, h0,
                                    n_batch=n_batch, t_len=t_len)
    ym = _mem_attend(u, mem_k, mem_v, n_batch=n_batch, t_len=t_len)
    merged = _merge(ya, yb, ym, u, w_pa, w_pb, w_pm, _row_tile(n, 256))
    y = _final(merged, x2, w_o, norm_final, _row_tile(n, 512)).reshape(n_batch, t_len, D_MODEL)
    return (y, k_new.reshape(n_batch, t_len, A_KV_HEADS, A_HEAD_DIM),
            v_new.reshape(n_batch, t_len, A_KV_HEADS, A_HEAD_DIM),
            kikw_new[:, :, 0:IDX_DIM], conv_state, h_last)


def kernel(x_prompt, x_sample, mem_prompt, cache_attn_k, cache_attn_v, cache_idx_k, state_conv, state_ssm,
           cache_mem_k, cache_mem_v, norm_in, w_in, conv_w, conv_b, dt_bias, a_log, d_skip, ssm_norm,
           norm_mem, w_mem_kv, w_pa, w_pb, w_pm, w_o, norm_final):
    depth = w_in.shape[0]
    assert depth == 1, "the final RMSNorm is fused into the (single) layer"
    bp = x_prompt.shape[0]
    bs = x_sample.shape[0]
    first = lambda t: t.reshape(t.shape[1:])
    lw = (first(norm_in), _pack_w_in(w_in), first(conv_w), first(conv_b), first(dt_bias), first(a_log),
          first(d_skip), first(ssm_norm), first(w_pa).astype(BF16), first(w_pb).astype(BF16),
          first(w_pm).astype(BF16), first(w_o).astype(BF16))

    mem2 = mem_prompt.reshape(bp * N_MEM, D_MODEL)
    mkv = _rms_proj(mem2, first(norm_mem), first(w_mem_kv).astype(BF16), _row_tile(bp * N_MEM, 1024), 1024)
    mk_p = mkv[:, 0:M_WIDTH].reshape(bp, N_MEM, M_WIDTH)
    mv_p = mkv[:, M_WIDTH:2 * M_WIDTH].reshape(bp, N_MEM, M_WIDTH)

    yp, kp, vp, kip, convp, ssmp = _layer(x_prompt, 0, None, None, None, None, None, mk_p, mv_p, lw, norm_final)
    ys, ks, vs, kis, convs, ssms = _layer(
        x_sample, PAST_LEN, first(cache_attn_k), first(cache_attn_v), first(cache_idx_k), first(state_conv),
        first(state_ssm), first(cache_mem_k).reshape(bs, N_MEM, M_WIDTH),
        first(cache_mem_v).reshape(bs, N_MEM, M_WIDTH), lw, norm_final)

    st = lambda t: t[None]
    return (yp, ys, st(kp), st(vp), st(kip), st(convp), st(ssmp),
            st(mk_p.reshape(bp, N_MEM, M_HEADS, M_HEAD_DIM)), st(mv_p.reshape(bp, N_MEM, M_HEADS, M_HEAD_DIM)),
            st(ks), st(vs), st(kis), st(convs), st(ssms))
```

```python
import functools

import jax
import jax.numpy as jnp
from jax import lax
from jax.experimental import pallas as pl
from jax.experimental.pallas import tpu as pltpu

F32 = jnp.float32
BF16 = jnp.bfloat16

D_MODEL = 2048
CHUNK = 64
CHUNK_SHIFT = 6
assert 1 << CHUNK_SHIFT == CHUNK
N_MEM = 256
EPS = 1e-6
PAST_LEN = 1024

A_HEADS = 8
A_KV_HEADS = 2
A_HEAD_DIM = 128
A_GROUP = A_HEADS // A_KV_HEADS
A_WIDTH = A_HEADS * A_HEAD_DIM
A_KV_WIDTH = A_KV_HEADS * A_HEAD_DIM
IDX_HEADS = 16
IDX_DIM = 64
TOPK_MAX = 256
IDX_SCALE = (IDX_DIM * IDX_HEADS) ** -0.5

B_WIDTH = 2048
B_HEAD_DIM = 64
B_HEADS = B_WIDTH // B_HEAD_DIM
B_GROUPS = 4
B_HPG = B_HEADS // B_GROUPS
B_STATE = 128
B_CONV = 4
B_CONV_DIM = B_WIDTH + 2 * B_GROUPS * B_STATE
B_GROUP_W = B_WIDTH // B_GROUPS

M_HEADS = 4
M_HEAD_DIM = 256
M_WIDTH = M_HEADS * M_HEAD_DIM

N_BRANCH = 3
IN_SPLITS = (A_WIDTH, A_KV_WIDTH, A_KV_WIDTH, IDX_HEADS * IDX_DIM, IDX_DIM, IDX_HEADS, A_WIDTH,
             B_WIDTH, B_CONV_DIM, B_HEADS, M_WIDTH, M_WIDTH, N_BRANCH * D_MODEL)
IN_COLS = sum(IN_SPLITS)
(SRC_AQ, SRC_AK, SRC_AV, SRC_IQ, SRC_IK, SRC_IW, SRC_AZ, SRC_BZ, SRC_XBC, SRC_DT, SRC_MQ, SRC_MZ,
 SRC_GATES) = (sum(IN_SPLITS[:i]) for i in range(len(IN_SPLITS)))

LANES = 128
SUBLANES = 8
VMEM_LIMIT_BYTES = 56 * 1024 * 1024

COL_GATES = 0
COL_XBC = 6144
COL_AQ = 9216
COL_IQ = 10240
COL_AZ = 11264
COL_MQ = 12288
COL_MZ = 13312
COL_BZ = 14336
COL_AK = 16384
COL_AV = 16640
COL_IKW = 16896
COL_DT = 17152
PACKED_COLS = 17408

PACK_CHUNK = 256
PACK_CHUNKS_PER_STEP = 4
PACK_ROW_ALIGN = 2 * SUBLANES
PACK_SEGMENTS = ((SRC_GATES, COL_GATES, N_BRANCH * D_MODEL), (SRC_XBC, COL_XBC, B_CONV_DIM),
                 (SRC_AQ, COL_AQ, A_WIDTH), (SRC_IQ, COL_IQ, IDX_HEADS * IDX_DIM), (SRC_AZ, COL_AZ, A_WIDTH),
                 (SRC_MQ, COL_MQ, M_WIDTH), (SRC_MZ, COL_MZ, M_WIDTH), (SRC_BZ, COL_BZ, B_WIDTH),
                 (SRC_AK, COL_AK, A_KV_WIDTH), (SRC_AV, COL_AV, A_KV_WIDTH),
                 (SRC_IK, COL_IKW, PACK_CHUNK), (SRC_DT, COL_DT, PACK_CHUNK))


def _pack_source_table():
    table = [None] * (PACKED_COLS // PACK_CHUNK)
    for src, dst, width in PACK_SEGMENTS:
        assert dst % PACK_CHUNK == 0 and width % PACK_CHUNK == 0 and src % (2 * SUBLANES) == 0
        for off in range(0, width, PACK_CHUNK):
            assert src + off + PACK_CHUNK <= IN_COLS
            table[(dst + off) // PACK_CHUNK] = src + off
    assert all(t is not None for t in table)
    return table


PACK_SOURCE = _pack_source_table()

KEY_TILE = 256
QUERY_TILE = 256
PROJ_ROW_TILE = 1024
PROJ_COL_TILE = 1024
MERGE_ROW_TILE = 256
FINAL_ROW_TILE = 512
MEM_ROW_TILE = 512
PROJ_SUB_ROWS = 512
CONV_SUB_ROWS = 512
FINAL_SUB_ROWS = 256
LOG2_E = 1.4426950408889634
INT_MIN = -2 ** 31
KEY_NEG_INF = INT_MIN + 0x7FFFFF
NEG_BIG = -1e30


def _cparams(n_grid):
    return pltpu.CompilerParams(dimension_semantics=("arbitrary",) * n_grid,
                                vmem_limit_bytes=VMEM_LIMIT_BYTES)


def _sigmoid(z):
    return 0.5 * jnp.tanh(0.5 * z) + 0.5


def _silu(z):
    half = 0.5 * z
    return half * jnp.tanh(half) + half


def _dot(a, b):
    return jnp.dot(a, b, preferred_element_type=F32)


def _dot_nt(a, b):
    return lax.dot_general(a, b, (((1,), (1,)), ((), ())), preferred_element_type=F32)


def _pack_kernel(src_ref, *refs):
    del src_ref
    *wt_refs, o_ref = refs
    for k, wt_ref in enumerate(wt_refs):
        o_ref[:, PACK_CHUNK * k:PACK_CHUNK * (k + 1)] = wt_ref[...].T.astype(o_ref.dtype)


def _pack_w_in(w):
    wt = jnp.transpose(w.reshape(D_MODEL, IN_COLS))
    per_step = PACK_CHUNKS_PER_STEP
    window = lambda k: pl.BlockSpec((pl.Element(PACK_CHUNK), pl.Element(D_MODEL)),
                                    lambda i, src: (src[per_step * i + k] * PACK_ROW_ALIGN, 0))
    grid_spec = pltpu.PrefetchScalarGridSpec(
        num_scalar_prefetch=1,
        grid=(PACKED_COLS // (PACK_CHUNK * per_step),),
        in_specs=[window(k) for k in range(per_step)],
        out_specs=pl.BlockSpec((D_MODEL, PACK_CHUNK * per_step), lambda i, src: (0, i)))
    return pl.pallas_call(
        _pack_kernel,
        grid_spec=grid_spec,
        out_shape=jax.ShapeDtypeStruct((D_MODEL, PACKED_COLS), BF16),
        compiler_params=_cparams(1),
        name="pack_w_in",
    )(jnp.asarray([s // PACK_ROW_ALIGN for s in PACK_SOURCE], jnp.int32), *([wt] * per_step))


def _proj_kernel(*refs, conv):
    if conv is None:
        x_ref, g_ref, w_ref, o_ref, h_ref = refs
    else:
        x_ref, g_ref, w_ref, cw_ref, cb_ref, o_ref, tail_ref, h_ref, pad_ref, halo_ref = refs
    i = pl.program_id(0)
    j = pl.program_id(1)

    @pl.when(j == 0)
    def _():
        x = x_ref[...]
        ms = jnp.mean(x * x, axis=-1, keepdims=True)
        h_ref[...] = (x * lax.rsqrt(ms + EPS) * g_ref[...]).astype(BF16)

    if conv is not None:
        @pl.when(jnp.logical_and(i == 0, j == 0))
        def _():
            halo_ref[...] = jnp.zeros(halo_ref.shape, F32)

    tm = o_ref.shape[0]

    def plain():
        sub = min(tm, PROJ_SUB_ROWS)
        for r in range(tm // sub):
            o_ref[r * sub:(r + 1) * sub, :] = _dot(h_ref[r * sub:(r + 1) * sub, :], w_ref[...])

    if conv is None:
        plain()
        return
    first_tile, n_conv_tiles, tiles_per_seq = conv
    is_conv = jnp.logical_and(j >= first_tile, j < first_tile + n_conv_tiles)
    pl.when(jnp.logical_not(is_conv))(plain)

    @pl.when(is_conv)
    def _():
        c = j - first_tile
        seq_start = (i % tiles_per_seq) == 0
        sub = pad_ref.shape[1] - SUBLANES
        n_sub = tm // sub
        for r in range(n_sub + 1):
            if r < n_sub:
                slot = r % 2
                if r == 0:
                    pad_ref[slot, 0:SUBLANES, :] = jnp.where(seq_start, 0.0, halo_ref[c])
                else:
                    pad_ref[slot, 0:SUBLANES, :] = pad_ref[1 - slot, sub:sub + SUBLANES, :]
                pad_ref[slot, SUBLANES:SUBLANES + sub, :] = _dot(h_ref[r * sub:(r + 1) * sub, :], w_ref[...])
            if r >= 1:
                hist = pad_ref[(r - 1) % 2]
                acc = cb_ref[...] + cw_ref[B_CONV - 1:B_CONV, :] * hist[SUBLANES:, :]
                for k in range(1, B_CONV):
                    shifted = pltpu.roll(hist, k, axis=0)[SUBLANES:, :]
                    acc = acc + cw_ref[B_CONV - 1 - k:B_CONV - k, :] * shifted
                o_ref[(r - 1) * sub:r * sub, :] = _silu(acc)
        last_rows = pad_ref[(n_sub - 1) % 2, sub:sub + SUBLANES, :]
        halo_ref[c] = last_rows
        tail_ref[...] = last_rows


def _rms_proj(x, g, w, tm, tn, conv=None):
    m, d = x.shape
    n = w.shape[1]
    in_specs = [pl.BlockSpec((tm, d), lambda i, j: (i, 0)),
                pl.BlockSpec((1, d), lambda i, j: (0, 0)),
                pl.BlockSpec((d, tn), lambda i, j: (0, j))]
    args = [x, g.reshape(1, d), w]
    out_specs = pl.BlockSpec((tm, tn), lambda i, j: (i, j))
    out_shape = jax.ShapeDtypeStruct((m, n), F32)
    scratch = [pltpu.VMEM((tm, d), BF16)]
    kern_conv = None
    if conv is not None:
        conv_w, conv_b, t_len = conv
        assert COL_XBC % tn == 0 and B_CONV_DIM % tn == 0 and t_len % tm == 0
        first, count = COL_XBC // tn, B_CONV_DIM // tn
        ctile = lambda i, j: jnp.clip(j - first, 0, count - 1)
        in_specs += [pl.BlockSpec((B_CONV, tn), lambda i, j: (0, ctile(i, j))),
                     pl.BlockSpec((1, tn), lambda i, j: (0, ctile(i, j)))]
        args += [conv_w, conv_b.reshape(1, B_CONV_DIM)]
        out_specs = [out_specs, pl.BlockSpec((None, SUBLANES, tn), lambda i, j: (i, 0, ctile(i, j)))]
        out_shape = [out_shape, jax.ShapeDtypeStruct((m // tm, SUBLANES, B_CONV_DIM), F32)]
        scratch += [pltpu.VMEM((2, SUBLANES + min(tm, CONV_SUB_ROWS), tn), F32),
                    pltpu.VMEM((count, SUBLANES, tn), F32)]
        kern_conv = (first, count, t_len // tm)
    return pl.pallas_call(
        functools.partial(_proj_kernel, conv=kern_conv),
        grid=(m // tm, n // tn),
        in_specs=in_specs,
        out_specs=out_specs,
        out_shape=out_shape,
        scratch_shapes=scratch,
        compiler_params=_cparams(2),
        name="rms_proj",
    )(*args)


def _key_to_f32(key):
    bits = jnp.where(key >= 0, key, key ^ jnp.int32(0x7FFFFFFF))
    return pltpu.bitcast(bits, F32)


def _dsa_kernel(q_ref, iq_ref, ikw_ref, az_ref, k_ref, v_ref, kikw_ref, o_ref,
                kb_ref, vb_ref, kie_ref, kio_ref, iqb_ref, wt_ref, qs_ref, sc_ref, bias_ref,
                cnt_ref, gt_ref, m_ref, ans_ref, s_ref, p_ref, acc_ref, stat_ref, *pad_refs,
                tq, tq_in, nq, n_tiles, n_valid_keys, pos0, topk):
    jq = pl.program_id(1)
    kt_sz = KEY_TILE
    topk_f = float(topk)

    @pl.when(jq == 0)
    def _():
        for kt in range(n_tiles):
            tile = slice(kt_sz * kt, kt_sz * (kt + 1))
            kb_ref[kt] = k_ref[tile, :].astype(BF16)
            v_t = v_ref[tile, :].T.astype(BF16)
            for g in range(A_KV_HEADS):
                vb_ref[kt, g] = v_t[A_HEAD_DIM * g:A_HEAD_DIM * (g + 1), :]
            ki = kikw_ref[tile, :]
            lane = lax.broadcasted_iota(jnp.int32, ki.shape, 1)
            kie = jnp.where(lane < IDX_DIM, ki, 0.0)
            kie_ref[kt] = kie.astype(BF16)
            kio_ref[kt] = pltpu.roll(kie, IDX_DIM, axis=1).astype(BF16)

    if tq_in < tq:
        qp_ref, iqp_ref, ikwp_ref, azp_ref = pad_refs
        for dst, src in ((qp_ref, q_ref), (iqp_ref, iq_ref), (ikwp_ref, ikw_ref), (azp_ref, az_ref)):
            dst[...] = jnp.zeros(dst.shape, dst.dtype)
            dst[0:tq_in, :] = src[...]
        q_ref, iq_ref, ikw_ref, az_ref = qp_ref, iqp_ref, ikwp_ref, azp_ref

    iqb_ref[...] = iq_ref[...].astype(BF16)
    wt_ref[...] = ikw_ref[...].T * IDX_SCALE
    q = (q_ref[...] * ((A_HEAD_DIM ** -0.5) * LOG2_E)).astype(BF16)
    for head in range(A_HEADS):
        g, hh = divmod(head, A_GROUP)
        qs_ref[g, hh * tq:(hh + 1) * tq, :] = q[:, A_HEAD_DIM * head:A_HEAD_DIM * (head + 1)]

    q_last = pos0 + (jq + 1) * tq_in - 1
    key_end = jnp.minimum((q_last // CHUNK + 1) * CHUNK, n_valid_keys)
    n_need = (key_end + kt_sz - 1) // kt_sz

    def for_needed_tiles(body):
        def step(kt, carry):
            body(kt)
            return carry

        lax.fori_loop(0, n_need, step, 0)

    krow = lax.broadcasted_iota(jnp.int32, (kt_sz, tq), 0)
    qcol = lax.broadcasted_iota(jnp.int32, (kt_sz, tq), 1)
    qchunk = jnp.right_shift(pos0 + jq * tq_in + qcol, CHUNK_SHIFT)

    def score_tile(kt):
        acc = None
        for pair in range(IDX_HEADS // 2):
            iq_pair = iqb_ref[:, LANES * pair:LANES * (pair + 1)]
            for half, kref in enumerate((kie_ref, kio_ref)):
                head = 2 * pair + half
                d = _dot_nt(kref[kt], iq_pair)
                contrib = jnp.maximum(d, 0.0) * wt_ref[IDX_DIM + head:IDX_DIM + head + 1, :]
                acc = contrib if acc is None else acc + contrib
        kpos = kt * kt_sz + krow
        if n_tiles * kt_sz > n_valid_keys:
            acc = jnp.where(kpos < n_valid_keys, acc, -jnp.inf)
        sc_ref[kt] = jnp.where(jnp.right_shift(kpos, CHUNK_SHIFT) <= qchunk, acc, -jnp.inf)

    for_needed_tiles(score_tile)

    def count_into(ref, kt, hit):
        ref[...] += jnp.sum(hit.reshape(kt_sz // SUBLANES, SUBLANES, tq), axis=0)

    def total(ref):
        return jnp.sum(ref[...], axis=0, keepdims=True)

    def run_bisection(n_static):
        def bisect(i, carry):
            ans, cnt_ans = carry
            cand = ans + jnp.left_shift(jnp.int32(1), jnp.int32(31) - i)
            cand_f = _key_to_f32(jnp.maximum(cand, jnp.int32(KEY_NEG_INF)))
            part = jnp.zeros((SUBLANES, tq), F32)
            for kt in range(n_static):
                hit = jnp.where(sc_ref[kt] >= cand_f, 1.0, 0.0)
                part = part + jnp.sum(hit.reshape(kt_sz // SUBLANES, SUBLANES, tq), axis=0)
            cnt = jnp.sum(part, axis=0, keepdims=True)
            keep = cnt >= topk_f
            return jnp.where(keep, cand, ans), jnp.where(keep, cnt, cnt_ans)

        init = (jnp.full((1, tq), INT_MIN, jnp.int32), jnp.zeros((1, tq), F32))
        ans, cnt_ans = lax.fori_loop(0, 32, bisect, init)
        ans_ref[...] = jnp.broadcast_to(ans, ans_ref.shape)
        cnt_ref[...] = jnp.broadcast_to(cnt_ans, cnt_ref.shape)

    need_of = []
    for j in range(nq):
        end = min(((pos0 + (j + 1) * tq_in - 1) // CHUNK + 1) * CHUNK, n_valid_keys)
        need_of.append(-(-end // kt_sz))

    def per_tile_count(body):
        for n_static in sorted(set(need_of)):
            blocks = [j for j in range(nq) if need_of[j] == n_static]
            if len(blocks) == nq:
                body(n_static)
            else:
                in_range = jnp.logical_and(jq >= blocks[0], jq <= blocks[-1])
                pl.when(in_range)(functools.partial(body, n_static))

    per_tile_count(run_bisection)
    ans = ans_ref[0:1, :]
    thr = _key_to_f32(jnp.maximum(ans, jnp.int32(KEY_NEG_INF)))

    cnt_ge = cnt_ref[0:1, :]
    qcol1 = lax.broadcasted_iota(jnp.int32, (1, tq), 1)
    tie = jnp.where(cnt_ge > topk_f, jnp.where(thr > -jnp.inf, jnp.where(qcol1 < tq_in, 1.0, 0.0), 0.0), 0.0)
    m_ref[...] = jnp.full(m_ref.shape, n_tiles * kt_sz, jnp.int32)

    @pl.when(jnp.max(tie) > 0.0)
    def _():
        nbits = max(1, (n_tiles * kt_sz - 1).bit_length())
        gt_ref[...] = jnp.zeros(gt_ref.shape, F32)
        for_needed_tiles(lambda kt: count_into(gt_ref, kt, jnp.where(sc_ref[kt] > thr, 1.0, 0.0)))
        cnt_gt = total(gt_ref)

        def bisect_pos(i, t):
            cand = t + jnp.left_shift(jnp.int32(1), jnp.int32(nbits - 1) - i)
            cnt_ref[...] = jnp.zeros(cnt_ref.shape, F32)

            def count_below(kt):
                hit = jnp.where(sc_ref[kt] == thr, jnp.where(kt * kt_sz + krow < cand, 1.0, 0.0), 0.0)
                count_into(cnt_ref, kt, hit)

            for_needed_tiles(count_below)
            return jnp.where(cnt_gt + total(cnt_ref) < topk_f, cand, t)

        t_last = lax.fori_loop(0, nbits, bisect_pos, jnp.zeros((1, tq), jnp.int32))
        m_ref[...] = jnp.broadcast_to(t_last, m_ref.shape)

    m_last = m_ref[0:1, :]
    thr_eq = jnp.where(thr > -jnp.inf, thr, jnp.inf)

    def bias_tile(kt):
        sc = sc_ref[kt]
        kpos = kt * kt_sz + krow
        tied = jnp.where(sc == thr_eq, jnp.where(kpos <= m_last, 0.0, NEG_BIG), NEG_BIG)
        bias_ref[kt] = jnp.where(sc > thr, 0.0, tied)

    for_needed_tiles(bias_tile)

    cols = A_GROUP * tq
    groups = range(A_KV_HEADS)
    gsl = [slice(A_HEAD_DIM * g, A_HEAD_DIM * (g + 1)) for g in groups]
    ROW_MAX, ROW_SUM, ROW_ALPHA, ROW_TILE_MAX = 0, 1, 2, 3

    def stat(g, k):
        return stat_ref[SUBLANES * g + k:SUBLANES * g + k + 1, :]

    def set_stat(g, k, v):
        stat_ref[SUBLANES * g + k:SUBLANES * g + k + 1, :] = v

    def produce(kt, slot):
        bias = jnp.concatenate([bias_ref[kt]] * A_GROUP, axis=1)
        for g in groups:
            s = _dot_nt(kb_ref[kt, :, gsl[g]], qs_ref[g]) + bias
            s_ref[slot, g] = s
            set_stat(g, ROW_TILE_MAX + slot, jnp.max(s, axis=0, keepdims=True))

    def consume(kt, slot):
        k_prev = max(kt - 1, 0)
        for g in groups:
            pv = _dot(vb_ref[k_prev, g], p_ref[g])
            acc_ref[g] = acc_ref[g] * stat(g, ROW_ALPHA) + pv
            m_run = stat(g, ROW_MAX)
            m = jnp.maximum(m_run, stat(g, ROW_TILE_MAX + slot))
            alpha = jnp.exp2(m_run - m)
            p = jnp.exp2(s_ref[slot, g] - m)
            set_stat(g, ROW_SUM, alpha * stat(g, ROW_SUM) + jnp.sum(p, axis=0, keepdims=True))
            p_ref[g] = p.astype(BF16)
            set_stat(g, ROW_MAX, m)
            set_stat(g, ROW_ALPHA, alpha)

    acc_ref[...] = jnp.zeros(acc_ref.shape, F32)
    p_ref[...] = jnp.zeros(p_ref.shape, BF16)
    for g in groups:
        set_stat(g, ROW_MAX, jnp.full((1, cols), NEG_BIG, F32))
        set_stat(g, ROW_ALPHA, jnp.ones((1, cols), F32))
        set_stat(g, ROW_SUM, jnp.zeros((1, cols), F32))
    def run_attention(n_static):
        produce(0, 0)
        for t in range(n_static):
            if t + 1 < n_static:
                produce(t + 1, (t + 1) % 2)
            consume(t, t % 2)

    per_tile_count(run_attention)
    for g in groups:
        acc = acc_ref[g] * stat(g, ROW_ALPHA) + _dot(vb_ref[n_need - 1, g], p_ref[g])
        o = (acc / stat(g, ROW_SUM)).T
        for hh in range(A_GROUP):
            hs = slice(A_HEAD_DIM * (A_GROUP * g + hh), A_HEAD_DIM * (A_GROUP * g + hh + 1))
            res = (o[hh * tq:(hh + 1) * tq, :] * _silu(az_ref[:, hs])).astype(o_ref.dtype)
            o_ref[:, hs] = res[0:tq_in, :]


def _dsa(u, keys, *, n_batch, t_len, n_valid_keys, pos0):
    n_keys = t_len if keys is None else keys[0].shape[1]
    assert n_keys % KEY_TILE == 0
    n_tiles = n_keys // KEY_TILE
    topk = min(TOPK_MAX, n_valid_keys // 4)
    if t_len % QUERY_TILE == 0:
        tq = tq_in = QUERY_TILE
    elif t_len % LANES == 0:
        tq = tq_in = LANES
    else:
        tq, tq_in = LANES, t_len
    nq = t_len // tq_in
    row = lambda b, j: b * nq + j
    if keys is None:
        key_args = (u, u, u)
        key_specs = [pl.BlockSpec((n_keys, A_KV_WIDTH), lambda b, j: (b, COL_AK // A_KV_WIDTH)),
                     pl.BlockSpec((n_keys, A_KV_WIDTH), lambda b, j: (b, COL_AV // A_KV_WIDTH)),
                     pl.BlockSpec((n_keys, LANES), lambda b, j: (b, COL_IKW // LANES))]
    else:
        key_args = keys
        key_specs = [pl.BlockSpec((None, n_keys, t.shape[-1]), lambda b, j: (b, 0, 0)) for t in keys]
    pad_scratch = []
    if tq_in < tq:
        pad_scratch = [pltpu.VMEM((tq, A_WIDTH), F32), pltpu.VMEM((tq, IDX_HEADS * IDX_DIM), F32),
                       pltpu.VMEM((tq, LANES), F32), pltpu.VMEM((tq, A_WIDTH), F32)]
    kern = functools.partial(_dsa_kernel, tq=tq, tq_in=tq_in, nq=nq, n_tiles=n_tiles,
                             n_valid_keys=n_valid_keys, pos0=pos0, topk=topk)
    return pl.pallas_call(
        kern,
        grid=(n_batch, nq),
        in_specs=[pl.BlockSpec((tq_in, A_WIDTH), lambda b, j: (row(b, j), COL_AQ // A_WIDTH)),
                  pl.BlockSpec((tq_in, A_WIDTH), lambda b, j: (row(b, j), COL_IQ // A_WIDTH)),
                  pl.BlockSpec((tq_in, LANES), lambda b, j: (row(b, j), COL_IKW // LANES)),
                  pl.BlockSpec((tq_in, A_WIDTH), lambda b, j: (row(b, j), COL_AZ // A_WIDTH)),
                  *key_specs],
        out_specs=pl.BlockSpec((tq_in, A_WIDTH), lambda b, j: (row(b, j), 0)),
        out_shape=jax.ShapeDtypeStruct((n_batch * t_len, A_WIDTH), BF16),
        scratch_shapes=[pltpu.VMEM((n_tiles, KEY_TILE, A_KV_WIDTH), BF16),
                        pltpu.VMEM((n_tiles, A_KV_HEADS, A_HEAD_DIM, KEY_TILE), BF16),
                        pltpu.VMEM((n_tiles, KEY_TILE, LANES), BF16),
                        pltpu.VMEM((n_tiles, KEY_TILE, LANES), BF16),
                        pltpu.VMEM((tq, IDX_HEADS * IDX_DIM), BF16),
                        pltpu.VMEM((LANES, tq), F32),
                        pltpu.VMEM((A_KV_HEADS, A_GROUP * tq, A_HEAD_DIM), BF16),
                        pltpu.VMEM((n_tiles, KEY_TILE, tq), F32),
                        pltpu.VMEM((n_tiles, KEY_TILE, tq), F32),
                        pltpu.VMEM((SUBLANES, tq), F32), pltpu.VMEM((SUBLANES, tq), F32),
                        pltpu.VMEM((SUBLANES, tq), jnp.int32),
                        pltpu.VMEM((SUBLANES, tq), jnp.int32),
                        pltpu.VMEM((2, A_KV_HEADS, KEY_TILE, A_GROUP * tq), F32),
                        pltpu.VMEM((A_KV_HEADS, KEY_TILE, A_GROUP * tq), BF16),
                        pltpu.VMEM((A_KV_HEADS, A_HEAD_DIM, A_GROUP * tq), F32),
                        pltpu.VMEM((A_KV_HEADS * SUBLANES, A_GROUP * tq), F32)] + pad_scratch,
        compiler_params=_cparams(2),
        name="dsa",
    )(u, u, u, u, *key_args)


def _expand_heads(v, e):
    hi = v.astype(BF16)
    lo = (v - hi.astype(F32)).astype(BF16)
    return _dot(hi, e) + _dot(lo, e)


def _mamba_kernel(*refs, lc, n_in, has_init, conv_done):
    z_ref, xbc_ref, dt_ref, *rest = refs
    if conv_done:
        tail_ref, *rest = rest
    else:
        cw_ref, cb_ref, *rest = rest
    dtb_ref, alog_ref, dsk_ref, nrm_ref, e_ref, *rest = rest
    if has_init:
        cprev_ref, h0_ref, *rest = rest
    y_ref, cst_ref, hl_ref, xpad_ref, st_ref, yacc_ref, xbc_ref2, x1_ref, xd_ref, *pad_refs = rest
    c = pl.program_id(1)
    n_chunks = pl.num_programs(1)

    @pl.when(c == 0)
    def _():
        xpad_ref[...] = jnp.zeros(xpad_ref.shape, F32)
        if has_init:
            xpad_ref[0:SUBLANES, :] = cprev_ref[...]
            for g in range(B_GROUPS):
                st_ref[g] = h0_ref[B_GROUP_W * g:B_GROUP_W * (g + 1), :].T
        else:
            st_ref[...] = jnp.zeros(st_ref.shape, F32)

    if conv_done:
        xbc_act = xbc_ref

        @pl.when(c == n_chunks - 1)
        def _():
            cst_ref[...] = tail_ref[SUBLANES - (B_CONV - 1):SUBLANES, :]
    else:
        xpad_ref[SUBLANES:SUBLANES + n_in, :] = xbc_ref[...]
        acc = cb_ref[...] + cw_ref[0:1, :] * xpad_ref[SUBLANES - 3:SUBLANES - 3 + lc, :]
        for j in range(1, B_CONV):
            acc = acc + cw_ref[j:j + 1, :] * xpad_ref[SUBLANES - 3 + j:SUBLANES - 3 + j + lc, :]
        xbc_ref2[...] = _silu(acc)
        xbc_act = xbc_ref2

        @pl.when(c == n_chunks - 1)
        def _():
            cst_ref[...] = xpad_ref[SUBLANES + n_in - 3:SUBLANES + n_in, :]

        xpad_ref[0:SUBLANES, :] = xpad_ref[lc:lc + SUBLANES, :]

    if n_in < lc:
        dtp_ref, zp_ref = pad_refs
        dtp_ref[...] = jnp.zeros(dtp_ref.shape, F32)
        dtp_ref[0:n_in, :] = dt_ref[...]
        zp_ref[...] = jnp.zeros(zp_ref.shape, F32)
        zp_ref[0:n_in, :] = z_ref[...]
        dt_raw = dtp_ref[...]
        z_all = zp_ref
    else:
        dt_raw = dt_ref[...]
        z_all = z_ref
    pre = dt_raw + dtb_ref[...]
    dt = jnp.maximum(pre, 0.0) + jnp.log1p(jnp.exp(-jnp.abs(pre)))
    row = lax.broadcasted_iota(jnp.int32, (lc, LANES), 0)
    dt = jnp.where(lax.broadcasted_iota(jnp.int32, (lc, LANES), 1) < B_HEADS, dt, 0.0)
    if n_in < lc:
        dt = jnp.where(row < n_in, dt, 0.0)
    a = -jnp.exp(alog_ref[...])
    cum = dt * (a * LOG2_E)
    shift = 1
    while shift < lc:
        cum = cum + jnp.where(row >= shift, pltpu.roll(cum, shift, axis=0), 0.0)
        shift *= 2
    cum_t = cum.T
    dt_t = dt.T
    c_last = cum[lc - 1:lc, :]
    e = e_ref[...]
    x1_ref[...] = _expand_heads(jnp.exp2(cum), e)
    x2 = _expand_heads(jnp.exp2(c_last - cum) * dt, e)
    x3 = _expand_heads(jnp.broadcast_to(jnp.exp2(c_last), (SUBLANES, LANES)), e)[0:1, :]

    xd_ref[...] = (xbc_act[:, 0:B_WIDTH] * x2).astype(BF16)
    li = lax.broadcasted_iota(jnp.int32, (lc, lc), 0)
    si = lax.broadcasted_iota(jnp.int32, (lc, lc), 1)
    causal = li >= si
    lane = lax.broadcasted_iota(jnp.int32, (lc, LANES), 1)
    dsk = dsk_ref[...]
    for g in range(B_GROUPS):
        bg = xbc_act[:, B_WIDTH + B_STATE * g:B_WIDTH + B_STATE * (g + 1)]
        cg = xbc_act[:, B_WIDTH + B_GROUPS * B_STATE + B_STATE * g:B_WIDTH + B_GROUPS * B_STATE + B_STATE * (g + 1)]
        bgb = bg.astype(BF16)
        cgb = cg.astype(BF16)
        cb = _dot_nt(cgb, bgb)
        gs = slice(B_GROUP_W * g, B_GROUP_W * (g + 1))
        state = st_ref[g]
        y_off = _dot(cgb, state.astype(BF16)) * x1_ref[:, gs]
        for pp in range(B_GROUP_W // LANES):
            col = B_GROUP_W * g + LANES * pp
            xp = xbc_act[:, col:col + LANES]
            y_pair = y_off[:, LANES * pp:LANES * (pp + 1)] + dsk[:, col:col + LANES] * xp
            xpb = xp.astype(BF16)
            y_half = []
            for half in range(2):
                head = col // B_HEAD_DIM + half
                seg = cum[:, head:head + 1] - cum_t[head:head + 1, :]
                wgt = cb * jnp.exp2(jnp.where(causal, seg, NEG_BIG)) * dt_t[head:head + 1, :]
                y_half.append(_dot(wgt.astype(BF16), xpb))
            y_pair = y_pair + jnp.where(lane < B_HEAD_DIM, y_half[0], y_half[1])
            yacc_ref[:, col:col + LANES] = y_pair
        st_ref[g] = state * x3[:, gs] + _dot(bg.T.astype(BF16), xd_ref[:, gs])

    for g in range(B_GROUPS):
        gs = slice(B_GROUP_W * g, B_GROUP_W * (g + 1))
        yg = yacc_ref[:, gs] * _silu(z_all[:, gs])
        ms = jnp.mean(yg * yg, axis=-1, keepdims=True)
        out = (yg * lax.rsqrt(ms + EPS) * nrm_ref[:, gs]).astype(y_ref.dtype)
        y_ref[:, gs] = out[0:n_in, :]

    @pl.when(c == n_chunks - 1)
    def _():
        for g in range(B_GROUPS):
            hl_ref[B_GROUP_W * g:B_GROUP_W * (g + 1), :] = st_ref[g].T


def _mamba(u, tails, conv_w, conv_b, dt_bias, a_log, d_skip, ssm_norm, conv_prev, h0, *, n_batch, t_len):
    lc = LANES
    if t_len % lc == 0:
        n_in = lc
    else:
        assert t_len < lc
        n_in = t_len
    nc = t_len // n_in
    has_init = conv_prev is not None
    row = lambda b, c: b * nc + c
    pad1 = lambda v: jnp.concatenate([v.astype(F32), jnp.zeros((LANES - B_HEADS,), F32)]).reshape(1, LANES)
    head_of_col = jnp.arange(B_WIDTH, dtype=jnp.int32) // B_HEAD_DIM
    expand = (jnp.arange(LANES, dtype=jnp.int32)[:, None] == head_of_col[None, :]).astype(BF16)
    dsk_row = jnp.repeat(d_skip.astype(F32), B_HEAD_DIM).reshape(1, B_WIDTH)
    const = lambda shape: pl.BlockSpec(shape, lambda b, c: (0,) * len(shape))
    conv_done = tails is not None
    z_spec = pl.BlockSpec((n_in, B_WIDTH), lambda b, c: (row(b, c), COL_BZ // B_WIDTH))
    dt_spec = pl.BlockSpec((n_in, LANES), lambda b, c: (row(b, c), COL_DT // LANES))
    if conv_done:
        assert not has_init and n_in == lc
        tiles_per_seq = tails.shape[0] // n_batch
        in_specs = [z_spec, pl.BlockSpec((n_in, B_CONV_DIM), lambda b, c: (row(b, c), COL_XBC // B_CONV_DIM)), dt_spec,
                    pl.BlockSpec((None, SUBLANES, B_CONV_DIM), lambda b, c: ((b + 1) * tiles_per_seq - 1, 0, 0))]
        args = [u, u, u, tails]
    else:
        in_specs = [z_spec, pl.BlockSpec((n_in, B_CONV_DIM), lambda b, c: (row(b, c), COL_XBC // B_CONV_DIM)), dt_spec,
                    const((B_CONV, B_CONV_DIM)), const((1, B_CONV_DIM))]
        args = [u, u, u, conv_w, conv_b.reshape(1, B_CONV_DIM)]
    in_specs += [const((1, LANES)), const((1, LANES)), const((1, B_WIDTH)), const((1, B_WIDTH)), const((LANES, B_WIDTH))]
    args += [pad1(dt_bias), pad1(a_log), dsk_row, ssm_norm.reshape(1, B_WIDTH), expand]
    if has_init:
        cprev8 = jnp.concatenate([jnp.zeros((n_batch, SUBLANES - (B_CONV - 1), B_CONV_DIM), F32), conv_prev], axis=1)
        in_specs += [pl.BlockSpec((None, SUBLANES, B_CONV_DIM), lambda b, c: (b, 0, 0)),
                     pl.BlockSpec((None, B_WIDTH, B_STATE), lambda b, c: (b, 0, 0))]
        args += [cprev8, h0.reshape(n_batch, B_WIDTH, B_STATE)]
    pad_scratch = []
    if n_in < lc:
        pad_scratch = [pltpu.VMEM((lc, LANES), F32), pltpu.VMEM((lc, B_WIDTH), F32)]
    kern = functools.partial(_mamba_kernel, lc=lc, n_in=n_in, has_init=has_init, conv_done=conv_done)
    y, cst, hl = pl.pallas_call(
        kern,
        grid=(n_batch, nc),
        in_specs=in_specs,
        out_specs=[pl.BlockSpec((n_in, B_WIDTH), lambda b, c: (row(b, c), 0)),
                   pl.BlockSpec((None, B_CONV - 1, B_CONV_DIM), lambda b, c: (b, 0, 0)),
                   pl.BlockSpec((None, B_WIDTH, B_STATE), lambda b, c: (b, 0, 0))],
        out_shape=[jax.ShapeDtypeStruct((n_batch * t_len, B_WIDTH), BF16),
                   jax.ShapeDtypeStruct((n_batch, B_CONV - 1, B_CONV_DIM), F32),
                   jax.ShapeDtypeStruct((n_batch, B_WIDTH, B_STATE), F32)],
        scratch_shapes=[pltpu.VMEM((SUBLANES + lc, B_CONV_DIM), F32),
                        pltpu.VMEM((B_GROUPS, B_STATE, B_GROUP_W), F32),
                        pltpu.VMEM((lc, B_WIDTH), F32),
                        pltpu.VMEM((lc, B_CONV_DIM), F32), pltpu.VMEM((lc, B_WIDTH), F32),
                        pltpu.VMEM((lc, B_WIDTH), BF16)] + pad_scratch,
        compiler_params=_cparams(2),
        name="mamba",
    )(*args)
    return y, cst, hl.reshape(n_batch, B_HEADS, B_HEAD_DIM, B_STATE)


def _mem_kernel(q_ref, z_ref, k_ref, v_ref, o_ref):
    q = q_ref[...].astype(BF16)
    scale = M_HEAD_DIM ** -0.5
    for head in range(M_HEADS):
        hs = slice(M_HEAD_DIM * head, M_HEAD_DIM * (head + 1))
        s = _dot_nt(q[:, hs], k_ref[:, hs].astype(BF16)) * scale
        s_max = jnp.max(s, axis=-1, keepdims=True)
        p = jnp.exp(s - s_max)
        denom = jnp.sum(p, axis=-1, keepdims=True)
        o = _dot(p.astype(BF16), v_ref[:, hs].astype(BF16)) / denom
        o_ref[:, hs] = (o * _silu(z_ref[:, hs])).astype(o_ref.dtype)


def _mem_attend(u, mk, mv, *, n_batch, t_len):
    tq = MEM_ROW_TILE if t_len % MEM_ROW_TILE == 0 else t_len
    nq = t_len // tq
    row = lambda b, j: b * nq + j
    return pl.pallas_call(
        _mem_kernel,
        grid=(n_batch, nq),
        in_specs=[pl.BlockSpec((tq, M_WIDTH), lambda b, j: (row(b, j), COL_MQ // M_WIDTH)),
                  pl.BlockSpec((tq, M_WIDTH), lambda b, j: (row(b, j), COL_MZ // M_WIDTH)),
                  pl.BlockSpec((None, N_MEM, M_WIDTH), lambda b, j: (b, 0, 0)),
                  pl.BlockSpec((None, N_MEM, M_WIDTH), lambda b, j: (b, 0, 0))],
        out_specs=pl.BlockSpec((tq, M_WIDTH), lambda b, j: (row(b, j), 0)),
        out_shape=jax.ShapeDtypeStruct((n_batch * t_len, M_WIDTH), BF16),
        compiler_params=_cparams(2),
        name="mem_attend",
    )(u, u, mk, mv)


def _merge_kernel(ya_ref, yb_ref, ym_ref, ga_ref, gb_ref, gm_ref, wa_ref, wb_ref, wm_ref, o_ref):
    merged = _sigmoid(ga_ref[...]) * _dot(ya_ref[...], wa_ref[...])
    merged = merged + _sigmoid(gb_ref[...]) * _dot(yb_ref[...], wb_ref[...])
    merged = merged + _sigmoid(gm_ref[...]) * _dot(ym_ref[...], wm_ref[...])
    o_ref[...] = merged.astype(o_ref.dtype)


def _merge(ya, yb, ym, u, w_pa, w_pb, w_pm, tm):
    n = ya.shape[0]
    rows = lambda width: pl.BlockSpec((tm, width), lambda i: (i, 0))
    gate = lambda k: pl.BlockSpec((tm, D_MODEL), lambda i: (i, COL_GATES // D_MODEL + k))
    weight = lambda width: pl.BlockSpec((width, D_MODEL), lambda i: (0, 0), pipeline_mode=pl.Buffered(1))
    return pl.pallas_call(
        _merge_kernel,
        grid=(n // tm,),
        in_specs=[rows(A_WIDTH), rows(B_WIDTH), rows(M_WIDTH), gate(0), gate(1), gate(2),
                  weight(A_WIDTH), weight(B_WIDTH), weight(M_WIDTH)],
        out_specs=rows(D_MODEL),
        out_shape=jax.ShapeDtypeStruct((n, D_MODEL), BF16),
        compiler_params=_cparams(1),
        name="merge",
    )(ya, yb, ym, u, u, u, w_pa, w_pb, w_pm)


def _final_kernel(m_ref, x_ref, wo_ref, g_ref, o_ref):
    tm = o_ref.shape[0]
    sub = min(tm, FINAL_SUB_ROWS)
    for r in range(tm // sub):
        rs = slice(r * sub, (r + 1) * sub)
        y = x_ref[rs, :] + _dot(m_ref[rs, :], wo_ref[...])
        ms = jnp.mean(y * y, axis=-1, keepdims=True)
        o_ref[rs, :] = y * lax.rsqrt(ms + EPS) * g_ref[...]


def _final(merged, x, w_o, g, tm):
    n = x.shape[0]
    rows = pl.BlockSpec((tm, D_MODEL), lambda i: (i, 0))
    return pl.pallas_call(
        _final_kernel,
        grid=(n // tm,),
        in_specs=[rows, rows,
                  pl.BlockSpec((D_MODEL, D_MODEL), lambda i: (0, 0), pipeline_mode=pl.Buffered(1)),
                  pl.BlockSpec((1, D_MODEL), lambda i: (0, 0))],
        out_specs=rows,
        out_shape=jax.ShapeDtypeStruct((n, D_MODEL), F32),
        compiler_params=_cparams(1),
        name="final",
    )(merged, x, w_o, g.reshape(1, D_MODEL))


def _row_tile(n, pref):
    t = pref
    while n % t:
        t //= 2
    return t


def _pad_keys(t, n_keys):
    pad = n_keys - t.shape[1]
    if pad == 0:
        return t
    return jnp.concatenate([t, jnp.zeros((t.shape[0], pad, t.shape[2]), t.dtype)], axis=1)


def _layer(x, pos0, past_k, past_v, past_ki, conv_prev, h0, mem_k, mem_v, lw, norm_final):
    (norm_in, w_in_packed, conv_w, conv_b, dt_bias, a_log, d_skip, ssm_norm, w_pa, w_pb, w_pm, w_o) = lw
    n_batch, t_len, _ = x.shape
    n = n_batch * t_len
    x2 = x.reshape(n, D_MODEL)
    tm = _row_tile(n, PROJ_ROW_TILE)
    if conv_prev is None and t_len % tm == 0:
        u, tails = _rms_proj(x2, norm_in, w_in_packed, tm, PROJ_COL_TILE, conv=(conv_w, conv_b, t_len))
    else:
        u, tails = _rms_proj(x2, norm_in, w_in_packed, tm, PROJ_COL_TILE), None

    k_new = u[:, COL_AK:COL_AK + A_KV_WIDTH].reshape(n_batch, t_len, A_KV_WIDTH)
    v_new = u[:, COL_AV:COL_AV + A_KV_WIDTH].reshape(n_batch, t_len, A_KV_WIDTH)
    kikw_new = u[:, COL_IKW:COL_IKW + LANES].reshape(n_batch, t_len, LANES)
    if past_k is None:
        keys, n_valid_keys = None, t_len
    else:
        n_past = past_k.shape[1]
        past_kikw = jnp.concatenate([past_ki, jnp.zeros((n_batch, n_past, LANES - IDX_DIM), F32)], axis=2)
        n_valid_keys = n_past + t_len
        n_keys = -(-n_valid_keys // KEY_TILE) * KEY_TILE
        keys = (_pad_keys(jnp.concatenate([past_k.reshape(n_batch, n_past, A_KV_WIDTH), k_new], axis=1), n_keys),
                _pad_keys(jnp.concatenate([past_v.reshape(n_batch, n_past, A_KV_WIDTH), v_new], axis=1), n_keys),
                _pad_keys(jnp.concatenate([past_kikw, kikw_new], axis=1), n_keys))
    ya = _dsa(u, keys, n_batch=n_batch, t_len=t_len, n_valid_keys=n_valid_keys, pos0=pos0)

    yb, conv_state, h_last = _mamba(u, tails, conv_w, conv_b, dt_bias, a_log, d_skip, ssm_norm, conv_prev, h0,
                                    n_batch=n_batch, t_len=t_len)
    ym = _mem_attend(u, mem_k, mem_v, n_batch=n_batch, t_len=t_len)
    merged = _merge(ya, yb, ym, u, w_pa, w_pb, w_pm, _row_tile(n, MERGE_ROW_TILE))
    y = _final(merged, x2, w_o, norm_final, _row_tile(n, FINAL_ROW_TILE)).reshape(n_batch, t_len, D_MODEL)
    return (y, k_new.reshape(n_batch, t_len, A_KV_HEADS, A_HEAD_DIM),
            v_new.reshape(n_batch, t_len, A_KV_HEADS, A_HEAD_DIM),
            kikw_new[:, :, 0:IDX_DIM], conv_state, h_last)


def kernel(x_prompt, x_sample, mem_prompt, cache_attn_k, cache_attn_v, cache_idx_k, state_conv, state_ssm,
           cache_mem_k, cache_mem_v, norm_in, w_in, conv_w, conv_b, dt_bias, a_log, d_skip, ssm_norm,
           norm_mem, w_mem_kv, w_pa, w_pb, w_pm, w_o, norm_final):
    depth = w_in.shape[0]
    assert depth == 1, "the final RMSNorm is fused into the (single) layer"
    bp = x_prompt.shape[0]
    bs = x_sample.shape[0]
    first = lambda t: t.reshape(t.shape[1:])
    lw = (first(norm_in), _pack_w_in(w_in), first(conv_w), first(conv_b), first(dt_bias), first(a_log),
          first(d_skip), first(ssm_norm), first(w_pa).astype(BF16), first(w_pb).astype(BF16),
          first(w_pm).astype(BF16), first(w_o).astype(BF16))

    mem2 = mem_prompt.reshape(bp * N_MEM, D_MODEL)
    mkv = _rms_proj(mem2, first(norm_mem), first(w_mem_kv).astype(BF16), _row_tile(bp * N_MEM, PROJ_ROW_TILE),
                    PROJ_COL_TILE)
    mk_p = mkv[:, 0:M_WIDTH].reshape(bp, N_MEM, M_WIDTH)
    mv_p = mkv[:, M_WIDTH:2 * M_WIDTH].reshape(bp, N_MEM, M_WIDTH)

    yp, kp, vp, kip, convp, ssmp = _layer(x_prompt, 0, None, None, None, None, None, mk_p, mv_p, lw, norm_final)
    ys, ks, vs, kis, convs, ssms = _layer(
        x_sample, PAST_LEN, first(cache_attn_k), first(cache_attn_v), first(cache_idx_k), first(state_conv),
        first(state_ssm), first(cache_mem_k).reshape(bs, N_MEM, M_WIDTH),
        first(cache_mem_v).reshape(bs, N_MEM, M_WIDTH), lw, norm_final)

    st = lambda t: t[None]
    return (yp, ys, st(kp), st(vp), st(kip), st(convp), st(ssmp),
            st(mk_p.reshape(bp, N_MEM, M_HEADS, M_HEAD_DIM)), st(mv_p.reshape(bp, N_MEM, M_HEADS, M_HEAD_DIM)),
            st(ks), st(vs), st(kis), st(convs), st(ssms))
```

```python
import functools

import jax
import jax.numpy as jnp
from jax import lax
from jax.experimental import pallas as pl
from jax.experimental.pallas import tpu as pltpu

F32 = jnp.float32
BF16 = jnp.bfloat16

D_MODEL = 2048
CHUNK = 64
CHUNK_SHIFT = 6
assert 1 << CHUNK_SHIFT == CHUNK
N_MEM = 256
EPS = 1e-6
PAST_LEN = 1024

A_HEADS = 8
A_KV_HEADS = 2
A_HEAD_DIM = 128
A_GROUP = A_HEADS // A_KV_HEADS
A_WIDTH = A_HEADS * A_HEAD_DIM
A_KV_WIDTH = A_KV_HEADS * A_HEAD_DIM
IDX_HEADS = 16
IDX_DIM = 64
TOPK_MAX = 256
IDX_SCALE = (IDX_DIM * IDX_HEADS) ** -0.5

B_WIDTH = 2048
B_HEAD_DIM = 64
B_HEADS = B_WIDTH // B_HEAD_DIM
B_GROUPS = 4
B_HPG = B_HEADS // B_GROUPS
B_STATE = 128
B_CONV = 4
B_CONV_DIM = B_WIDTH + 2 * B_GROUPS * B_STATE
B_GROUP_W = B_WIDTH // B_GROUPS

M_HEADS = 4
M_HEAD_DIM = 256
M_WIDTH = M_HEADS * M_HEAD_DIM

N_BRANCH = 3
IN_SPLITS = (A_WIDTH, A_KV_WIDTH, A_KV_WIDTH, IDX_HEADS * IDX_DIM, IDX_DIM, IDX_HEADS, A_WIDTH,
             B_WIDTH, B_CONV_DIM, B_HEADS, M_WIDTH, M_WIDTH, N_BRANCH * D_MODEL)
IN_COLS = sum(IN_SPLITS)
(SRC_AQ, SRC_AK, SRC_AV, SRC_IQ, SRC_IK, SRC_IW, SRC_AZ, SRC_BZ, SRC_XBC, SRC_DT, SRC_MQ, SRC_MZ,
 SRC_GATES) = (sum(IN_SPLITS[:i]) for i in range(len(IN_SPLITS)))

LANES = 128
SUBLANES = 8
VMEM_LIMIT_BYTES = 56 * 1024 * 1024

COL_GATES = 0
COL_XBC = 6144
COL_AQ = 9216
COL_IQ = 10240
COL_AZ = 11264
COL_MQ = 12288
COL_MZ = 13312
COL_BZ = 14336
COL_AK = 16384
COL_AV = 16640
COL_IKW = 16896
COL_DT = 17152
PACKED_COLS = 17408

PACK_CHUNK = 256
PACK_CHUNKS_PER_STEP = 4
PACK_ROW_ALIGN = 2 * SUBLANES
PACK_SEGMENTS = ((SRC_GATES, COL_GATES, N_BRANCH * D_MODEL), (SRC_XBC, COL_XBC, B_CONV_DIM),
                 (SRC_AQ, COL_AQ, A_WIDTH), (SRC_IQ, COL_IQ, IDX_HEADS * IDX_DIM), (SRC_AZ, COL_AZ, A_WIDTH),
                 (SRC_MQ, COL_MQ, M_WIDTH), (SRC_MZ, COL_MZ, M_WIDTH), (SRC_BZ, COL_BZ, B_WIDTH),
                 (SRC_AK, COL_AK, A_KV_WIDTH), (SRC_AV, COL_AV, A_KV_WIDTH),
                 (SRC_IK, COL_IKW, PACK_CHUNK), (SRC_DT, COL_DT, PACK_CHUNK))


def _pack_source_table():
    table = [None] * (PACKED_COLS // PACK_CHUNK)
    for src, dst, width in PACK_SEGMENTS:
        assert dst % PACK_CHUNK == 0 and width % PACK_CHUNK == 0 and src % (2 * SUBLANES) == 0
        for off in range(0, width, PACK_CHUNK):
            assert src + off + PACK_CHUNK <= IN_COLS
            table[(dst + off) // PACK_CHUNK] = src + off
    assert all(t is not None for t in table)
    return table


PACK_SOURCE = _pack_source_table()

KEY_TILE = 256
QUERY_TILE = 256
PROJ_ROW_TILE = 1024
PROJ_COL_TILE = 1024
MERGE_ROW_TILE = 256
FINAL_ROW_TILE = 512
MEM_ROW_TILE = 512
PROJ_SUB_ROWS = 512
CONV_SUB_ROWS = 512
FINAL_SUB_ROWS = 256
LOG2_E = 1.4426950408889634
INT_MIN = -2 ** 31
KEY_NEG_INF = INT_MIN + 0x7FFFFF
NEG_BIG = -1e30


def _cparams(n_grid):
    return pltpu.CompilerParams(dimension_semantics=("arbitrary",) * n_grid,
                                vmem_limit_bytes=VMEM_LIMIT_BYTES)


def _sigmoid(z):
    return 0.5 * jnp.tanh(0.5 * z) + 0.5


def _silu(z):
    half = 0.5 * z
    return half * jnp.tanh(half) + half


def _dot(a, b):
    return jnp.dot(a, b, preferred_element_type=F32)


def _dot_nt(a, b):
    return lax.dot_general(a, b, (((1,), (1,)), ((), ())), preferred_element_type=F32)


def _pack_kernel(src_ref, *refs):
    del src_ref
    *wt_refs, o_ref = refs
    for k, wt_ref in enumerate(wt_refs):
        o_ref[:, PACK_CHUNK * k:PACK_CHUNK * (k + 1)] = wt_ref[...].T.astype(o_ref.dtype)


def _pack_w_in(w):
    wt = jnp.transpose(w.reshape(D_MODEL, IN_COLS))
    per_step = PACK_CHUNKS_PER_STEP
    window = lambda k: pl.BlockSpec((pl.Element(PACK_CHUNK), pl.Element(D_MODEL)),
                                    lambda i, src: (src[per_step * i + k] * PACK_ROW_ALIGN, 0))
    grid_spec = pltpu.PrefetchScalarGridSpec(
        num_scalar_prefetch=1,
        grid=(PACKED_COLS // (PACK_CHUNK * per_step),),
        in_specs=[window(k) for k in range(per_step)],
        out_specs=pl.BlockSpec((D_MODEL, PACK_CHUNK * per_step), lambda i, src: (0, i)))
    return pl.pallas_call(
        _pack_kernel,
        grid_spec=grid_spec,
        out_shape=jax.ShapeDtypeStruct((D_MODEL, PACKED_COLS), BF16),
        compiler_params=_cparams(1),
        name="pack_w_in",
    )(jnp.asarray([s // PACK_ROW_ALIGN for s in PACK_SOURCE], jnp.int32), *([wt] * per_step))


def _proj_kernel(*refs, conv):
    if conv is None:
        x_ref, g_ref, w_ref, o_ref, h_ref = refs
    else:
        x_ref, g_ref, w_ref, cw_ref, cb_ref, o_ref, tail_ref, h_ref, pad_ref, halo_ref = refs
    i = pl.program_id(0)
    j = pl.program_id(1)

    @pl.when(j == 0)
    def _():
        x = x_ref[...]
        ms = jnp.mean(x * x, axis=-1, keepdims=True)
        h_ref[...] = (x * lax.rsqrt(ms + EPS) * g_ref[...]).astype(BF16)

    if conv is not None:
        @pl.when(jnp.logical_and(i == 0, j == 0))
        def _():
            halo_ref[...] = jnp.zeros(halo_ref.shape, F32)

    tm = o_ref.shape[0]

    def plain():
        sub = min(tm, PROJ_SUB_ROWS)
        for r in range(tm // sub):
            o_ref[r * sub:(r + 1) * sub, :] = _dot(h_ref[r * sub:(r + 1) * sub, :], w_ref[...])

    if conv is None:
        plain()
        return
    first_tile, n_conv_tiles, tiles_per_seq = conv
    is_conv = jnp.logical_and(j >= first_tile, j < first_tile + n_conv_tiles)
    pl.when(jnp.logical_not(is_conv))(plain)

    @pl.when(is_conv)
    def _():
        c = j - first_tile
        seq_start = (i % tiles_per_seq) == 0
        sub = pad_ref.shape[1] - SUBLANES
        n_sub = tm // sub
        for r in range(n_sub + 1):
            if r < n_sub:
                slot = r % 2
                if r == 0:
                    pad_ref[slot, 0:SUBLANES, :] = jnp.where(seq_start, 0.0, halo_ref[c])
                else:
                    pad_ref[slot, 0:SUBLANES, :] = pad_ref[1 - slot, sub:sub + SUBLANES, :]
                pad_ref[slot, SUBLANES:SUBLANES + sub, :] = _dot(h_ref[r * sub:(r + 1) * sub, :], w_ref[...])
            if r >= 1:
                hist = pad_ref[(r - 1) % 2]
                acc = cb_ref[...] + cw_ref[B_CONV - 1:B_CONV, :] * hist[SUBLANES:, :]
                for k in range(1, B_CONV):
                    shifted = pltpu.roll(hist, k, axis=0)[SUBLANES:, :]
                    acc = acc + cw_ref[B_CONV - 1 - k:B_CONV - k, :] * shifted
                o_ref[(r - 1) * sub:r * sub, :] = _silu(acc)
        last_rows = pad_ref[(n_sub - 1) % 2, sub:sub + SUBLANES, :]
        halo_ref[c] = last_rows
        tail_ref[...] = last_rows


def _rms_proj(x, g, w, tm, tn, conv=None):
    m, d = x.shape
    n = w.shape[1]
    in_specs = [pl.BlockSpec((tm, d), lambda i, j: (i, 0)),
                pl.BlockSpec((1, d), lambda i, j: (0, 0)),
                pl.BlockSpec((d, tn), lambda i, j: (0, j))]
    args = [x, g.reshape(1, d), w]
    out_specs = pl.BlockSpec((tm, tn), lambda i, j: (i, j))
    out_shape = jax.ShapeDtypeStruct((m, n), F32)
    scratch = [pltpu.VMEM((tm, d), BF16)]
    kern_conv = None
    if conv is not None:
        conv_w, conv_b, t_len = conv
        assert COL_XBC % tn == 0 and B_CONV_DIM % tn == 0 and t_len % tm == 0
        first, count = COL_XBC // tn, B_CONV_DIM // tn
        ctile = lambda i, j: jnp.clip(j - first, 0, count - 1)
        in_specs += [pl.BlockSpec((B_CONV, tn), lambda i, j: (0, ctile(i, j))),
                     pl.BlockSpec((1, tn), lambda i, j: (0, ctile(i, j)))]
        args += [conv_w, conv_b.reshape(1, B_CONV_DIM)]
        out_specs = [out_specs, pl.BlockSpec((None, SUBLANES, tn), lambda i, j: (i, 0, ctile(i, j)))]
        out_shape = [out_shape, jax.ShapeDtypeStruct((m // tm, SUBLANES, B_CONV_DIM), F32)]
        scratch += [pltpu.VMEM((2, SUBLANES + min(tm, CONV_SUB_ROWS), tn), F32),
                    pltpu.VMEM((count, SUBLANES, tn), F32)]
        kern_conv = (first, count, t_len // tm)
    return pl.pallas_call(
        functools.partial(_proj_kernel, conv=kern_conv),
        grid=(m // tm, n // tn),
        in_specs=in_specs,
        out_specs=out_specs,
        out_shape=out_shape,
        scratch_shapes=scratch,
        compiler_params=_cparams(2),
        name="rms_proj",
    )(*args)


def _key_to_f32(key):
    bits = jnp.where(key >= 0, key, key ^ jnp.int32(0x7FFFFFFF))
    return pltpu.bitcast(bits, F32)


def _dsa_kernel(q_ref, iq_ref, ikw_ref, az_ref, k_ref, v_ref, kikw_ref, o_ref,
                kb_ref, vb_ref, kie_ref, kio_ref, iqb_ref, wt_ref, qs_ref, sc_ref, bias_ref,
                cnt_ref, gt_ref, m_ref, ans_ref, s_ref, p_ref, acc_ref, stat_ref, *pad_refs,
                tq, tq_in, nq, n_tiles, n_valid_keys, pos0, topk):
    jq = pl.program_id(1)
    kt_sz = KEY_TILE
    topk_f = float(topk)

    @pl.when(jq == 0)
    def _():
        for kt in range(n_tiles):
            tile = slice(kt_sz * kt, kt_sz * (kt + 1))
            kb_ref[kt] = k_ref[tile, :].astype(BF16)
            v_t = v_ref[tile, :].T.astype(BF16)
            for g in range(A_KV_HEADS):
                vb_ref[kt, g] = v_t[A_HEAD_DIM * g:A_HEAD_DIM * (g + 1), :]
            ki = kikw_ref[tile, :]
            lane = lax.broadcasted_iota(jnp.int32, ki.shape, 1)
            kie = jnp.where(lane < IDX_DIM, ki, 0.0)
            kie_ref[kt] = kie.astype(BF16)
            kio_ref[kt] = pltpu.roll(kie, IDX_DIM, axis=1).astype(BF16)

    if tq_in < tq:
        qp_ref, iqp_ref, ikwp_ref, azp_ref = pad_refs
        for dst, src in ((qp_ref, q_ref), (iqp_ref, iq_ref), (ikwp_ref, ikw_ref), (azp_ref, az_ref)):
            dst[...] = jnp.zeros(dst.shape, dst.dtype)
            dst[0:tq_in, :] = src[...]
        q_ref, iq_ref, ikw_ref, az_ref = qp_ref, iqp_ref, ikwp_ref, azp_ref

    iqb_ref[...] = iq_ref[...].astype(BF16)
    wt_ref[...] = ikw_ref[...].T * IDX_SCALE
    q = (q_ref[...] * ((A_HEAD_DIM ** -0.5) * LOG2_E)).astype(BF16)
    for head in range(A_HEADS):
        g, hh = divmod(head, A_GROUP)
        qs_ref[g, hh * tq:(hh + 1) * tq, :] = q[:, A_HEAD_DIM * head:A_HEAD_DIM * (head + 1)]

    q_last = pos0 + (jq + 1) * tq_in - 1
    key_end = jnp.minimum((q_last // CHUNK + 1) * CHUNK, n_valid_keys)
    n_need = (key_end + kt_sz - 1) // kt_sz

    def for_needed_tiles(body):
        if nq == 1:
            for kt in range(n_tiles):
                body(kt)
            return

        def step(kt, carry):
            body(kt)
            return carry

        lax.fori_loop(0, n_need, step, 0)

    krow = lax.broadcasted_iota(jnp.int32, (kt_sz, tq), 0)
    qcol = lax.broadcasted_iota(jnp.int32, (kt_sz, tq), 1)
    qchunk = jnp.right_shift(pos0 + jq * tq_in + qcol, CHUNK_SHIFT)

    def score_tile(kt):
        acc = None
        for pair in range(IDX_HEADS // 2):
            iq_pair = iqb_ref[:, LANES * pair:LANES * (pair + 1)]
            for half, kref in enumerate((kie_ref, kio_ref)):
                head = 2 * pair + half
                d = _dot_nt(kref[kt], iq_pair)
                contrib = jnp.maximum(d, 0.0) * wt_ref[IDX_DIM + head:IDX_DIM + head + 1, :]
                acc = contrib if acc is None else acc + contrib
        kpos = kt * kt_sz + krow
        if n_tiles * kt_sz > n_valid_keys:
            acc = jnp.where(kpos < n_valid_keys, acc, -jnp.inf)
        sc_ref[kt] = jnp.where(jnp.right_shift(kpos, CHUNK_SHIFT) <= qchunk, acc, -jnp.inf)

    for_needed_tiles(score_tile)

    def count_into(ref, kt, hit):
        ref[...] += jnp.sum(hit.reshape(kt_sz // SUBLANES, SUBLANES, tq), axis=0)

    def total(ref):
        return jnp.sum(ref[...], axis=0, keepdims=True)

    def run_bisection(n_static):
        def bisect(i, carry):
            ans, cnt_ans = carry
            cand = ans + jnp.left_shift(jnp.int32(1), jnp.int32(31) - i)
            cand_f = _key_to_f32(jnp.maximum(cand, jnp.int32(KEY_NEG_INF)))
            part = jnp.zeros((SUBLANES, tq), F32)
            for kt in range(n_static):
                hit = jnp.where(sc_ref[kt] >= cand_f, 1.0, 0.0)
                part = part + jnp.sum(hit.reshape(kt_sz // SUBLANES, SUBLANES, tq), axis=0)
            cnt = jnp.sum(part, axis=0, keepdims=True)
            keep = cnt >= topk_f
            return jnp.where(keep, cand, ans), jnp.where(keep, cnt, cnt_ans)

        init = (jnp.full((1, tq), INT_MIN, jnp.int32), jnp.zeros((1, tq), F32))
        ans, cnt_ans = lax.fori_loop(0, 32, bisect, init)
        ans_ref[...] = jnp.broadcast_to(ans, ans_ref.shape)
        cnt_ref[...] = jnp.broadcast_to(cnt_ans, cnt_ref.shape)

    need_of = []
    for j in range(nq):
        end = min(((pos0 + (j + 1) * tq_in - 1) // CHUNK + 1) * CHUNK, n_valid_keys)
        need_of.append(-(-end // kt_sz))
    assert nq > 1 or need_of[0] == n_tiles

    def per_tile_count(body):
        for n_static in sorted(set(need_of)):
            blocks = [j for j in range(nq) if need_of[j] == n_static]
            if len(blocks) == nq:
                body(n_static)
            else:
                in_range = jnp.logical_and(jq >= blocks[0], jq <= blocks[-1])
                pl.when(in_range)(functools.partial(body, n_static))

    per_tile_count(run_bisection)
    ans = ans_ref[0:1, :]
    thr = _key_to_f32(jnp.maximum(ans, jnp.int32(KEY_NEG_INF)))

    cnt_ge = cnt_ref[0:1, :]
    qcol1 = lax.broadcasted_iota(jnp.int32, (1, tq), 1)
    tie = jnp.where(cnt_ge > topk_f, jnp.where(thr > -jnp.inf, jnp.where(qcol1 < tq_in, 1.0, 0.0), 0.0), 0.0)
    m_ref[...] = jnp.full(m_ref.shape, n_tiles * kt_sz, jnp.int32)

    @pl.when(jnp.max(tie) > 0.0)
    def _():
        nbits = max(1, (n_tiles * kt_sz - 1).bit_length())
        gt_ref[...] = jnp.zeros(gt_ref.shape, F32)
        for_needed_tiles(lambda kt: count_into(gt_ref, kt, jnp.where(sc_ref[kt] > thr, 1.0, 0.0)))
        cnt_gt = total(gt_ref)

        def bisect_pos(i, t):
            cand = t + jnp.left_shift(jnp.int32(1), jnp.int32(nbits - 1) - i)
            cnt_ref[...] = jnp.zeros(cnt_ref.shape, F32)

            def count_below(kt):
                hit = jnp.where(sc_ref[kt] == thr, jnp.where(kt * kt_sz + krow < cand, 1.0, 0.0), 0.0)
                count_into(cnt_ref, kt, hit)

            for_needed_tiles(count_below)
            return jnp.where(cnt_gt + total(cnt_ref) < topk_f, cand, t)

        t_last = lax.fori_loop(0, nbits, bisect_pos, jnp.zeros((1, tq), jnp.int32))
        m_ref[...] = jnp.broadcast_to(t_last, m_ref.shape)

    m_last = m_ref[0:1, :]
    thr_eq = jnp.where(thr > -jnp.inf, thr, jnp.inf)

    def bias_tile(kt):
        sc = sc_ref[kt]
        kpos = kt * kt_sz + krow
        tied = jnp.where(sc == thr_eq, jnp.where(kpos <= m_last, 0.0, NEG_BIG), NEG_BIG)
        bias_ref[kt] = jnp.where(sc > thr, 0.0, tied)

    for_needed_tiles(bias_tile)

    cols = A_GROUP * tq
    groups = range(A_KV_HEADS)
    gsl = [slice(A_HEAD_DIM * g, A_HEAD_DIM * (g + 1)) for g in groups]
    ROW_MAX, ROW_SUM, ROW_ALPHA, ROW_TILE_MAX = 0, 1, 2, 3

    def stat(g, k):
        return stat_ref[SUBLANES * g + k:SUBLANES * g + k + 1, :]

    def set_stat(g, k, v):
        stat_ref[SUBLANES * g + k:SUBLANES * g + k + 1, :] = v

    def produce(kt, slot):
        bias = jnp.concatenate([bias_ref[kt]] * A_GROUP, axis=1)
        for g in groups:
            s = _dot_nt(kb_ref[kt, :, gsl[g]], qs_ref[g]) + bias
            s_ref[slot, g] = s
            set_stat(g, ROW_TILE_MAX + slot, jnp.max(s, axis=0, keepdims=True))

    def consume(kt, slot):
        k_prev = max(kt - 1, 0)
        for g in groups:
            pv = _dot(vb_ref[k_prev, g], p_ref[g])
            acc_ref[g] = acc_ref[g] * stat(g, ROW_ALPHA) + pv
            m_run = stat(g, ROW_MAX)
            m = jnp.maximum(m_run, stat(g, ROW_TILE_MAX + slot))
            alpha = jnp.exp2(m_run - m)
            p = jnp.exp2(s_ref[slot, g] - m)
            set_stat(g, ROW_SUM, alpha * stat(g, ROW_SUM) + jnp.sum(p, axis=0, keepdims=True))
            p_ref[g] = p.astype(BF16)
            set_stat(g, ROW_MAX, m)
            set_stat(g, ROW_ALPHA, alpha)

    acc_ref[...] = jnp.zeros(acc_ref.shape, F32)
    p_ref[...] = jnp.zeros(p_ref.shape, BF16)
    for g in groups:
        set_stat(g, ROW_MAX, jnp.full((1, cols), NEG_BIG, F32))
        set_stat(g, ROW_ALPHA, jnp.ones((1, cols), F32))
        set_stat(g, ROW_SUM, jnp.zeros((1, cols), F32))
    def run_attention(n_static):
        produce(0, 0)
        for t in range(n_static):
            if t + 1 < n_static:
                produce(t + 1, (t + 1) % 2)
            consume(t, t % 2)

    per_tile_count(run_attention)
    for g in groups:
        acc = acc_ref[g] * stat(g, ROW_ALPHA) + _dot(vb_ref[n_need - 1, g], p_ref[g])
        o = (acc / stat(g, ROW_SUM)).T
        for hh in range(A_GROUP):
            hs = slice(A_HEAD_DIM * (A_GROUP * g + hh), A_HEAD_DIM * (A_GROUP * g + hh + 1))
            res = (o[hh * tq:(hh + 1) * tq, :] * _silu(az_ref[:, hs])).astype(o_ref.dtype)
            o_ref[:, hs] = res[0:tq_in, :]


def _dsa(u, keys, *, n_batch, t_len, n_valid_keys, pos0):
    n_keys = t_len if keys is None else keys[0].shape[1]
    assert n_keys % KEY_TILE == 0
    n_tiles = n_keys // KEY_TILE
    topk = min(TOPK_MAX, n_valid_keys // 4)
    if t_len % QUERY_TILE == 0:
        tq = tq_in = QUERY_TILE
    elif t_len % LANES == 0:
        tq = tq_in = LANES
    else:
        tq, tq_in = LANES, t_len
    nq = t_len // tq_in
    row = lambda b, j: b * nq + j
    if keys is None:
        key_args = (u, u, u)
        key_specs = [pl.BlockSpec((n_keys, A_KV_WIDTH), lambda b, j: (b, COL_AK // A_KV_WIDTH)),
                     pl.BlockSpec((n_keys, A_KV_WIDTH), lambda b, j: (b, COL_AV // A_KV_WIDTH)),
                     pl.BlockSpec((n_keys, LANES), lambda b, j: (b, COL_IKW // LANES))]
    else:
        key_args = keys
        key_specs = [pl.BlockSpec((None, n_keys, t.shape[-1]), lambda b, j: (b, 0, 0)) for t in keys]
    pad_scratch = []
    if tq_in < tq:
        pad_scratch = [pltpu.VMEM((tq, A_WIDTH), F32), pltpu.VMEM((tq, IDX_HEADS * IDX_DIM), F32),
                       pltpu.VMEM((tq, LANES), F32), pltpu.VMEM((tq, A_WIDTH), F32)]
    kern = functools.partial(_dsa_kernel, tq=tq, tq_in=tq_in, nq=nq, n_tiles=n_tiles,
                             n_valid_keys=n_valid_keys, pos0=pos0, topk=topk)
    return pl.pallas_call(
        kern,
        grid=(n_batch, nq),
        in_specs=[pl.BlockSpec((tq_in, A_WIDTH), lambda b, j: (row(b, j), COL_AQ // A_WIDTH)),
                  pl.BlockSpec((tq_in, A_WIDTH), lambda b, j: (row(b, j), COL_IQ // A_WIDTH)),
                  pl.BlockSpec((tq_in, LANES), lambda b, j: (row(b, j), COL_IKW // LANES)),
                  pl.BlockSpec((tq_in, A_WIDTH), lambda b, j: (row(b, j), COL_AZ // A_WIDTH)),
                  *key_specs],
        out_specs=pl.BlockSpec((tq_in, A_WIDTH), lambda b, j: (row(b, j), 0)),
        out_shape=jax.ShapeDtypeStruct((n_batch * t_len, A_WIDTH), BF16),
        scratch_shapes=[pltpu.VMEM((n_tiles, KEY_TILE, A_KV_WIDTH), BF16),
                        pltpu.VMEM((n_tiles, A_KV_HEADS, A_HEAD_DIM, KEY_TILE), BF16),
                        pltpu.VMEM((n_tiles, KEY_TILE, LANES), BF16),
                        pltpu.VMEM((n_tiles, KEY_TILE, LANES), BF16),
                        pltpu.VMEM((tq, IDX_HEADS * IDX_DIM), BF16),
                        pltpu.VMEM((LANES, tq), F32),
                        pltpu.VMEM((A_KV_HEADS, A_GROUP * tq, A_HEAD_DIM), BF16),
                        pltpu.VMEM((n_tiles, KEY_TILE, tq), F32),
                        pltpu.VMEM((n_tiles, KEY_TILE, tq), F32),
                        pltpu.VMEM((SUBLANES, tq), F32), pltpu.VMEM((SUBLANES, tq), F32),
                        pltpu.VMEM((SUBLANES, tq), jnp.int32),
                        pltpu.VMEM((SUBLANES, tq), jnp.int32),
                        pltpu.VMEM((2, A_KV_HEADS, KEY_TILE, A_GROUP * tq), F32),
                        pltpu.VMEM((A_KV_HEADS, KEY_TILE, A_GROUP * tq), BF16),
                        pltpu.VMEM((A_KV_HEADS, A_HEAD_DIM, A_GROUP * tq), F32),
                        pltpu.VMEM((A_KV_HEADS * SUBLANES, A_GROUP * tq), F32)] + pad_scratch,
        compiler_params=_cparams(2),
        name="dsa",
    )(u, u, u, u, *key_args)


def _expand_heads(v, e):
    hi = v.astype(BF16)
    lo = (v - hi.astype(F32)).astype(BF16)
    return _dot(hi, e) + _dot(lo, e)


def _mamba_kernel(*refs, lc, n_in, has_init, conv_done):
    z_ref, xbc_ref, dt_ref, *rest = refs
    if conv_done:
        tail_ref, *rest = rest
    else:
        cw_ref, cb_ref, *rest = rest
    dtb_ref, alog_ref, dsk_ref, nrm_ref, e_ref, *rest = rest
    if has_init:
        cprev_ref, h0_ref, *rest = rest
    y_ref, cst_ref, hl_ref, xpad_ref, st_ref, yacc_ref, xbc_ref2, x1_ref, xd_ref, *pad_refs = rest
    c = pl.program_id(1)
    n_chunks = pl.num_programs(1)

    @pl.when(c == 0)
    def _():
        xpad_ref[...] = jnp.zeros(xpad_ref.shape, F32)
        if has_init:
            xpad_ref[0:SUBLANES, :] = cprev_ref[...]
            for g in range(B_GROUPS):
                st_ref[g] = h0_ref[B_GROUP_W * g:B_GROUP_W * (g + 1), :].T
        else:
            st_ref[...] = jnp.zeros(st_ref.shape, F32)

    if conv_done:
        xbc_act = xbc_ref

        @pl.when(c == n_chunks - 1)
        def _():
            cst_ref[...] = tail_ref[SUBLANES - (B_CONV - 1):SUBLANES, :]
    else:
        xpad_ref[SUBLANES:SUBLANES + n_in, :] = xbc_ref[...]
        acc = cb_ref[...] + cw_ref[0:1, :] * xpad_ref[SUBLANES - 3:SUBLANES - 3 + lc, :]
        for j in range(1, B_CONV):
            acc = acc + cw_ref[j:j + 1, :] * xpad_ref[SUBLANES - 3 + j:SUBLANES - 3 + j + lc, :]
        xbc_ref2[...] = _silu(acc)
        xbc_act = xbc_ref2

        @pl.when(c == n_chunks - 1)
        def _():
            cst_ref[...] = xpad_ref[SUBLANES + n_in - 3:SUBLANES + n_in, :]

        xpad_ref[0:SUBLANES, :] = xpad_ref[lc:lc + SUBLANES, :]

    if n_in < lc:
        dtp_ref, zp_ref = pad_refs
        dtp_ref[...] = jnp.zeros(dtp_ref.shape, F32)
        dtp_ref[0:n_in, :] = dt_ref[...]
        zp_ref[...] = jnp.zeros(zp_ref.shape, F32)
        zp_ref[0:n_in, :] = z_ref[...]
        dt_raw = dtp_ref[...]
        z_all = zp_ref
    else:
        dt_raw = dt_ref[...]
        z_all = z_ref
    pre = dt_raw + dtb_ref[...]
    dt = jnp.maximum(pre, 0.0) + jnp.log1p(jnp.exp(-jnp.abs(pre)))
    row = lax.broadcasted_iota(jnp.int32, (lc, LANES), 0)
    dt = jnp.where(lax.broadcasted_iota(jnp.int32, (lc, LANES), 1) < B_HEADS, dt, 0.0)
    if n_in < lc:
        dt = jnp.where(row < n_in, dt, 0.0)
    a = -jnp.exp(alog_ref[...])
    cum = dt * (a * LOG2_E)
    shift = 1
    while shift < lc:
        cum = cum + jnp.where(row >= shift, pltpu.roll(cum, shift, axis=0), 0.0)
        shift *= 2
    cum_t = cum.T
    dt_t = dt.T
    c_last = cum[lc - 1:lc, :]
    e = e_ref[...]
    x1_ref[...] = _expand_heads(jnp.exp2(cum), e)
    x2 = _expand_heads(jnp.exp2(c_last - cum) * dt, e)
    x3 = _expand_heads(jnp.broadcast_to(jnp.exp2(c_last), (SUBLANES, LANES)), e)[0:1, :]

    xd_ref[...] = (xbc_act[:, 0:B_WIDTH] * x2).astype(BF16)
    li = lax.broadcasted_iota(jnp.int32, (lc, lc), 0)
    si = lax.broadcasted_iota(jnp.int32, (lc, lc), 1)
    causal = li >= si
    lane = lax.broadcasted_iota(jnp.int32, (lc, LANES), 1)
    dsk = dsk_ref[...]
    for g in range(B_GROUPS):
        bg = xbc_act[:, B_WIDTH + B_STATE * g:B_WIDTH + B_STATE * (g + 1)]
        cg = xbc_act[:, B_WIDTH + B_GROUPS * B_STATE + B_STATE * g:B_WIDTH + B_GROUPS * B_STATE + B_STATE * (g + 1)]
        bgb = bg.astype(BF16)
        cgb = cg.astype(BF16)
        cb = _dot_nt(cgb, bgb)
        gs = slice(B_GROUP_W * g, B_GROUP_W * (g + 1))
        state = st_ref[g]
        y_off = _dot(cgb, state.astype(BF16)) * x1_ref[:, gs]
        for pp in range(B_GROUP_W // LANES):
            col = B_GROUP_W * g + LANES * pp
            xp = xbc_act[:, col:col + LANES]
            y_pair = y_off[:, LANES * pp:LANES * (pp + 1)] + dsk[:, col:col + LANES] * xp
            xpb = xp.astype(BF16)
            y_half = []
            for half in range(2):
                head = col // B_HEAD_DIM + half
                seg = cum[:, head:head + 1] - cum_t[head:head + 1, :]
                wgt = cb * jnp.exp2(jnp.where(causal, seg, NEG_BIG)) * dt_t[head:head + 1, :]
                y_half.append(_dot(wgt.astype(BF16), xpb))
            y_pair = y_pair + jnp.where(lane < B_HEAD_DIM, y_half[0], y_half[1])
            yacc_ref[:, col:col + LANES] = y_pair
        st_ref[g] = state * x3[:, gs] + _dot(bg.T.astype(BF16), xd_ref[:, gs])

    for g in range(B_GROUPS):
        gs = slice(B_GROUP_W * g, B_GROUP_W * (g + 1))
        yg = yacc_ref[:, gs] * _silu(z_all[:, gs])
        ms = jnp.mean(yg * yg, axis=-1, keepdims=True)
        out = (yg * lax.rsqrt(ms + EPS) * nrm_ref[:, gs]).astype(y_ref.dtype)
        y_ref[:, gs] = out[0:n_in, :]

    @pl.when(c == n_chunks - 1)
    def _():
        for g in range(B_GROUPS):
            hl_ref[B_GROUP_W * g:B_GROUP_W * (g + 1), :] = st_ref[g].T


def _mamba(u, tails, conv_w, conv_b, dt_bias, a_log, d_skip, ssm_norm, conv_prev, h0, *, n_batch, t_len):
    lc = LANES
    if t_len % lc == 0:
        n_in = lc
    else:
        assert t_len < lc
        n_in = t_len
    nc = t_len // n_in
    has_init = conv_prev is not None
    row = lambda b, c: b * nc + c
    pad1 = lambda v: jnp.concatenate([v.astype(F32), jnp.zeros((LANES - B_HEADS,), F32)]).reshape(1, LANES)
    head_of_col = jnp.arange(B_WIDTH, dtype=jnp.int32) // B_HEAD_DIM
    expand = (jnp.arange(LANES, dtype=jnp.int32)[:, None] == head_of_col[None, :]).astype(BF16)
    dsk_row = jnp.repeat(d_skip.astype(F32), B_HEAD_DIM).reshape(1, B_WIDTH)
    const = lambda shape: pl.BlockSpec(shape, lambda b, c: (0,) * len(shape))
    conv_done = tails is not None
    z_spec = pl.BlockSpec((n_in, B_WIDTH), lambda b, c: (row(b, c), COL_BZ // B_WIDTH))
    dt_spec = pl.BlockSpec((n_in, LANES), lambda b, c: (row(b, c), COL_DT // LANES))
    if conv_done:
        assert not has_init and n_in == lc
        tiles_per_seq = tails.shape[0] // n_batch
        in_specs = [z_spec, pl.BlockSpec((n_in, B_CONV_DIM), lambda b, c: (row(b, c), COL_XBC // B_CONV_DIM)), dt_spec,
                    pl.BlockSpec((None, SUBLANES, B_CONV_DIM), lambda b, c: ((b + 1) * tiles_per_seq - 1, 0, 0))]
        args = [u, u, u, tails]
    else:
        in_specs = [z_spec, pl.BlockSpec((n_in, B_CONV_DIM), lambda b, c: (row(b, c), COL_XBC // B_CONV_DIM)), dt_spec,
                    const((B_CONV, B_CONV_DIM)), const((1, B_CONV_DIM))]
        args = [u, u, u, conv_w, conv_b.reshape(1, B_CONV_DIM)]
    in_specs += [const((1, LANES)), const((1, LANES)), const((1, B_WIDTH)), const((1, B_WIDTH)), const((LANES, B_WIDTH))]
    args += [pad1(dt_bias), pad1(a_log), dsk_row, ssm_norm.reshape(1, B_WIDTH), expand]
    if has_init:
        cprev8 = jnp.concatenate([jnp.zeros((n_batch, SUBLANES - (B_CONV - 1), B_CONV_DIM), F32), conv_prev], axis=1)
        in_specs += [pl.BlockSpec((None, SUBLANES, B_CONV_DIM), lambda b, c: (b, 0, 0)),
                     pl.BlockSpec((None, B_WIDTH, B_STATE), lambda b, c: (b, 0, 0))]
        args += [cprev8, h0.reshape(n_batch, B_WIDTH, B_STATE)]
    pad_scratch = []
    if n_in < lc:
        pad_scratch = [pltpu.VMEM((lc, LANES), F32), pltpu.VMEM((lc, B_WIDTH), F32)]
    kern = functools.partial(_mamba_kernel, lc=lc, n_in=n_in, has_init=has_init, conv_done=conv_done)
    y, cst, hl = pl.pallas_call(
        kern,
        grid=(n_batch, nc),
        in_specs=in_specs,
        out_specs=[pl.BlockSpec((n_in, B_WIDTH), lambda b, c: (row(b, c), 0)),
                   pl.BlockSpec((None, B_CONV - 1, B_CONV_DIM), lambda b, c: (b, 0, 0)),
                   pl.BlockSpec((None, B_WIDTH, B_STATE), lambda b, c: (b, 0, 0))],
        out_shape=[jax.ShapeDtypeStruct((n_batch * t_len, B_WIDTH), BF16),
                   jax.ShapeDtypeStruct((n_batch, B_CONV - 1, B_CONV_DIM), F32),
                   jax.ShapeDtypeStruct((n_batch, B_WIDTH, B_STATE), F32)],
        scratch_shapes=[pltpu.VMEM((SUBLANES + lc, B_CONV_DIM), F32),
                        pltpu.VMEM((B_GROUPS, B_STATE, B_GROUP_W), F32),
                        pltpu.VMEM((lc, B_WIDTH), F32),
                        pltpu.VMEM((lc, B_CONV_DIM), F32), pltpu.VMEM((lc, B_WIDTH), F32),
                        pltpu.VMEM((lc, B_WIDTH), BF16)] + pad_scratch,
        compiler_params=_cparams(2),
        name="mamba",
    )(*args)
    return y, cst, hl.reshape(n_batch, B_HEADS, B_HEAD_DIM, B_STATE)


def _mem_kernel(q_ref, z_ref, k_ref, v_ref, o_ref):
    q = q_ref[...].astype(BF16)
    scale = M_HEAD_DIM ** -0.5
    for head in range(M_HEADS):
        hs = slice(M_HEAD_DIM * head, M_HEAD_DIM * (head + 1))
        s = _dot_nt(q[:, hs], k_ref[:, hs].astype(BF16)) * scale
        s_max = jnp.max(s, axis=-1, keepdims=True)
        p = jnp.exp(s - s_max)
        denom = jnp.sum(p, axis=-1, keepdims=True)
        o = _dot(p.astype(BF16), v_ref[:, hs].astype(BF16)) / denom
        o_ref[:, hs] = (o * _silu(z_ref[:, hs])).astype(o_ref.dtype)


def _mem_attend(u, mk, mv, *, n_batch, t_len):
    tq = MEM_ROW_TILE if t_len % MEM_ROW_TILE == 0 else t_len
    nq = t_len // tq
    row = lambda b, j: b * nq + j
    return pl.pallas_call(
        _mem_kernel,
        grid=(n_batch, nq),
        in_specs=[pl.BlockSpec((tq, M_WIDTH), lambda b, j: (row(b, j), COL_MQ // M_WIDTH)),
                  pl.BlockSpec((tq, M_WIDTH), lambda b, j: (row(b, j), COL_MZ // M_WIDTH)),
                  pl.BlockSpec((None, N_MEM, M_WIDTH), lambda b, j: (b, 0, 0)),
                  pl.BlockSpec((None, N_MEM, M_WIDTH), lambda b, j: (b, 0, 0))],
        out_specs=pl.BlockSpec((tq, M_WIDTH), lambda b, j: (row(b, j), 0)),
        out_shape=jax.ShapeDtypeStruct((n_batch * t_len, M_WIDTH), BF16),
        compiler_params=_cparams(2),
        name="mem_attend",
    )(u, u, mk, mv)


def _merge_kernel(ya_ref, yb_ref, ym_ref, ga_ref, gb_ref, gm_ref, wa_ref, wb_ref, wm_ref, o_ref):
    merged = _sigmoid(ga_ref[...]) * _dot(ya_ref[...], wa_ref[...])
    merged = merged + _sigmoid(gb_ref[...]) * _dot(yb_ref[...], wb_ref[...])
    merged = merged + _sigmoid(gm_ref[...]) * _dot(ym_ref[...], wm_ref[...])
    o_ref[...] = merged.astype(o_ref.dtype)


def _merge(ya, yb, ym, u, w_pa, w_pb, w_pm, tm):
    n = ya.shape[0]
    rows = lambda width: pl.BlockSpec((tm, width), lambda i: (i, 0))
    gate = lambda k: pl.BlockSpec((tm, D_MODEL), lambda i: (i, COL_GATES // D_MODEL + k))
    weight = lambda width: pl.BlockSpec((width, D_MODEL), lambda i: (0, 0), pipeline_mode=pl.Buffered(1))
    return pl.pallas_call(
        _merge_kernel,
        grid=(n // tm,),
        in_specs=[rows(A_WIDTH), rows(B_WIDTH), rows(M_WIDTH), gate(0), gate(1), gate(2),
                  weight(A_WIDTH), weight(B_WIDTH), weight(M_WIDTH)],
        out_specs=rows(D_MODEL),
        out_shape=jax.ShapeDtypeStruct((n, D_MODEL), BF16),
        compiler_params=_cparams(1),
        name="merge",
    )(ya, yb, ym, u, u, u, w_pa, w_pb, w_pm)


def _final_kernel(m_ref, x_ref, wo_ref, g_ref, o_ref):
    tm = o_ref.shape[0]
    sub = min(tm, FINAL_SUB_ROWS)
    for r in range(tm // sub):
        rs = slice(r * sub, (r + 1) * sub)
        y = x_ref[rs, :] + _dot(m_ref[rs, :], wo_ref[...])
        ms = jnp.mean(y * y, axis=-1, keepdims=True)
        o_ref[rs, :] = y * lax.rsqrt(ms + EPS) * g_ref[...]


def _final(merged, x, w_o, g, tm):
    n = x.shape[0]
    rows = pl.BlockSpec((tm, D_MODEL), lambda i: (i, 0))
    return pl.pallas_call(
        _final_kernel,
        grid=(n // tm,),
        in_specs=[rows, rows,
                  pl.BlockSpec((D_MODEL, D_MODEL), lambda i: (0, 0), pipeline_mode=pl.Buffered(1)),
                  pl.BlockSpec((1, D_MODEL), lambda i: (0, 0))],
        out_specs=rows,
        out_shape=jax.ShapeDtypeStruct((n, D_MODEL), F32),
        compiler_params=_cparams(1),
        name="final",
    )(merged, x, w_o, g.reshape(1, D_MODEL))


def _row_tile(n, pref):
    t = pref
    while n % t:
        t //= 2
    return t


def _pad_keys(t, n_keys):
    pad = n_keys - t.shape[1]
    if pad == 0:
        return t
    return jnp.concatenate([t, jnp.zeros((t.shape[0], pad, t.shape[2]), t.dtype)], axis=1)


def _layer(x, pos0, past_k, past_v, past_ki, conv_prev, h0, mem_k, mem_v, lw, norm_final):
    (norm_in, w_in_packed, conv_w, conv_b, dt_bias, a_log, d_skip, ssm_norm, w_pa, w_pb, w_pm, w_o) = lw
    n_batch, t_len, _ = x.shape
    n = n_batch * t_len
    x2 = x.reshape(n, D_MODEL)
    tm = _row_tile(n, PROJ_ROW_TILE)
    if conv_prev is None and t_len % tm == 0:
        u, tails = _rms_proj(x2, norm_in, w_in_packed, tm, PROJ_COL_TILE, conv=(conv_w, conv_b, t_len))
    else:
        u, tails = _rms_proj(x2, norm_in, w_in_packed, tm, PROJ_COL_TILE), None

    k_new = u[:, COL_AK:COL_AK + A_KV_WIDTH].reshape(n_batch, t_len, A_KV_WIDTH)
    v_new = u[:, COL_AV:COL_AV + A_KV_WIDTH].reshape(n_batch, t_len, A_KV_WIDTH)
    kikw_new = u[:, COL_IKW:COL_IKW + LANES].reshape(n_batch, t_len, LANES)
    if past_k is None:
        keys, n_valid_keys = None, t_len
    else:
        n_past = past_k.shape[1]
        past_kikw = jnp.concatenate([past_ki, jnp.zeros((n_batch, n_past, LANES - IDX_DIM), F32)], axis=2)
        n_valid_keys = n_past + t_len
        n_keys = -(-n_valid_keys // KEY_TILE) * KEY_TILE
        keys = (_pad_keys(jnp.concatenate([past_k.reshape(n_batch, n_past, A_KV_WIDTH), k_new], axis=1), n_keys),
                _pad_keys(jnp.concatenate([past_v.reshape(n_batch, n_past, A_KV_WIDTH), v_new], axis=1), n_keys),
                _pad_keys(jnp.concatenate([past_kikw, kikw_new], axis=1), n_keys))
    ya = _dsa(u, keys, n_batch=n_batch, t_len=t_len, n_valid_keys=n_valid_keys, pos0=pos0)

    yb, conv_state, h_last = _mamba(u, tails, conv_w, conv_b, dt_bias, a_log, d_skip, ssm_norm, conv_prev, h0,
                                    n_batch=n_batch, t_len=t_len)
    ym = _mem_attend(u, mem_k, mem_v, n_batch=n_batch, t_len=t_len)
    merged = _merge(ya, yb, ym, u, w_pa, w_pb, w_pm, _row_tile(n, MERGE_ROW_TILE))
    y = _final(merged, x2, w_o, norm_final, _row_tile(n, FINAL_ROW_TILE)).reshape(n_batch, t_len, D_MODEL)
    return (y, k_new.reshape(n_batch, t_len, A_KV_HEADS, A_HEAD_DIM),
            v_new.reshape(n_batch, t_len, A_KV_HEADS, A_HEAD_DIM),
            kikw_new[:, :, 0:IDX_DIM], conv_state, h_last)


def kernel(x_prompt, x_sample, mem_prompt, cache_attn_k, cache_attn_v, cache_idx_k, state_conv, state_ssm,
           cache_mem_k, cache_mem_v, norm_in, w_in, conv_w, conv_b, dt_bias, a_log, d_skip, ssm_norm,
           norm_mem, w_mem_kv, w_pa, w_pb, w_pm, w_o, norm_final):
    depth = w_in.shape[0]
    assert depth == 1, "the final RMSNorm is fused into the (single) layer"
    bp = x_prompt.shape[0]
    bs = x_sample.shape[0]
    first = lambda t: t.reshape(t.shape[1:])
    lw = (first(norm_in), _pack_w_in(w_in), first(conv_w), first(conv_b), first(dt_bias), first(a_log),
          first(d_skip), first(ssm_norm), first(w_pa).astype(BF16), first(w_pb).astype(BF16),
          first(w_pm).astype(BF16), first(w_o).astype(BF16))

    mem2 = mem_prompt.reshape(bp * N_MEM, D_MODEL)
    mkv = _rms_proj(mem2, first(norm_mem), first(w_mem_kv).astype(BF16), _row_tile(bp * N_MEM, PROJ_ROW_TILE),
                    PROJ_COL_TILE)
    mk_p = mkv[:, 0:M_WIDTH].reshape(bp, N_MEM, M_WIDTH)
    mv_p = mkv[:, M_WIDTH:2 * M_WIDTH].reshape(bp, N_MEM, M_WIDTH)

    yp, kp, vp, kip, convp, ssmp = _layer(x_prompt, 0, None, None, None, None, None, mk_p, mv_p, lw, norm_final)
    ys, ks, vs, kis, convs, ssms = _layer(
        x_sample, PAST_LEN, first(cache_attn_k), first(cache_attn_v), first(cache_idx_k), first(state_conv),
        first(state_ssm), first(cache_mem_k).reshape(bs, N_MEM, M_WIDTH),
        first(cache_mem_v).reshape(bs, N_MEM, M_WIDTH), lw, norm_final)

    st = lambda t: t[None]
    return (yp, ys, st(kp), st(vp), st(kip), st(convp), st(ssmp),
            st(mk_p.reshape(bp, N_MEM, M_HEADS, M_HEAD_DIM)), st(mv_p.reshape(bp, N_MEM, M_HEADS, M_HEAD_DIM)),
            st(ks), st(vs), st(kis), st(convs), st(ssms))
```

```python
import functools

import jax
import jax.numpy as jnp
from jax import lax
from jax.experimental import pallas as pl
from jax.experimental.pallas import tpu as pltpu

F32 = jnp.float32
BF16 = jnp.bfloat16

D_MODEL = 2048
CHUNK = 64
CHUNK_SHIFT = 6
assert 1 << CHUNK_SHIFT == CHUNK
N_MEM = 256
EPS = 1e-6
PAST_LEN = 1024

A_HEADS = 8
A_KV_HEADS = 2
A_HEAD_DIM = 128
A_GROUP = A_HEADS // A_KV_HEADS
A_WIDTH = A_HEADS * A_HEAD_DIM
A_KV_WIDTH = A_KV_HEADS * A_HEAD_DIM
IDX_HEADS = 16
IDX_DIM = 64
TOPK_MAX = 256
IDX_SCALE = (IDX_DIM * IDX_HEADS) ** -0.5

B_WIDTH = 2048
B_HEAD_DIM = 64
B_HEADS = B_WIDTH // B_HEAD_DIM
B_GROUPS = 4
B_HPG = B_HEADS // B_GROUPS
B_STATE = 128
B_CONV = 4
B_CONV_DIM = B_WIDTH + 2 * B_GROUPS * B_STATE
B_GROUP_W = B_WIDTH // B_GROUPS

M_HEADS = 4
M_HEAD_DIM = 256
M_WIDTH = M_HEADS * M_HEAD_DIM

N_BRANCH = 3
IN_SPLITS = (A_WIDTH, A_KV_WIDTH, A_KV_WIDTH, IDX_HEADS * IDX_DIM, IDX_DIM, IDX_HEADS, A_WIDTH,
             B_WIDTH, B_CONV_DIM, B_HEADS, M_WIDTH, M_WIDTH, N_BRANCH * D_MODEL)
IN_COLS = sum(IN_SPLITS)
(SRC_AQ, SRC_AK, SRC_AV, SRC_IQ, SRC_IK, SRC_IW, SRC_AZ, SRC_BZ, SRC_XBC, SRC_DT, SRC_MQ, SRC_MZ,
 SRC_GATES) = (sum(IN_SPLITS[:i]) for i in range(len(IN_SPLITS)))

LANES = 128
SUBLANES = 8
VMEM_LIMIT_BYTES = 56 * 1024 * 1024

COL_GATES = 0
COL_XBC = 6144
COL_AQ = 9216
COL_IQ = 10240
COL_AZ = 11264
COL_MQ = 12288
COL_MZ = 13312
COL_BZ = 14336
COL_AK = 16384
COL_AV = 16640
COL_IKW = 16896
COL_DT = 17152
PACKED_COLS = 17408

PACK_CHUNK = 256
PACK_CHUNKS_PER_STEP = 4
PACK_ROW_ALIGN = 2 * SUBLANES
PACK_SEGMENTS = ((SRC_GATES, COL_GATES, N_BRANCH * D_MODEL), (SRC_XBC, COL_XBC, B_CONV_DIM),
                 (SRC_AQ, COL_AQ, A_WIDTH), (SRC_IQ, COL_IQ, IDX_HEADS * IDX_DIM), (SRC_AZ, COL_AZ, A_WIDTH),
                 (SRC_MQ, COL_MQ, M_WIDTH), (SRC_MZ, COL_MZ, M_WIDTH), (SRC_BZ, COL_BZ, B_WIDTH),
                 (SRC_AK, COL_AK, A_KV_WIDTH), (SRC_AV, COL_AV, A_KV_WIDTH),
                 (SRC_IK, COL_IKW, PACK_CHUNK), (SRC_DT, COL_DT, PACK_CHUNK))


def _pack_source_table():
    table = [None] * (PACKED_COLS // PACK_CHUNK)
    for src, dst, width in PACK_SEGMENTS:
        assert dst % PACK_CHUNK == 0 and width % PACK_CHUNK == 0 and src % (2 * SUBLANES) == 0
        for off in range(0, width, PACK_CHUNK):
            assert src + off + PACK_CHUNK <= IN_COLS
            table[(dst + off) // PACK_CHUNK] = src + off
    assert all(t is not None for t in table)
    return table


PACK_SOURCE = _pack_source_table()

KEY_TILE = 256
QUERY_TILE = 256
PROJ_ROW_TILE = 1024
PROJ_COL_TILE = 1024
MERGE_ROW_TILE = 256
FINAL_ROW_TILE = 512
MEM_ROW_TILE = 512
PROJ_SUB_ROWS = 512
CONV_SUB_ROWS = 512
FINAL_SUB_ROWS = 256
LOG2_E = 1.4426950408889634
INT_MIN = -2 ** 31
KEY_NEG_INF = INT_MIN + 0x7FFFFF
NEG_BIG = -1e30


def _cparams(n_grid):
    return pltpu.CompilerParams(dimension_semantics=("arbitrary",) * n_grid,
                                vmem_limit_bytes=VMEM_LIMIT_BYTES)


def _sigmoid(z):
    return 0.5 * jnp.tanh(0.5 * z) + 0.5


def _silu(z):
    half = 0.5 * z
    return half * jnp.tanh(half) + half


def _dot(a, b):
    return jnp.dot(a, b, preferred_element_type=F32)


def _dot_nt(a, b):
    return lax.dot_general(a, b, (((1,), (1,)), ((), ())), preferred_element_type=F32)


def _pack_kernel(src_ref, *refs):
    del src_ref
    *wt_refs, o_ref = refs
    for k, wt_ref in enumerate(wt_refs):
        o_ref[:, PACK_CHUNK * k:PACK_CHUNK * (k + 1)] = wt_ref[...].T.astype(o_ref.dtype)


def _pack_w_in(w):
    wt = jnp.transpose(w.reshape(D_MODEL, IN_COLS))
    per_step = PACK_CHUNKS_PER_STEP
    window = lambda k: pl.BlockSpec((pl.Element(PACK_CHUNK), pl.Element(D_MODEL)),
                                    lambda i, src: (src[per_step * i + k] * PACK_ROW_ALIGN, 0))
    grid_spec = pltpu.PrefetchScalarGridSpec(
        num_scalar_prefetch=1,
        grid=(PACKED_COLS // (PACK_CHUNK * per_step),),
        in_specs=[window(k) for k in range(per_step)],
        out_specs=pl.BlockSpec((D_MODEL, PACK_CHUNK * per_step), lambda i, src: (0, i)))
    return pl.pallas_call(
        _pack_kernel,
        grid_spec=grid_spec,
        out_shape=jax.ShapeDtypeStruct((D_MODEL, PACKED_COLS), BF16),
        compiler_params=_cparams(1),
        name="pack_w_in",
    )(jnp.asarray([s // PACK_ROW_ALIGN for s in PACK_SOURCE], jnp.int32), *([wt] * per_step))


def _proj_kernel(*refs, conv):
    if conv is None:
        x_ref, g_ref, w_ref, o_ref, h_ref = refs
    else:
        x_ref, g_ref, w_ref, cw_ref, cb_ref, o_ref, tail_ref, h_ref, pad_ref, halo_ref = refs
    i = pl.program_id(0)
    j = pl.program_id(1)

    @pl.when(j == 0)
    def _():
        x = x_ref[...]
        ms = jnp.mean(x * x, axis=-1, keepdims=True)
        h_ref[...] = (x * lax.rsqrt(ms + EPS) * g_ref[...]).astype(BF16)

    if conv is not None:
        @pl.when(jnp.logical_and(i == 0, j == 0))
        def _():
            halo_ref[...] = jnp.zeros(halo_ref.shape, F32)

    tm = o_ref.shape[0]

    def plain():
        sub = min(tm, PROJ_SUB_ROWS)
        for r in range(tm // sub):
            o_ref[r * sub:(r + 1) * sub, :] = _dot(h_ref[r * sub:(r + 1) * sub, :], w_ref[...])

    if conv is None:
        plain()
        return
    first_tile, n_conv_tiles, tiles_per_seq = conv
    is_conv = jnp.logical_and(j >= first_tile, j < first_tile + n_conv_tiles)
    pl.when(jnp.logical_not(is_conv))(plain)

    @pl.when(is_conv)
    def _():
        c = j - first_tile
        seq_start = (i % tiles_per_seq) == 0
        sub = pad_ref.shape[1] - SUBLANES
        n_sub = tm // sub
        for r in range(n_sub + 1):
            if r < n_sub:
                slot = r % 2
                if r == 0:
                    pad_ref[slot, 0:SUBLANES, :] = jnp.where(seq_start, 0.0, halo_ref[c])
                else:
                    pad_ref[slot, 0:SUBLANES, :] = pad_ref[1 - slot, sub:sub + SUBLANES, :]
                pad_ref[slot, SUBLANES:SUBLANES + sub, :] = _dot(h_ref[r * sub:(r + 1) * sub, :], w_ref[...])
            if r >= 1:
                hist = pad_ref[(r - 1) % 2]
                acc = cb_ref[...] + cw_ref[B_CONV - 1:B_CONV, :] * hist[SUBLANES:, :]
                for k in range(1, B_CONV):
                    shifted = pltpu.roll(hist, k, axis=0)[SUBLANES:, :]
                    acc = acc + cw_ref[B_CONV - 1 - k:B_CONV - k, :] * shifted
                o_ref[(r - 1) * sub:r * sub, :] = _silu(acc)
        last_rows = pad_ref[(n_sub - 1) % 2, sub:sub + SUBLANES, :]
        halo_ref[c] = last_rows
        tail_ref[...] = last_rows


def _rms_proj(x, g, w, tm, tn, conv=None):
    m, d = x.shape
    n = w.shape[1]
    in_specs = [pl.BlockSpec((tm, d), lambda i, j: (i, 0)),
                pl.BlockSpec((1, d), lambda i, j: (0, 0)),
                pl.BlockSpec((d, tn), lambda i, j: (0, j))]
    args = [x, g.reshape(1, d), w]
    out_specs = pl.BlockSpec((tm, tn), lambda i, j: (i, j))
    out_shape = jax.ShapeDtypeStruct((m, n), F32)
    scratch = [pltpu.VMEM((tm, d), BF16)]
    kern_conv = None
    if conv is not None:
        conv_w, conv_b, t_len = conv
        assert COL_XBC % tn == 0 and B_CONV_DIM % tn == 0 and t_len % tm == 0
        first, count = COL_XBC // tn, B_CONV_DIM // tn
        ctile = lambda i, j: jnp.clip(j - first, 0, count - 1)
        in_specs += [pl.BlockSpec((B_CONV, tn), lambda i, j: (0, ctile(i, j))),
                     pl.BlockSpec((1, tn), lambda i, j: (0, ctile(i, j)))]
        args += [conv_w, conv_b.reshape(1, B_CONV_DIM)]
        out_specs = [out_specs, pl.BlockSpec((None, SUBLANES, tn), lambda i, j: (i, 0, ctile(i, j)))]
        out_shape = [out_shape, jax.ShapeDtypeStruct((m // tm, SUBLANES, B_CONV_DIM), F32)]
        scratch += [pltpu.VMEM((2, SUBLANES + min(tm, CONV_SUB_ROWS), tn), F32),
                    pltpu.VMEM((count, SUBLANES, tn), F32)]
        kern_conv = (first, count, t_len // tm)
    return pl.pallas_call(
        functools.partial(_proj_kernel, conv=kern_conv),
        grid=(m // tm, n // tn),
        in_specs=in_specs,
        out_specs=out_specs,
        out_shape=out_shape,
        scratch_shapes=scratch,
        compiler_params=_cparams(2),
        name="rms_proj",
    )(*args)


def _key_to_f32(key):
    bits = jnp.where(key >= 0, key, key ^ jnp.int32(0x7FFFFFFF))
    return pltpu.bitcast(bits, F32)


def _dsa_kernel(q_ref, iq_ref, ikw_ref, az_ref, k_ref, v_ref, kikw_ref, o_ref,
                kb_ref, vb_ref, kie_ref, kio_ref, iqb_ref, wt_ref, qs_ref, sc_ref, bias_ref,
                cnt_ref, gt_ref, m_ref, ans_ref, s_ref, p_ref, acc_ref, stat_ref, *pad_refs,
                tq, tq_in, nq, n_tiles, n_valid_keys, pos0, topk):
    jq = pl.program_id(1)
    kt_sz = KEY_TILE
    topk_f = float(topk)

    @pl.when(jq == 0)
    def _():
        for kt in range(n_tiles):
            tile = slice(kt_sz * kt, kt_sz * (kt + 1))
            kb_ref[kt] = k_ref[tile, :].astype(BF16)
            v_t = v_ref[tile, :].T.astype(BF16)
            for g in range(A_KV_HEADS):
                vb_ref[kt, g] = v_t[A_HEAD_DIM * g:A_HEAD_DIM * (g + 1), :]
            ki = kikw_ref[tile, :]
            lane = lax.broadcasted_iota(jnp.int32, ki.shape, 1)
            kie = jnp.where(lane < IDX_DIM, ki, 0.0)
            kie_ref[kt] = kie.astype(BF16)
            kio_ref[kt] = pltpu.roll(kie, IDX_DIM, axis=1).astype(BF16)

    if tq_in < tq:
        qp_ref, iqp_ref, ikwp_ref, azp_ref = pad_refs
        for dst, src in ((qp_ref, q_ref), (iqp_ref, iq_ref), (ikwp_ref, ikw_ref), (azp_ref, az_ref)):
            dst[...] = jnp.zeros(dst.shape, dst.dtype)
            dst[0:tq_in, :] = src[...]
        q_ref, iq_ref, ikw_ref, az_ref = qp_ref, iqp_ref, ikwp_ref, azp_ref

    iqb_ref[...] = iq_ref[...].astype(BF16)
    wt_ref[...] = ikw_ref[...].T * IDX_SCALE
    q = (q_ref[...] * ((A_HEAD_DIM ** -0.5) * LOG2_E)).astype(BF16)
    for head in range(A_HEADS):
        g, hh = divmod(head, A_GROUP)
        qs_ref[g, hh * tq:(hh + 1) * tq, :] = q[:, A_HEAD_DIM * head:A_HEAD_DIM * (head + 1)]

    q_last = pos0 + (jq + 1) * tq_in - 1
    key_end = jnp.minimum((q_last // CHUNK + 1) * CHUNK, n_valid_keys)
    n_need = (key_end + kt_sz - 1) // kt_sz

    def for_needed_tiles(body):
        def step(kt, carry):
            body(kt)
            return carry

        lax.fori_loop(0, n_need, step, 0)

    krow = lax.broadcasted_iota(jnp.int32, (kt_sz, tq), 0)
    qcol = lax.broadcasted_iota(jnp.int32, (kt_sz, tq), 1)
    qchunk = jnp.right_shift(pos0 + jq * tq_in + qcol, CHUNK_SHIFT)

    def score_tile(kt):
        acc = None
        for pair in range(IDX_HEADS // 2):
            iq_pair = iqb_ref[:, LANES * pair:LANES * (pair + 1)]
            for half, kref in enumerate((kie_ref, kio_ref)):
                head = 2 * pair + half
                d = _dot_nt(kref[kt], iq_pair)
                contrib = jnp.maximum(d, 0.0) * wt_ref[IDX_DIM + head:IDX_DIM + head + 1, :]
                acc = contrib if acc is None else acc + contrib
        kpos = kt * kt_sz + krow
        if n_tiles * kt_sz > n_valid_keys:
            acc = jnp.where(kpos < n_valid_keys, acc, -jnp.inf)
        sc_ref[kt] = jnp.where(jnp.right_shift(kpos, CHUNK_SHIFT) <= qchunk, acc, -jnp.inf)

    for_needed_tiles(score_tile)

    def count_into(ref, kt, hit):
        ref[...] += jnp.sum(hit.reshape(kt_sz // SUBLANES, SUBLANES, tq), axis=0)

    def total(ref):
        return jnp.sum(ref[...], axis=0, keepdims=True)

    def run_bisection(n_static):
        def bisect(i, carry):
            ans, cnt_ans = carry
            cand = ans + jnp.left_shift(jnp.int32(1), jnp.int32(31) - i)
            cand_f = _key_to_f32(jnp.maximum(cand, jnp.int32(KEY_NEG_INF)))
            part = jnp.zeros((SUBLANES, tq), F32)
            for kt in range(n_static):
                hit = jnp.where(sc_ref[kt] >= cand_f, 1.0, 0.0)
                part = part + jnp.sum(hit.reshape(kt_sz // SUBLANES, SUBLANES, tq), axis=0)
            cnt = jnp.sum(part, axis=0, keepdims=True)
            keep = cnt >= topk_f
            return jnp.where(keep, cand, ans), jnp.where(keep, cnt, cnt_ans)

        init = (jnp.full((1, tq), INT_MIN, jnp.int32), jnp.zeros((1, tq), F32))
        ans, cnt_ans = lax.fori_loop(0, 32, bisect, init)
        ans_ref[...] = jnp.broadcast_to(ans, ans_ref.shape)
        cnt_ref[...] = jnp.broadcast_to(cnt_ans, cnt_ref.shape)

    need_of = []
    for j in range(nq):
        end = min(((pos0 + (j + 1) * tq_in - 1) // CHUNK + 1) * CHUNK, n_valid_keys)
        need_of.append(-(-end // kt_sz))

    def per_tile_count(body):
        for n_static in sorted(set(need_of)):
            blocks = [j for j in range(nq) if need_of[j] == n_static]
            if len(blocks) == nq:
                body(n_static)
            else:
                in_range = jnp.logical_and(jq >= blocks[0], jq <= blocks[-1])
                pl.when(in_range)(functools.partial(body, n_static))

    per_tile_count(run_bisection)
    ans = ans_ref[0:1, :]
    thr = _key_to_f32(jnp.maximum(ans, jnp.int32(KEY_NEG_INF)))

    cnt_ge = cnt_ref[0:1, :]
    qcol1 = lax.broadcasted_iota(jnp.int32, (1, tq), 1)
    tie = jnp.where(cnt_ge > topk_f, jnp.where(thr > -jnp.inf, jnp.where(qcol1 < tq_in, 1.0, 0.0), 0.0), 0.0)
    m_ref[...] = jnp.full(m_ref.shape, n_tiles * kt_sz, jnp.int32)

    @pl.when(jnp.max(tie) > 0.0)
    def _():
        nbits = max(1, (n_tiles * kt_sz - 1).bit_length())
        gt_ref[...] = jnp.zeros(gt_ref.shape, F32)
        for_needed_tiles(lambda kt: count_into(gt_ref, kt, jnp.where(sc_ref[kt] > thr, 1.0, 0.0)))
        cnt_gt = total(gt_ref)

        def bisect_pos(i, t):
            cand = t + jnp.left_shift(jnp.int32(1), jnp.int32(nbits - 1) - i)
            cnt_ref[...] = jnp.zeros(cnt_ref.shape, F32)

            def count_below(kt):
                hit = jnp.where(sc_ref[kt] == thr, jnp.where(kt * kt_sz + krow < cand, 1.0, 0.0), 0.0)
                count_into(cnt_ref, kt, hit)

            for_needed_tiles(count_below)
            return jnp.where(cnt_gt + total(cnt_ref) < topk_f, cand, t)

        t_last = lax.fori_loop(0, nbits, bisect_pos, jnp.zeros((1, tq), jnp.int32))
        m_ref[...] = jnp.broadcast_to(t_last, m_ref.shape)

    m_last = m_ref[0:1, :]
    thr_eq = jnp.where(thr > -jnp.inf, thr, jnp.inf)

    def bias_tile(kt):
        sc = sc_ref[kt]
        kpos = kt * kt_sz + krow
        tied = jnp.where(sc == thr_eq, jnp.where(kpos <= m_last, 0.0, NEG_BIG), NEG_BIG)
        bias_ref[kt] = jnp.where(sc > thr, 0.0, tied)

    for_needed_tiles(bias_tile)

    cols = A_GROUP * tq
    groups = range(A_KV_HEADS)
    gsl = [slice(A_HEAD_DIM * g, A_HEAD_DIM * (g + 1)) for g in groups]
    ROW_MAX, ROW_SUM, ROW_ALPHA, ROW_TILE_MAX = 0, 1, 2, 3

    def stat(g, k):
        return stat_ref[SUBLANES * g + k:SUBLANES * g + k + 1, :]

    def set_stat(g, k, v):
        stat_ref[SUBLANES * g + k:SUBLANES * g + k + 1, :] = v

    def produce(kt, slot):
        bias = jnp.concatenate([bias_ref[kt]] * A_GROUP, axis=1)
        for g in groups:
            s = _dot_nt(kb_ref[kt, :, gsl[g]], qs_ref[g]) + bias
            s_ref[slot, g] = s
            set_stat(g, ROW_TILE_MAX + slot, jnp.max(s, axis=0, keepdims=True))

    def consume(kt, slot):
        k_prev = jnp.maximum(kt - 1, 0)
        for g in groups:
            pv = _dot(vb_ref[k_prev, g], p_ref[g])
            acc_ref[g] = acc_ref[g] * stat(g, ROW_ALPHA) + pv
            m_run = stat(g, ROW_MAX)
            m = jnp.maximum(m_run, stat(g, ROW_TILE_MAX + slot))
            alpha = jnp.exp2(m_run - m)
            p = jnp.exp2(s_ref[slot, g] - m)
            set_stat(g, ROW_SUM, alpha * stat(g, ROW_SUM) + jnp.sum(p, axis=0, keepdims=True))
            p_ref[g] = p.astype(BF16)
            set_stat(g, ROW_MAX, m)
            set_stat(g, ROW_ALPHA, alpha)

    acc_ref[...] = jnp.zeros(acc_ref.shape, F32)
    p_ref[...] = jnp.zeros(p_ref.shape, BF16)
    for g in groups:
        set_stat(g, ROW_MAX, jnp.full((1, cols), NEG_BIG, F32))
        set_stat(g, ROW_ALPHA, jnp.ones((1, cols), F32))
        set_stat(g, ROW_SUM, jnp.zeros((1, cols), F32))
    produce(0, 0)

    def tile_pair(i, carry):
        k0 = 2 * i
        produce(jnp.minimum(k0 + 1, n_need - 1), 1)
        consume(k0, 0)

        @pl.when(k0 + 1 < n_need)
        def _():
            produce(jnp.minimum(k0 + 2, n_need - 1), 0)
            consume(k0 + 1, 1)

        return carry

    lax.fori_loop(0, (n_need + 1) // 2, tile_pair, 0)
    for g in groups:
        acc = acc_ref[g] * stat(g, ROW_ALPHA) + _dot(vb_ref[n_need - 1, g], p_ref[g])
        o = (acc / stat(g, ROW_SUM)).T
        for hh in range(A_GROUP):
            hs = slice(A_HEAD_DIM * (A_GROUP * g + hh), A_HEAD_DIM * (A_GROUP * g + hh + 1))
            res = (o[hh * tq:(hh + 1) * tq, :] * _silu(az_ref[:, hs])).astype(o_ref.dtype)
            o_ref[:, hs] = res[0:tq_in, :]


def _dsa(u, keys, *, n_batch, t_len, n_valid_keys, pos0):
    n_keys = t_len if keys is None else keys[0].shape[1]
    assert n_keys % KEY_TILE == 0
    n_tiles = n_keys // KEY_TILE
    topk = min(TOPK_MAX, n_valid_keys // 4)
    if t_len % QUERY_TILE == 0:
        tq = tq_in = QUERY_TILE
    elif t_len % LANES == 0:
        tq = tq_in = LANES
    else:
        tq, tq_in = LANES, t_len
    nq = t_len // tq_in
    row = lambda b, j: b * nq + j
    if keys is None:
        key_args = (u, u, u)
        key_specs = [pl.BlockSpec((n_keys, A_KV_WIDTH), lambda b, j: (b, COL_AK // A_KV_WIDTH)),
                     pl.BlockSpec((n_keys, A_KV_WIDTH), lambda b, j: (b, COL_AV // A_KV_WIDTH)),
                     pl.BlockSpec((n_keys, LANES), lambda b, j: (b, COL_IKW // LANES))]
    else:
        key_args = keys
        key_specs = [pl.BlockSpec((None, n_keys, t.shape[-1]), lambda b, j: (b, 0, 0)) for t in keys]
    pad_scratch = []
    if tq_in < tq:
        pad_scratch = [pltpu.VMEM((tq, A_WIDTH), F32), pltpu.VMEM((tq, IDX_HEADS * IDX_DIM), F32),
                       pltpu.VMEM((tq, LANES), F32), pltpu.VMEM((tq, A_WIDTH), F32)]
    kern = functools.partial(_dsa_kernel, tq=tq, tq_in=tq_in, nq=nq, n_tiles=n_tiles,
                             n_valid_keys=n_valid_keys, pos0=pos0, topk=topk)
    return pl.pallas_call(
        kern,
        grid=(n_batch, nq),
        in_specs=[pl.BlockSpec((tq_in, A_WIDTH), lambda b, j: (row(b, j), COL_AQ // A_WIDTH)),
                  pl.BlockSpec((tq_in, A_WIDTH), lambda b, j: (row(b, j), COL_IQ // A_WIDTH)),
                  pl.BlockSpec((tq_in, LANES), lambda b, j: (row(b, j), COL_IKW // LANES)),
                  pl.BlockSpec((tq_in, A_WIDTH), lambda b, j: (row(b, j), COL_AZ // A_WIDTH)),
                  *key_specs],
        out_specs=pl.BlockSpec((tq_in, A_WIDTH), lambda b, j: (row(b, j), 0)),
        out_shape=jax.ShapeDtypeStruct((n_batch * t_len, A_WIDTH), BF16),
        scratch_shapes=[pltpu.VMEM((n_tiles, KEY_TILE, A_KV_WIDTH), BF16),
                        pltpu.VMEM((n_tiles, A_KV_HEADS, A_HEAD_DIM, KEY_TILE), BF16),
                        pltpu.VMEM((n_tiles, KEY_TILE, LANES), BF16),
                        pltpu.VMEM((n_tiles, KEY_TILE, LANES), BF16),
                        pltpu.VMEM((tq, IDX_HEADS * IDX_DIM), BF16),
                        pltpu.VMEM((LANES, tq), F32),
                        pltpu.VMEM((A_KV_HEADS, A_GROUP * tq, A_HEAD_DIM), BF16),
                        pltpu.VMEM((n_tiles, KEY_TILE, tq), F32),
                        pltpu.VMEM((n_tiles, KEY_TILE, tq), F32),
                        pltpu.VMEM((SUBLANES, tq), F32), pltpu.VMEM((SUBLANES, tq), F32),
                        pltpu.VMEM((SUBLANES, tq), jnp.int32),
                        pltpu.VMEM((SUBLANES, tq), jnp.int32),
                        pltpu.VMEM((2, A_KV_HEADS, KEY_TILE, A_GROUP * tq), F32),
                        pltpu.VMEM((A_KV_HEADS, KEY_TILE, A_GROUP * tq), BF16),
                        pltpu.VMEM((A_KV_HEADS, A_HEAD_DIM, A_GROUP * tq), F32),
                        pltpu.VMEM((A_KV_HEADS * SUBLANES, A_GROUP * tq), F32)] + pad_scratch,
        compiler_params=_cparams(2),
        name="dsa",
    )(u, u, u, u, *key_args)


def _expand_heads(v, e):
    hi = v.astype(BF16)
    lo = (v - hi.astype(F32)).astype(BF16)
    return _dot(hi, e) + _dot(lo, e)


def _mamba_kernel(*refs, lc, n_in, has_init, conv_done):
    z_ref, xbc_ref, dt_ref, *rest = refs
    if conv_done:
        tail_ref, *rest = rest
    else:
        cw_ref, cb_ref, *rest = rest
    dtb_ref, alog_ref, dsk_ref, nrm_ref, e_ref, *rest = rest
    if has_init:
        cprev_ref, h0_ref, *rest = rest
    y_ref, cst_ref, hl_ref, xpad_ref, st_ref, yacc_ref, xbc_ref2, x1_ref, xd_ref, *pad_refs = rest
    c = pl.program_id(1)
    n_chunks = pl.num_programs(1)

    @pl.when(c == 0)
    def _():
        xpad_ref[...] = jnp.zeros(xpad_ref.shape, F32)
        if has_init:
            xpad_ref[0:SUBLANES, :] = cprev_ref[...]
            for g in range(B_GROUPS):
                st_ref[g] = h0_ref[B_GROUP_W * g:B_GROUP_W * (g + 1), :].T
        else:
            st_ref[...] = jnp.zeros(st_ref.shape, F32)

    if conv_done:
        xbc_act = xbc_ref

        @pl.when(c == n_chunks - 1)
        def _():
            cst_ref[...] = tail_ref[SUBLANES - (B_CONV - 1):SUBLANES, :]
    else:
        xpad_ref[SUBLANES:SUBLANES + n_in, :] = xbc_ref[...]
        acc = cb_ref[...] + cw_ref[0:1, :] * xpad_ref[SUBLANES - 3:SUBLANES - 3 + lc, :]
        for j in range(1, B_CONV):
            acc = acc + cw_ref[j:j + 1, :] * xpad_ref[SUBLANES - 3 + j:SUBLANES - 3 + j + lc, :]
        xbc_ref2[...] = _silu(acc)
        xbc_act = xbc_ref2

        @pl.when(c == n_chunks - 1)
        def _():
            cst_ref[...] = xpad_ref[SUBLANES + n_in - 3:SUBLANES + n_in, :]

        xpad_ref[0:SUBLANES, :] = xpad_ref[lc:lc + SUBLANES, :]

    if n_in < lc:
        dtp_ref, zp_ref = pad_refs
        dtp_ref[...] = jnp.zeros(dtp_ref.shape, F32)
        dtp_ref[0:n_in, :] = dt_ref[...]
        zp_ref[...] = jnp.zeros(zp_ref.shape, F32)
        zp_ref[0:n_in, :] = z_ref[...]
        dt_raw = dtp_ref[...]
        z_all = zp_ref
    else:
        dt_raw = dt_ref[...]
        z_all = z_ref
    pre = dt_raw + dtb_ref[...]
    dt = jnp.maximum(pre, 0.0) + jnp.log1p(jnp.exp(-jnp.abs(pre)))
    row = lax.broadcasted_iota(jnp.int32, (lc, LANES), 0)
    dt = jnp.where(lax.broadcasted_iota(jnp.int32, (lc, LANES), 1) < B_HEADS, dt, 0.0)
    if n_in < lc:
        dt = jnp.where(row < n_in, dt, 0.0)
    a = -jnp.exp(alog_ref[...])
    cum = dt * (a * LOG2_E)
    shift = 1
    while shift < lc:
        cum = cum + jnp.where(row >= shift, pltpu.roll(cum, shift, axis=0), 0.0)
        shift *= 2
    cum_t = cum.T
    dt_t = dt.T
    c_last = cum[lc - 1:lc, :]
    e = e_ref[...]
    x1_ref[...] = _expand_heads(jnp.exp2(cum), e)
    x2 = _expand_heads(jnp.exp2(c_last - cum) * dt, e)
    x3 = _expand_heads(jnp.broadcast_to(jnp.exp2(c_last), (SUBLANES, LANES)), e)[0:1, :]

    xd_ref[...] = (xbc_act[:, 0:B_WIDTH] * x2).astype(BF16)
    li = lax.broadcasted_iota(jnp.int32, (lc, lc), 0)
    si = lax.broadcasted_iota(jnp.int32, (lc, lc), 1)
    causal = li >= si
    lane = lax.broadcasted_iota(jnp.int32, (lc, LANES), 1)
    dsk = dsk_ref[...]
    for g in range(B_GROUPS):
        bg = xbc_act[:, B_WIDTH + B_STATE * g:B_WIDTH + B_STATE * (g + 1)]
        cg = xbc_act[:, B_WIDTH + B_GROUPS * B_STATE + B_STATE * g:B_WIDTH + B_GROUPS * B_STATE + B_STATE * (g + 1)]
        bgb = bg.astype(BF16)
        cgb = cg.astype(BF16)
        cb = _dot_nt(cgb, bgb)
        gs = slice(B_GROUP_W * g, B_GROUP_W * (g + 1))
        state = st_ref[g]
        y_off = _dot(cgb, state.astype(BF16)) * x1_ref[:, gs]
        for pp in range(B_GROUP_W // LANES):
            col = B_GROUP_W * g + LANES * pp
            xp = xbc_act[:, col:col + LANES]
            y_pair = y_off[:, LANES * pp:LANES * (pp + 1)] + dsk[:, col:col + LANES] * xp
            xpb = xp.astype(BF16)
            y_half = []
            for half in range(2):
                head = col // B_HEAD_DIM + half
                seg = cum[:, head:head + 1] - cum_t[head:head + 1, :]
                wgt = cb * jnp.exp2(jnp.where(causal, seg, NEG_BIG)) * dt_t[head:head + 1, :]
                y_half.append(_dot(wgt.astype(BF16), xpb))
            y_pair = y_pair + jnp.where(lane < B_HEAD_DIM, y_half[0], y_half[1])
            yacc_ref[:, col:col + LANES] = y_pair
        st_ref[g] = state * x3[:, gs] + _dot(bg.T.astype(BF16), xd_ref[:, gs])

    for g in range(B_GROUPS):
        gs = slice(B_GROUP_W * g, B_GROUP_W * (g + 1))
        yg = yacc_ref[:, gs] * _silu(z_all[:, gs])
        ms = jnp.mean(yg * yg, axis=-1, keepdims=True)
        out = (yg * lax.rsqrt(ms + EPS) * nrm_ref[:, gs]).astype(y_ref.dtype)
        y_ref[:, gs] = out[0:n_in, :]

    @pl.when(c == n_chunks - 1)
    def _():
        for g in range(B_GROUPS):
            hl_ref[B_GROUP_W * g:B_GROUP_W * (g + 1), :] = st_ref[g].T


def _mamba(u, tails, conv_w, conv_b, dt_bias, a_log, d_skip, ssm_norm, conv_prev, h0, *, n_batch, t_len):
    lc = LANES
    if t_len % lc == 0:
        n_in = lc
    else:
        assert t_len < lc
        n_in = t_len
    nc = t_len // n_in
    has_init = conv_prev is not None
    row = lambda b, c: b * nc + c
    pad1 = lambda v: jnp.concatenate([v.astype(F32), jnp.zeros((LANES - B_HEADS,), F32)]).reshape(1, LANES)
    head_of_col = jnp.arange(B_WIDTH, dtype=jnp.int32) // B_HEAD_DIM
    expand = (jnp.arange(LANES, dtype=jnp.int32)[:, None] == head_of_col[None, :]).astype(BF16)
    dsk_row = jnp.repeat(d_skip.astype(F32), B_HEAD_DIM).reshape(1, B_WIDTH)
    const = lambda shape: pl.BlockSpec(shape, lambda b, c: (0,) * len(shape))
    conv_done = tails is not None
    z_spec = pl.BlockSpec((n_in, B_WIDTH), lambda b, c: (row(b, c), COL_BZ // B_WIDTH))
    dt_spec = pl.BlockSpec((n_in, LANES), lambda b, c: (row(b, c), COL_DT // LANES))
    if conv_done:
        assert not has_init and n_in == lc
        tiles_per_seq = tails.shape[0] // n_batch
        in_specs = [z_spec, pl.BlockSpec((n_in, B_CONV_DIM), lambda b, c: (row(b, c), COL_XBC // B_CONV_DIM)), dt_spec,
                    pl.BlockSpec((None, SUBLANES, B_CONV_DIM), lambda b, c: ((b + 1) * tiles_per_seq - 1, 0, 0))]
        args = [u, u, u, tails]
    else:
        in_specs = [z_spec, pl.BlockSpec((n_in, B_CONV_DIM), lambda b, c: (row(b, c), COL_XBC // B_CONV_DIM)), dt_spec,
                    const((B_CONV, B_CONV_DIM)), const((1, B_CONV_DIM))]
        args = [u, u, u, conv_w, conv_b.reshape(1, B_CONV_DIM)]
    in_specs += [const((1, LANES)), const((1, LANES)), const((1, B_WIDTH)), const((1, B_WIDTH)), const((LANES, B_WIDTH))]
    args += [pad1(dt_bias), pad1(a_log), dsk_row, ssm_norm.reshape(1, B_WIDTH), expand]
    if has_init:
        cprev8 = jnp.concatenate([jnp.zeros((n_batch, SUBLANES - (B_CONV - 1), B_CONV_DIM), F32), conv_prev], axis=1)
        in_specs += [pl.BlockSpec((None, SUBLANES, B_CONV_DIM), lambda b, c: (b, 0, 0)),
                     pl.BlockSpec((None, B_WIDTH, B_STATE), lambda b, c: (b, 0, 0))]
        args += [cprev8, h0.reshape(n_batch, B_WIDTH, B_STATE)]
    pad_scratch = []
    if n_in < lc:
        pad_scratch = [pltpu.VMEM((lc, LANES), F32), pltpu.VMEM((lc, B_WIDTH), F32)]
    kern = functools.partial(_mamba_kernel, lc=lc, n_in=n_in, has_init=has_init, conv_done=conv_done)
    y, cst, hl = pl.pallas_call(
        kern,
        grid=(n_batch, nc),
        in_specs=in_specs,
        out_specs=[pl.BlockSpec((n_in, B_WIDTH), lambda b, c: (row(b, c), 0)),
                   pl.BlockSpec((None, B_CONV - 1, B_CONV_DIM), lambda b, c: (b, 0, 0)),
                   pl.BlockSpec((None, B_WIDTH, B_STATE), lambda b, c: (b, 0, 0))],
        out_shape=[jax.ShapeDtypeStruct((n_batch * t_len, B_WIDTH), BF16),
                   jax.ShapeDtypeStruct((n_batch, B_CONV - 1, B_CONV_DIM), F32),
                   jax.ShapeDtypeStruct((n_batch, B_WIDTH, B_STATE), F32)],
        scratch_shapes=[pltpu.VMEM((SUBLANES + lc, B_CONV_DIM), F32),
                        pltpu.VMEM((B_GROUPS, B_STATE, B_GROUP_W), F32),
                        pltpu.VMEM((lc, B_WIDTH), F32),
                        pltpu.VMEM((lc, B_CONV_DIM), F32), pltpu.VMEM((lc, B_WIDTH), F32),
                        pltpu.VMEM((lc, B_WIDTH), BF16)] + pad_scratch,
        compiler_params=_cparams(2),
        name="mamba",
    )(*args)
    return y, cst, hl.reshape(n_batch, B_HEADS, B_HEAD_DIM, B_STATE)


def _mem_kernel(q_ref, z_ref, k_ref, v_ref, o_ref):
    q = q_ref[...].astype(BF16)
    scale = M_HEAD_DIM ** -0.5
    for head in range(M_HEADS):
        hs = slice(M_HEAD_DIM * head, M_HEAD_DIM * (head + 1))
        s = _dot_nt(q[:, hs], k_ref[:, hs].astype(BF16)) * scale
        s_max = jnp.max(s, axis=-1, keepdims=True)
        p = jnp.exp(s - s_max)
        denom = jnp.sum(p, axis=-1, keepdims=True)
        o = _dot(p.astype(BF16), v_ref[:, hs].astype(BF16)) / denom
        o_ref[:, hs] = (o * _silu(z_ref[:, hs])).astype(o_ref.dtype)


def _mem_attend(u, mk, mv, *, n_batch, t_len):
    tq = MEM_ROW_TILE if t_len % MEM_ROW_TILE == 0 else t_len
    nq = t_len // tq
    row = lambda b, j: b * nq + j
    return pl.pallas_call(
        _mem_kernel,
        grid=(n_batch, nq),
        in_specs=[pl.BlockSpec((tq, M_WIDTH), lambda b, j: (row(b, j), COL_MQ // M_WIDTH)),
                  pl.BlockSpec((tq, M_WIDTH), lambda b, j: (row(b, j), COL_MZ // M_WIDTH)),
                  pl.BlockSpec((None, N_MEM, M_WIDTH), lambda b, j: (b, 0, 0)),
                  pl.BlockSpec((None, N_MEM, M_WIDTH), lambda b, j: (b, 0, 0))],
        out_specs=pl.BlockSpec((tq, M_WIDTH), lambda b, j: (row(b, j), 0)),
        out_shape=jax.ShapeDtypeStruct((n_batch * t_len, M_WIDTH), BF16),
        compiler_params=_cparams(2),
        name="mem_attend",
    )(u, u, mk, mv)


def _merge_kernel(ya_ref, yb_ref, ym_ref, ga_ref, gb_ref, gm_ref, wa_ref, wb_ref, wm_ref, o_ref):
    merged = _sigmoid(ga_ref[...]) * _dot(ya_ref[...], wa_ref[...])
    merged = merged + _sigmoid(gb_ref[...]) * _dot(yb_ref[...], wb_ref[...])
    merged = merged + _sigmoid(gm_ref[...]) * _dot(ym_ref[...], wm_ref[...])
    o_ref[...] = merged.astype(o_ref.dtype)


def _merge(ya, yb, ym, u, w_pa, w_pb, w_pm, tm):
    n = ya.shape[0]
    rows = lambda width: pl.BlockSpec((tm, width), lambda i: (i, 0))
    gate = lambda k: pl.BlockSpec((tm, D_MODEL), lambda i: (i, COL_GATES // D_MODEL + k))
    weight = lambda width: pl.BlockSpec((width, D_MODEL), lambda i: (0, 0), pipeline_mode=pl.Buffered(1))
    return pl.pallas_call(
        _merge_kernel,
        grid=(n // tm,),
        in_specs=[rows(A_WIDTH), rows(B_WIDTH), rows(M_WIDTH), gate(0), gate(1), gate(2),
                  weight(A_WIDTH), weight(B_WIDTH), weight(M_WIDTH)],
        out_specs=rows(D_MODEL),
        out_shape=jax.ShapeDtypeStruct((n, D_MODEL), BF16),
        compiler_params=_cparams(1),
        name="merge",
    )(ya, yb, ym, u, u, u, w_pa, w_pb, w_pm)


def _final_kernel(m_ref, x_ref, wo_ref, g_ref, o_ref):
    tm = o_ref.shape[0]
    sub = min(tm, FINAL_SUB_ROWS)
    for r in range(tm // sub):
        rs = slice(r * sub, (r + 1) * sub)
        y = x_ref[rs, :] + _dot(m_ref[rs, :], wo_ref[...])
        ms = jnp.mean(y * y, axis=-1, keepdims=True)
        o_ref[rs, :] = y * lax.rsqrt(ms + EPS) * g_ref[...]


def _final(merged, x, w_o, g, tm):
    n = x.shape[0]
    rows = pl.BlockSpec((tm, D_MODEL), lambda i: (i, 0))
    return pl.pallas_call(
        _final_kernel,
        grid=(n // tm,),
        in_specs=[rows, rows,
                  pl.BlockSpec((D_MODEL, D_MODEL), lambda i: (0, 0), pipeline_mode=pl.Buffered(1)),
                  pl.BlockSpec((1, D_MODEL), lambda i: (0, 0))],
        out_specs=rows,
        out_shape=jax.ShapeDtypeStruct((n, D_MODEL), F32),
        compiler_params=_cparams(1),
        name="final",
    )(merged, x, w_o, g.reshape(1, D_MODEL))


def _row_tile(n, pref):
    t = pref
    while n % t:
        t //= 2
    return t


def _pad_keys(t, n_keys):
    pad = n_keys - t.shape[1]
    if pad == 0:
        return t
    return jnp.concatenate([t, jnp.zeros((t.shape[0], pad, t.shape[2]), t.dtype)], axis=1)


def _layer(x, pos0, past_k, past_v, past_ki, conv_prev, h0, mem_k, mem_v, lw, norm_final):
    (norm_in, w_in_packed, conv_w, conv_b, dt_bias, a_log, d_skip, ssm_norm, w_pa, w_pb, w_pm, w_o) = lw
    n_batch, t_len, _ = x.shape
    n = n_batch * t_len
    x2 = x.reshape(n, D_MODEL)
    tm = _row_tile(n, PROJ_ROW_TILE)
    if conv_prev is None and t_len % tm == 0:
        u, tails = _rms_proj(x2, norm_in, w_in_packed, tm, PROJ_COL_TILE, conv=(conv_w, conv_b, t_len))
    else:
        u, tails = _rms_proj(x2, norm_in, w_in_packed, tm, PROJ_COL_TILE), None

    k_new = u[:, COL_AK:COL_AK + A_KV_WIDTH].reshape(n_batch, t_len, A_KV_WIDTH)
    v_new = u[:, COL_AV:COL_AV + A_KV_WIDTH].reshape(n_batch, t_len, A_KV_WIDTH)
    kikw_new = u[:, COL_IKW:COL_IKW + LANES].reshape(n_batch, t_len, LANES)
    if past_k is None:
        keys, n_valid_keys = None, t_len
    else:
        n_past = past_k.shape[1]
        past_kikw = jnp.concatenate([past_ki, jnp.zeros((n_batch, n_past, LANES - IDX_DIM), F32)], axis=2)
        n_valid_keys = n_past + t_len
        n_keys = -(-n_valid_keys // KEY_TILE) * KEY_TILE
        keys = (_pad_keys(jnp.concatenate([past_k.reshape(n_batch, n_past, A_KV_WIDTH), k_new], axis=1), n_keys),
                _pad_keys(jnp.concatenate([past_v.reshape(n_batch, n_past, A_KV_WIDTH), v_new], axis=1), n_keys),
                _pad_keys(jnp.concatenate([past_kikw, kikw_new], axis=1), n_keys))
    ya = _dsa(u, keys, n_batch=n_batch, t_len=t_len, n_valid_keys=n_valid_keys, pos0=pos0)

    yb, conv_state, h_last = _mamba(u, tails, conv_w, conv_b, dt_bias, a_log, d_skip, ssm_norm, conv_prev, h0,
                                    n_batch=n_batch, t_len=t_len)
    ym = _mem_attend(u, mem_k, mem_v, n_batch=n_batch, t_len=t_len)
    merged = _merge(ya, yb, ym, u, w_pa, w_pb, w_pm, _row_tile(n, MERGE_ROW_TILE))
    y = _final(merged, x2, w_o, norm_final, _row_tile(n, FINAL_ROW_TILE)).reshape(n_batch, t_len, D_MODEL)
    return (y, k_new.reshape(n_batch, t_len, A_KV_HEADS, A_HEAD_DIM),
            v_new.reshape(n_batch, t_len, A_KV_HEADS, A_HEAD_DIM),
            kikw_new[:, :, 0:IDX_DIM], conv_state, h_last)


def kernel(x_prompt, x_sample, mem_prompt, cache_attn_k, cache_attn_v, cache_idx_k, state_conv, state_ssm,
           cache_mem_k, cache_mem_v, norm_in, w_in, conv_w, conv_b, dt_bias, a_log, d_skip, ssm_norm,
           norm_mem, w_mem_kv, w_pa, w_pb, w_pm, w_o, norm_final):
    depth = w_in.shape[0]
    assert depth == 1, "the final RMSNorm is fused into the (single) layer"
    bp = x_prompt.shape[0]
    bs = x_sample.shape[0]
    first = lambda t: t.reshape(t.shape[1:])
    lw = (first(norm_in), _pack_w_in(w_in), first(conv_w), first(conv_b), first(dt_bias), first(a_log),
          first(d_skip), first(ssm_norm), first(w_pa).astype(BF16), first(w_pb).astype(BF16),
          first(w_pm).astype(BF16), first(w_o).astype(BF16))

    mem2 = mem_prompt.reshape(bp * N_MEM, D_MODEL)
    mkv = _rms_proj(mem2, first(norm_mem), first(w_mem_kv).astype(BF16), _row_tile(bp * N_MEM, PROJ_ROW_TILE),
                    PROJ_COL_TILE)
    mk_p = mkv[:, 0:M_WIDTH].reshape(bp, N_MEM, M_WIDTH)
    mv_p = mkv[:, M_WIDTH:2 * M_WIDTH].reshape(bp, N_MEM, M_WIDTH)

    yp, kp, vp, kip, convp, ssmp = _layer(x_prompt, 0, None, None, None, None, None, mk_p, mv_p, lw, norm_final)
    ys, ks, vs, kis, convs, ssms = _layer(
        x_sample, PAST_LEN, first(cache_attn_k), first(cache_attn_v), first(cache_idx_k), first(state_conv),
        first(state_ssm), first(cache_mem_k).reshape(bs, N_MEM, M_WIDTH),
        first(cache_mem_v).reshape(bs, N_MEM, M_WIDTH), lw, norm_final)

    st = lambda t: t[None]
    return (yp, ys, st(kp), st(vp), st(kip), st(convp), st(ssmp),
            st(mk_p.reshape(bp, N_MEM, M_HEADS, M_HEAD_DIM)), st(mv_p.reshape(bp, N_MEM, M_HEADS, M_HEAD_DIM)),
            st(ks), st(vs), st(kis), st(convs), st(ssms))
```
